```python
import jax, jax.numpy as jnp
from jax import lax
import numpy as np

D_MODEL = 1024
BATCH = 4
SEQ = 8192
DEPTH = 2
DEC_BATCH = 8
DEC_SEQ = 64
PAST_LEN = 4096

CHUNK = 64
Q_BLOCK = 128
ALPHA = (2 * DEPTH) ** 0.25
BETA = (8 * DEPTH) ** -0.25
EPS = 1e-5
ROPE_THETA = 10000.0
RET_HEADS = 4
RET_DK = 64
RET_DV = 128
MLA_HEADS = 8
MLA_Q_LORA = 384
MLA_KV_LORA = 256
MLA_NOPE = 64
MLA_ROPE = 32
MLA_DV = 64
GLA_HEADS = 4
GLA_DK = 128
GLA_DV = 256
GLA_GATE_RANK = 16
GLA_TAU = 16.0
D_FF = 2816
N_BRANCH = 3

IN_SPLITS = (RET_HEADS * RET_DK, RET_HEADS * RET_DK, RET_HEADS * RET_DV, RET_HEADS * RET_DV,
             MLA_Q_LORA, MLA_KV_LORA, MLA_ROPE,
             GLA_HEADS * GLA_DK, GLA_HEADS * GLA_DK, GLA_HEADS * GLA_DV, GLA_GATE_RANK, GLA_HEADS * GLA_DV,
             N_BRANCH * D_MODEL)
D_IN = sum(IN_SPLITS)

kernel_name = "hybrid_retention_mla_gla_streaming_step"


def split_in(h):
    offs = np.cumsum(np.array(IN_SPLITS))[:-1].tolist()
    return jnp.split(h, offs, axis=-1)


def layer_norm(x, g, b):
    xf = x.astype(jnp.float32)
    mu = jnp.mean(xf, axis=-1, keepdims=True)
    var = jnp.mean(jnp.square(xf - mu), axis=-1, keepdims=True)
    return ((xf - mu) * lax.rsqrt(var + EPS) * g + b).astype(x.dtype)


def rms_norm(x, g):
    xf = x.astype(jnp.float32)
    return (xf * lax.rsqrt(jnp.mean(jnp.square(xf), axis=-1, keepdims=True) + EPS) * g).astype(x.dtype)


def head_layer_norm(x, g):
    xf = x.astype(jnp.float32)
    mu = jnp.mean(xf, axis=-1, keepdims=True)
    var = jnp.mean(jnp.square(xf - mu), axis=-1, keepdims=True)
    return ((xf - mu) * lax.rsqrt(var + EPS) * g).astype(x.dtype)


def rope(x, pos):
    half = x.shape[-1] // 2
    inv = ROPE_THETA ** (-jnp.arange(half, dtype=jnp.float32) / half)
    ang = pos.astype(jnp.float32)[:, None] * inv[None, :]
    cos = jnp.cos(ang)[:, None, :]
    sin = jnp.sin(ang)[:, None, :]
    xf = x.astype(jnp.float32)
    x1, x2 = xf[..., :half], xf[..., half:]
    return jnp.concatenate([x1 * cos - x2 * sin, x2 * cos + x1 * sin], axis=-1).astype(x.dtype)


def swiglu(x, w_up, w_down):
    gate, up = jnp.split(x @ w_up, 2, axis=-1)
    return (jax.nn.silu(gate) * up) @ w_down


def chunk_linear_scan(q, k, v, log_a, s0):
    b_, t, h, dk = q.shape
    dv = v.shape[-1]
    lc = min(CHUNK, t)
    n = t // lc

    def to_chunks(a):
        return a.astype(jnp.float32).reshape(b_, n, lc, h, a.shape[-1]).transpose(1, 0, 3, 2, 4)

    causal = jnp.tril(jnp.ones((lc, lc), dtype=bool))

    def step(s, inp):
        qi, ki, vi, ai = inp
        bcum = jnp.cumsum(ai, axis=2)
        q_e = qi * jnp.exp(bcum)
        k_e = ki * jnp.exp(-bcum)
        att = jnp.where(causal, jnp.einsum("bhid,bhjd->bhij", q_e, k_e), 0.0)
        o = jnp.einsum("bhij,bhjv->bhiv", att, vi) + jnp.einsum("bhid,bhdv->bhiv", q_e, s)
        b_last = bcum[:, :, -1:, :]
        s_new = jnp.exp(b_last[:, :, 0, :])[..., None] * s + jnp.einsum(
            "bhjd,bhjv->bhdv", ki * jnp.exp(b_last - bcum), vi)
        return s_new, o

    s_fin, o = lax.scan(step, s0.astype(jnp.float32), (to_chunks(q), to_chunks(k), to_chunks(v), to_chunks(log_a)))
    o = o.transpose(1, 0, 3, 2, 4).reshape(b_, t, h, dv)
    return o.astype(v.dtype), s_fin.astype(v.dtype)


def block_causal_attention(q, k, v, q_pos, k_pos):
    b_, sq, h, dk = q.shape
    dv = v.shape[-1]
    blk = min(Q_BLOCK, sq)
    nb = sq // blk
    qb = q.reshape(b_, nb, blk, h, dk).transpose(1, 0, 2, 3, 4)
    cb = (q_pos // CHUNK).reshape(nb, blk)
    k_chunk = k_pos // CHUNK
    scale = dk ** -0.5

    def one(args):
        qi, ci = args
        s = jnp.einsum("bqhd,bkhd->bhqk", qi, k, preferred_element_type=jnp.float32) * scale
        mask = k_chunk[None, :] <= ci[:, None]
        s = jnp.where(mask[None, None], s, -jnp.inf)
        p = jax.nn.softmax(s, axis=-1).astype(v.dtype)
        return jnp.einsum("bhqk,bkhd->bqhd", p, v)

    out = lax.map(one, (qb, cb))
    return out.transpose(1, 0, 2, 3, 4).reshape(b_, sq, h, dv)


def token_mix(x, pos, past, lp):
    b_, t, _ = x.shape
    (r_q, r_k, r_v, r_g, m_cq, m_ckv, m_kr, g_q, g_k, g_v, g_lr, g_og, br_gate) = split_in(x @ lp["w_in"])
    if past is None:
        s_ret0 = jnp.zeros((b_, RET_HEADS, RET_DK, RET_DV), jnp.float32)
        s_gla0 = jnp.zeros((b_, GLA_HEADS, GLA_DK, GLA_DV), jnp.float32)
    else:
        ckv_past, kr_past, s_ret0, s_gla0 = past

    q = rope(r_q.reshape(b_, t, RET_HEADS, RET_DK), pos)
    k = rope(r_k.reshape(b_, t, RET_HEADS, RET_DK), pos) * (RET_DK ** -0.5)
    v = r_v.reshape(b_, t, RET_HEADS, RET_DV)
    log_gamma = jnp.log(1.0 - 2.0 ** (-5.0 - jnp.arange(RET_HEADS, dtype=jnp.float32)))
    log_a = jnp.broadcast_to(log_gamma[None, None, :, None], (b_, t, RET_HEADS, RET_DK))
    o_r, s_ret = chunk_linear_scan(q, k, v, log_a, s_ret0)
    o_r = head_layer_norm(o_r, lp["ret_gn_g"].reshape(RET_HEADS, RET_DV)).reshape(b_, t, -1) * jax.nn.silu(r_g)
    y_r = o_r @ lp["w_ret_o"]

    cq = rms_norm(m_cq, lp["mla_q_norm_g"])
    qm = (cq @ lp["mla_w_q_up"]).reshape(b_, t, MLA_HEADS, MLA_NOPE + MLA_ROPE)
    qm = jnp.concatenate([qm[..., :MLA_NOPE], rope(qm[..., MLA_NOPE:], pos)], axis=-1)
    ckv_new = rms_norm(m_ckv, lp["mla_kv_norm_g"])
    kr_new = rope(m_kr[:, :, None, :], pos)[:, :, 0, :]
    if past is None:
        ckv_all, kr_all, k_pos = ckv_new, kr_new, pos
    else:
        ckv_all = jnp.concatenate([ckv_past, ckv_new], axis=1)
        kr_all = jnp.concatenate([kr_past, kr_new], axis=1)
        k_pos = jnp.concatenate([jnp.arange(ckv_past.shape[1]), pos])
    sk = ckv_all.shape[1]
    kv = (ckv_all @ lp["mla_w_kv_up"]).reshape(b_, sk, MLA_HEADS, MLA_NOPE + MLA_DV)
    km = jnp.concatenate([kv[..., :MLA_NOPE],
                          jnp.broadcast_to(kr_all[:, :, None, :], (b_, sk, MLA_HEADS, MLA_ROPE))], axis=-1)
    vm = kv[..., MLA_NOPE:]
    o_m = block_causal_attention(qm, km, vm, pos, k_pos).reshape(b_, t, -1)
    y_m = o_m @ lp["w_mla_o"]

    qg = g_q.reshape(b_, t, GLA_HEADS, GLA_DK) * (GLA_DK ** -0.5)
    kg = g_k.reshape(b_, t, GLA_HEADS, GLA_DK)
    vg = g_v.reshape(b_, t, GLA_HEADS, GLA_DV)
    gate_logit = (g_lr @ lp["gla_w_gate_up"] + lp["gla_b_gate"]).astype(jnp.float32)
    log_a_g = (jax.nn.log_sigmoid(gate_logit) / GLA_TAU).reshape(b_, t, GLA_HEADS, GLA_DK)
    o_g, s_gla = chunk_linear_scan(qg, kg, vg, log_a_g, s_gla0)
    o_g = rms_norm(o_g, lp["gla_gn_g"].reshape(GLA_HEADS, GLA_DV)).reshape(b_, t, -1) * jax.nn.silu(g_og)
    y_g = o_g @ lp["w_gla_o"]

    gr, gm, gg = jnp.split(jax.nn.sigmoid(br_gate), N_BRANCH, axis=-1)
    y = (gr * y_r + gm * y_m + gg * y_g) @ lp["w_out"]
    return y, (ckv_new, kr_new, s_ret, s_gla)


def layer(x, pos, past, lp):
    x = layer_norm(ALPHA * x + 0.5 * swiglu(x, lp["ffn1_up"], lp["ffn1_down"]), lp["ln_g"][0], lp["ln_b"][0])
    m, st = token_mix(x, pos, past, lp)
    x = layer_norm(ALPHA * x + m, lp["ln_g"][1], lp["ln_b"][1])
    x = layer_norm(ALPHA * x + 0.5 * swiglu(x, lp["ffn2_up"], lp["ffn2_down"]), lp["ln_g"][2], lp["ln_b"][2])
    return x, st


def setup_inputs(seed: int = 0) -> dict:
    key = jax.random.key(seed)
    ks = jax.random.split(key, 32)

    def nrm(k, shape, scale):
        return jax.random.normal(k, shape, jnp.float32) * scale

    L = DEPTH
    return {
        "x_prompt": nrm(ks[0], (BATCH, SEQ, D_MODEL), 1.0),
        "x_sample": nrm(ks[1], (DEC_BATCH, DEC_SEQ, D_MODEL), 1.0),
        "cache_mla_ckv": nrm(ks[2], (L, DEC_BATCH, PAST_LEN, MLA_KV_LORA), 1.0),
        "cache_mla_krope": nrm(ks[3], (L, DEC_BATCH, PAST_LEN, MLA_ROPE), 1.0),
        "state_ret": nrm(ks[4], (L, DEC_BATCH, RET_HEADS, RET_DK, RET_DV), 1.0),
        "state_gla": nrm(ks[5], (L, DEC_BATCH, GLA_HEADS, GLA_DK, GLA_DV), 1.0),
        "w_in": nrm(ks[6], (L, D_MODEL, D_IN), D_MODEL ** -0.5),
        "ret_gn_g": 1.0 + nrm(ks[7], (L, RET_HEADS * RET_DV), 0.02),
        "mla_q_norm_g": 1.0 + nrm(ks[8], (L, MLA_Q_LORA), 0.02),
        "mla_w_q_up": nrm(ks[9], (L, MLA_Q_LORA, MLA_HEADS * (MLA_NOPE + MLA_ROPE)), MLA_Q_LORA ** -0.5),
        "mla_kv_norm_g": 1.0 + nrm(ks[10], (L, MLA_KV_LORA), 0.02),
        "mla_w_kv_up": nrm(ks[11], (L, MLA_KV_LORA, MLA_HEADS * (MLA_NOPE + MLA_DV)), MLA_KV_LORA ** -0.5),
        "gla_w_gate_up": nrm(ks[12], (L, GLA_GATE_RANK, GLA_HEADS * GLA_DK), GLA_GATE_RANK ** -0.5),
        "gla_b_gate": nrm(ks[13], (L, GLA_HEADS * GLA_DK), 0.02),
        "gla_gn_g": 1.0 + nrm(ks[14], (L, GLA_HEADS * GLA_DV), 0.02),
        "w_ret_o": nrm(ks[15], (L, RET_HEADS * RET_DV, D_MODEL), BETA * (RET_HEADS * RET_DV) ** -0.5),
        "w_mla_o": nrm(ks[16], (L, MLA_HEADS * MLA_DV, D_MODEL), BETA * (MLA_HEADS * MLA_DV) ** -0.5),
        "w_gla_o": nrm(ks[17], (L, GLA_HEADS * GLA_DV, D_MODEL), BETA * (GLA_HEADS * GLA_DV) ** -0.5),
        "w_out": nrm(ks[18], (L, D_MODEL, D_MODEL), BETA * D_MODEL ** -0.5),
        "ffn1_up": nrm(ks[19], (L, D_MODEL, 2 * D_FF), BETA * D_MODEL ** -0.5),
        "ffn1_down": nrm(ks[20], (L, D_FF, D_MODEL), BETA * D_FF ** -0.5),
        "ffn2_up": nrm(ks[21], (L, D_MODEL, 2 * D_FF), BETA * D_MODEL ** -0.5),
        "ffn2_down": nrm(ks[22], (L, D_FF, D_MODEL), BETA * D_FF ** -0.5),
        "ln_g": 1.0 + nrm(ks[23], (L, 3, D_MODEL), 0.02),
        "ln_b": nrm(ks[24], (L, 3, D_MODEL), 0.02),
    }


def reference(x_prompt, x_sample, cache_mla_ckv, cache_mla_krope, state_ret, state_gla,
              w_in, ret_gn_g, mla_q_norm_g, mla_w_q_up, mla_kv_norm_g, mla_w_kv_up,
              gla_w_gate_up, gla_b_gate, gla_gn_g, w_ret_o, w_mla_o, w_gla_o, w_out,
              ffn1_up, ffn1_down, ffn2_up, ffn2_down, ln_g, ln_b):
    pos_p = jnp.arange(x_prompt.shape[1])
    pos_s = cache_mla_ckv.shape[2] + jnp.arange(x_sample.shape[1])
    xp, xs = x_prompt, x_sample
    ckv_p, kr_p, ret_p, gla_p = [], [], [], []
    ckv_s, kr_s, ret_s, gla_s = [], [], [], []
    for l in range(DEPTH):
        lp = {
            "w_in": w_in[l], "ret_gn_g": ret_gn_g[l], "mla_q_norm_g": mla_q_norm_g[l],
            "mla_w_q_up": mla_w_q_up[l], "mla_kv_norm_g": mla_kv_norm_g[l], "mla_w_kv_up": mla_w_kv_up[l],
            "gla_w_gate_up": gla_w_gate_up[l], "gla_b_gate": gla_b_gate[l], "gla_gn_g": gla_gn_g[l],
            "w_ret_o": w_ret_o[l], "w_mla_o": w_mla_o[l], "w_gla_o": w_gla_o[l], "w_out": w_out[l],
            "ffn1_up": ffn1_up[l], "ffn1_down": ffn1_down[l], "ffn2_up": ffn2_up[l], "ffn2_down": ffn2_down[l],
            "ln_g": ln_g[l], "ln_b": ln_b[l],
        }
        xp, (a, b, c, d) = layer(xp, pos_p, None, lp)
        ckv_p.append(a); kr_p.append(b); ret_p.append(c); gla_p.append(d)
        past = (cache_mla_ckv[l], cache_mla_krope[l], state_ret[l], state_gla[l])
        xs, (a, b, c, d) = layer(xs, pos_s, past, lp)
        ckv_s.append(a); kr_s.append(b); ret_s.append(c); gla_s.append(d)
    return (xp, xs,
            jnp.stack(ckv_p), jnp.stack(kr_p), jnp.stack(ret_p), jnp.stack(gla_p),
            jnp.stack(ckv_s), jnp.stack(kr_s), jnp.stack(ret_s), jnp.stack(gla_s))
```

```python
import functools

import numpy as np
import jax
import jax.numpy as jnp
from jax import lax
from jax.experimental import pallas as pl
from jax.experimental.pallas import tpu as pltpu

F32 = jnp.float32
BF16 = jnp.bfloat16

D_MODEL = 1024
DEPTH = 2
CHUNK = 64
CHUNK_SHIFT = 6
ALPHA = (2 * DEPTH) ** 0.25
EPS = 1e-5
ROPE_THETA = 10000.0
RET_HEADS, RET_DK, RET_DV = 4, 64, 128
MLA_HEADS, MLA_Q_LORA, MLA_KV_LORA, MLA_NOPE, MLA_ROPE, MLA_DV = 8, 384, 256, 64, 32, 64
GLA_HEADS, GLA_DK, GLA_DV, GLA_GATE_RANK, GLA_TAU = 4, 128, 256, 16, 16.0
D_FF = 2816
N_BRANCH = 3
IN_SPLITS = (RET_HEADS * RET_DK, RET_HEADS * RET_DK, RET_HEADS * RET_DV, RET_HEADS * RET_DV,
             MLA_Q_LORA, MLA_KV_LORA, MLA_ROPE,
             GLA_HEADS * GLA_DK, GLA_HEADS * GLA_DK, GLA_HEADS * GLA_DV, GLA_GATE_RANK, GLA_HEADS * GLA_DV,
             N_BRANCH * D_MODEL)

LANES = 128
MLA_HEAD_PAD = LANES
VMEM_LIMIT = 56 * 1024 * 1024

_C_RQK, _C_RV, _C_CQ, _C_CKV, _C_KR, _C_GQK, _C_GV, _C_LR, _C_END = (
    0, 512, 1024, 1408, 1664, 1792, 2816, 3840, 3968)
_G_RG, _G_GOG, _G_BR, _G_END = 0, 512, 1536, 4608

_NT = (((1,), (1,)), ((), ()))
_TN = (((0,), (0,)), ((), ()))


def _params(*sem):
    return pltpu.CompilerParams(dimension_semantics=sem, vmem_limit_bytes=VMEM_LIMIT)


def _dot(a, b):
    return jnp.dot(a, b, preferred_element_type=F32)


def _layer_norm(z, g, b):
    mu = jnp.mean(z, axis=-1, keepdims=True)
    zc = z - mu
    var = jnp.mean(zc * zc, axis=-1, keepdims=True)
    return zc * lax.rsqrt(var + EPS) * g + b


def _row_spec(tm, cols):
    return pl.BlockSpec((tm, cols), lambda i: (i, 0))


def _full_spec(shape):
    return pl.BlockSpec(shape, lambda *_: (0,) * len(shape))


FFN_TM = 512
FFN_FC = 256


def _ffn_ln_kernel(x_ref, wg_ref, wu_ref, wd_ref, g_ref, b_ref, o_ref, h_ref):
    x = x_ref[...]
    xb = x.astype(BF16)
    for c in range(D_FF // FFN_FC):
        sl = slice(c * FFN_FC, (c + 1) * FFN_FC)
        gate = _dot(xb, wg_ref[:, sl])
        up = _dot(xb, wu_ref[:, sl])
        h_ref[:, sl] = (gate * jax.nn.sigmoid(gate) * up).astype(BF16)
    y = _dot(h_ref[...], wd_ref[...])
    o_ref[...] = _layer_norm(ALPHA * x + 0.5 * y, g_ref[...], b_ref[...])


def _ffn_ln(x, wg, wu, wd, g, b):
    n = x.shape[0]
    tm = min(FFN_TM, n)
    return pl.pallas_call(
        _ffn_ln_kernel,
        grid=(n // tm,),
        in_specs=[_row_spec(tm, D_MODEL), _full_spec(wg.shape), _full_spec(wu.shape), _full_spec(wd.shape),
                  _full_spec(g.shape), _full_spec(b.shape)],
        out_specs=_row_spec(tm, D_MODEL),
        out_shape=jax.ShapeDtypeStruct((n, D_MODEL), F32),
        scratch_shapes=[pltpu.VMEM((tm, D_FF), BF16)],
        compiler_params=_params("parallel"),
        name="ffn_ln",
    )(x, wg, wu, wd, g, b)


INP_TM = 256


def _swap_halves(x, first_mask, half):
    return jnp.where(first_mask, pltpu.roll(x, LANES - half, 1), pltpu.roll(x, half, 1))


def _in_proj_kernel(x_ref, w_ref, wq_ref, wgate_ref, bgate_ref, qn_ref, kvn_ref,
                    cr_ref, sr_ref, cq_ref, sq_ref, ck_ref, sk_ref,
                    rqk_ref, rv_ref, qm_ref, ckv_ref, kr_ref, gqk_ref, gv_ref, la_ref):
    xb = x_ref[...].astype(BF16)
    tm = xb.shape[0]
    lane = lax.broadcasted_iota(jnp.int32, (tm, LANES), 1)

    ret_first = (lane & (RET_DK - 1)) < RET_DK // 2
    for c in range(2 * RET_HEADS * RET_DK // LANES):
        sl = slice(c * LANES, (c + 1) * LANES)
        h = _dot(xb, w_ref[:, _C_RQK + c * LANES:_C_RQK + (c + 1) * LANES])
        rqk_ref[:, sl] = h * cr_ref[:, sl] + _swap_halves(h, ret_first, RET_DK // 2) * sr_ref[:, sl]
    rv_ref[...] = _dot(xb, w_ref[:, _C_RV:_C_CQ]).astype(BF16)

    hq = _dot(xb, w_ref[:, _C_CQ:_C_CKV])
    cq = hq * lax.rsqrt(jnp.mean(hq * hq, axis=-1, keepdims=True) + EPS) * qn_ref[...]
    q_first = lane < MLA_NOPE + MLA_ROPE // 2
    cqb = cq.astype(BF16)
    for h_i in range(MLA_HEADS):
        sl = slice(h_i * MLA_HEAD_PAD, (h_i + 1) * MLA_HEAD_PAD)
        qh = _dot(cqb, wq_ref[:, sl])
        qm_ref[:, sl] = (qh * cq_ref[...] + _swap_halves(qh, q_first, MLA_ROPE // 2) * sq_ref[...]).astype(BF16)

    hkv = _dot(xb, w_ref[:, _C_CKV:_C_KR])
    ckv_ref[...] = hkv * lax.rsqrt(jnp.mean(hkv * hkv, axis=-1, keepdims=True) + EPS) * kvn_ref[...]
    hkr = _dot(xb, w_ref[:, _C_KR:_C_GQK])
    kr_ref[...] = hkr * ck_ref[...] + _swap_halves(hkr, lane < MLA_ROPE // 2, MLA_ROPE // 2) * sk_ref[...]

    ghd = GLA_HEADS * GLA_DK
    gla_q = _dot(xb, w_ref[:, _C_GQK:_C_GQK + ghd])
    gqk_ref[:, :ghd] = gla_q * (GLA_DK ** -0.5)
    gqk_ref[:, ghd:] = _dot(xb, w_ref[:, _C_GQK + ghd:_C_GV])
    gv_ref[...] = _dot(xb, w_ref[:, _C_GV:_C_LR]).astype(BF16)
    lr = _dot(xb, w_ref[:, _C_LR:_C_END]).astype(BF16)
    logit = _dot(lr, wgate_ref[...]) + bgate_ref[...]
    log_sig = jnp.minimum(logit, 0.0) - jnp.log1p(jnp.exp(-jnp.abs(logit)))
    la_ref[...] = log_sig / GLA_TAU


def _in_proj(x, lw, tabs):
    n = x.shape[0]
    tm = min(INP_TM, n)
    period = tabs["cr"].shape[0] // tm

    def tab_spec(cols):
        return pl.BlockSpec((tm, cols), lambda i: (i % period, 0))

    out_cols = [(2 * RET_HEADS * RET_DK, F32), (RET_HEADS * RET_DV, BF16), (MLA_HEADS * MLA_HEAD_PAD, BF16),
                (MLA_KV_LORA, F32), (LANES, F32), (2 * GLA_HEADS * GLA_DK, F32), (GLA_HEADS * GLA_DV, BF16),
                (GLA_HEADS * GLA_DK, F32)]
    weights = [lw["w1"], lw["wq"], lw["wgate"], lw["bgate"], lw["qn"], lw["kvn"]]
    return pl.pallas_call(
        _in_proj_kernel,
        grid=(n // tm,),
        in_specs=[_row_spec(tm, D_MODEL)] + [_full_spec(w.shape) for w in weights]
        + [tab_spec(512), tab_spec(512), tab_spec(LANES), tab_spec(LANES), tab_spec(LANES), tab_spec(LANES)],
        out_specs=[_row_spec(tm, c) for c, _ in out_cols],
        out_shape=[jax.ShapeDtypeStruct((n, c), d) for c, d in out_cols],
        compiler_params=_params("parallel"),
        name="in_proj",
    )(x, *weights, tabs["cr"], tabs["sr"], tabs["cq"], tabs["sq"], tabs["ck"], tabs["sk"])


KVUP_TM = 512


def _kv_up_kernel(ckv_ref, kr_ref, wk_ref, e_ref, wv_ref, k_ref, v_ref):
    cb = ckv_ref[...].astype(BF16)
    k_ref[...] = (_dot(cb, wk_ref[...]) + _dot(kr_ref[...].astype(BF16), e_ref[...])).astype(BF16)
    v_ref[...] = _dot(cb, wv_ref[...]).astype(BF16)


def _kv_up(ckv, krp, lw):
    n = ckv.shape[0]
    tm = KVUP_TM if n % KVUP_TM == 0 else n
    weights = [lw["wk"], lw["e"], lw["wv"]]
    return pl.pallas_call(
        _kv_up_kernel,
        grid=(n // tm,),
        in_specs=[_row_spec(tm, MLA_KV_LORA), _row_spec(tm, LANES)] + [_full_spec(w.shape) for w in weights],
        out_specs=[_row_spec(tm, MLA_HEADS * MLA_HEAD_PAD), _row_spec(tm, MLA_HEADS * MLA_DV)],
        out_shape=[jax.ShapeDtypeStruct((n, MLA_HEADS * MLA_HEAD_PAD), BF16),
                   jax.ShapeDtypeStruct((n, MLA_HEADS * MLA_DV), BF16)],
        compiler_params=_params("parallel"),
        name="kv_up",
    )(ckv, krp, *weights)


SCAN_CHUNKS_PER_STEP = 8


def _cumsum_rows(a):
    rows = lax.broadcasted_iota(jnp.int32, a.shape, 0)
    s = 1
    while s < a.shape[0]:
        a = a + jnp.where(rows >= s, pltpu.roll(a, s, 0), 0.0)
        s *= 2
    return a


def _scan_kernel(*refs, heads, dk, dv, n_chunks, has_la, has_s0):
    it = iter(refs)
    qk_ref, v_ref = next(it), next(it)
    la_ref = next(it)
    s0_ref = next(it) if has_s0 else None
    o_ref, sT_ref, st_ref = next(it), next(it), next(it)
    hd = heads * dk
    step = pl.program_id(1)

    @pl.when(step == 0)
    def _():
        if has_s0:
            st_ref[...] = s0_ref[0]
        else:
            st_ref[...] = jnp.zeros_like(st_ref)

    row = lax.broadcasted_iota(jnp.int32, (CHUNK, CHUNK), 0)
    col = lax.broadcasted_iota(jnp.int32, (CHUNK, CHUNK), 1)
    causal = row >= col

    def chunk(c, carry):
        rows = pl.ds(pl.multiple_of(c * CHUNK, CHUNK), CHUNK)
        if has_la:
            bc = _cumsum_rows(la_ref[0, rows, :])
        else:
            steps = lax.broadcasted_iota(jnp.int32, (CHUNK, hd), 0) + 1
            bc = steps.astype(F32) * la_ref[...]
        bl = bc[CHUNK - 1:CHUNK, :]
        q = qk_ref[0, rows, :hd]
        k = qk_ref[0, rows, hd:]
        qe = (q * jnp.exp(bc)).astype(BF16)
        ke = (k * jnp.exp(-bc)).astype(BF16)
        kd = (k * jnp.exp(bl - bc)).astype(BF16)
        el = jnp.exp(bl)
        for h in range(heads):
            ks = slice(h * dk, (h + 1) * dk)
            vs = slice(h * dv, (h + 1) * dv)
            att = lax.dot_general(qe[:, ks], ke[:, ks], _NT, preferred_element_type=F32)
            att = jnp.where(causal, att, 0.0).astype(BF16)
            vh = v_ref[0, rows, vs]
            sT = st_ref[h]
            o = _dot(att, vh) + lax.dot_general(qe[:, ks], sT.astype(BF16), _NT, preferred_element_type=F32)
            o_ref[0, rows, vs] = o
            st_ref[h] = sT * el[:, ks] + lax.dot_general(vh, kd[:, ks], _TN, preferred_element_type=F32)
        return carry

    lax.fori_loop(0, n_chunks, chunk, 0)

    @pl.when(step == pl.num_programs(1) - 1)
    def _():
        sT_ref[0] = st_ref[...]


def _scan(qk, v, la, s0T, *, heads, dk, dv):
    b, t, _ = qk.shape
    has_la = la.ndim == 3
    has_s0 = s0T is not None
    ncs = min(SCAN_CHUNKS_PER_STEP, t // CHUNK)
    rows = ncs * CHUNK
    hd, hv = heads * dk, heads * dv

    def seq_spec(cols):
        return pl.BlockSpec((1, rows, cols), lambda bi, si: (bi, si, 0))

    st_spec = pl.BlockSpec((1, heads, dv, dk), lambda bi, si: (bi, 0, 0, 0))
    in_specs = [seq_spec(2 * hd), seq_spec(hv), seq_spec(hd) if has_la else _full_spec(la.shape)]
    args = [qk, v, la]
    if has_s0:
        in_specs.append(st_spec)
        args.append(s0T)
    kern = functools.partial(_scan_kernel, heads=heads, dk=dk, dv=dv, n_chunks=ncs, has_la=has_la, has_s0=has_s0)
    return pl.pallas_call(
        kern,
        grid=(b, t // rows),
        in_specs=in_specs,
        out_specs=[seq_spec(hv), st_spec],
        out_shape=[jax.ShapeDtypeStruct((b, t, hv), F32), jax.ShapeDtypeStruct((b, heads, dv, dk), F32)],
        scratch_shapes=[pltpu.VMEM((heads, dv, dk), F32)],
        compiler_params=_params("parallel", "arbitrary"),
        name="scan_h%d_dk%d" % (heads, dk),
    )(*args)


ATT_TQ = 512
ATT_TK = 512


def _attn_kernel(q_ref, k_ref, v_ref, o_ref, m_ref, l_ref, acc_ref, *, tq, tk, q_chunk0, nk):
    i, j = pl.program_id(1), pl.program_id(2)
    q_lo = q_chunk0 + (i * tq) // CHUNK
    q_hi = q_chunk0 + (i * tq + tq - 1) // CHUNK
    k_lo = (j * tk) // CHUNK
    k_hi = (j * tk + tk - 1) // CHUNK
    j_last = jnp.minimum(nk - 1, ((q_hi + 1) * CHUNK - 1) // tk)

    @pl.when(j == 0)
    def _():
        m_ref[...] = jnp.full_like(m_ref, -jnp.inf)
        l_ref[...] = jnp.zeros_like(l_ref)
        acc_ref[...] = jnp.zeros_like(acc_ref)

    def tile(masked):
        if masked:
            qc = q_chunk0 + ((i * tq + lax.broadcasted_iota(jnp.int32, (tq, tk), 0)) >> CHUNK_SHIFT)
            kc = (j * tk + lax.broadcasted_iota(jnp.int32, (tq, tk), 1)) >> CHUNK_SHIFT
            vis = kc <= qc
        for h in range(MLA_HEADS):
            qs = slice(h * MLA_HEAD_PAD, (h + 1) * MLA_HEAD_PAD)
            vs = slice(h * MLA_DV, (h + 1) * MLA_DV)
            s = lax.dot_general(q_ref[0, :, qs], k_ref[0, :, qs], _NT, preferred_element_type=F32)
            if masked:
                s = jnp.where(vis, s, -jnp.inf)
            m_prev = m_ref[h]
            m_new = jnp.maximum(m_prev, jnp.max(s, axis=-1, keepdims=True))
            alpha = jnp.exp(m_prev - m_new)
            p = jnp.exp(s - m_new)
            l_ref[h] = alpha * l_ref[h] + jnp.sum(p, axis=-1, keepdims=True)
            acc_ref[h] = alpha * acc_ref[h] + _dot(p.astype(BF16), v_ref[0, :, vs])
            m_ref[h] = m_new

    @pl.when(k_hi <= q_lo)
    def _():
        tile(False)

    @pl.when(jnp.logical_and(k_hi > q_lo, k_lo <= q_hi))
    def _():
        tile(True)

    @pl.when(j == j_last)
    def _():
        for h in range(MLA_HEADS):
            o_ref[0, :, h * MLA_DV:(h + 1) * MLA_DV] = (acc_ref[h] / l_ref[h]).astype(o_ref.dtype)


def _attention(q, k, v, *, q_chunk0, tq, tk):
    b, t_q, _ = q.shape
    t_k = k.shape[1]
    nq, nk = t_q // tq, t_k // tk

    def kv_map(bi, i, j):
        q_hi = q_chunk0 + (i * tq + tq - 1) // CHUNK
        return (bi, jnp.minimum(j, jnp.minimum(nk - 1, ((q_hi + 1) * CHUNK - 1) // tk)), 0)

    kern = functools.partial(_attn_kernel, tq=tq, tk=tk, q_chunk0=q_chunk0, nk=nk)
    return pl.pallas_call(
        kern,
        grid=(b, nq, nk),
        in_specs=[pl.BlockSpec((1, tq, MLA_HEADS * MLA_HEAD_PAD), lambda bi, i, j: (bi, i, 0)),
                  pl.BlockSpec((1, tk, MLA_HEADS * MLA_HEAD_PAD), kv_map),
                  pl.BlockSpec((1, tk, MLA_HEADS * MLA_DV), kv_map)],
        out_specs=pl.BlockSpec((1, tq, MLA_HEADS * MLA_DV), lambda bi, i, j: (bi, i, 0)),
        out_shape=jax.ShapeDtypeStruct((b, t_q, MLA_HEADS * MLA_DV), BF16),
        scratch_shapes=[pltpu.VMEM((MLA_HEADS, tq, 1), F32), pltpu.VMEM((MLA_HEADS, tq, 1), F32),
                        pltpu.VMEM((MLA_HEADS, tq, MLA_DV), F32)],
        compiler_params=_params("parallel", "parallel", "arbitrary"),
        name="mla_attention",
    )(q, k, v)


OUT_TM = 256


def _out_proj_kernel(x_ref, or_ref, om_ref, og_ref, wg3_ref, wro_ref, wmo_ref, wgo_ref, wout_ref,
                     rgn_ref, ggn_ref, g_ref, b_ref, o_ref, hr_ref, hg_ref):
    x = x_ref[...]
    xb = x.astype(BF16)

    for h in range(RET_HEADS):
        sl = slice(h * RET_DV, (h + 1) * RET_DV)
        gate = _dot(xb, wg3_ref[:, _G_RG + h * RET_DV:_G_RG + (h + 1) * RET_DV])
        o = or_ref[:, sl]
        oc = o - jnp.mean(o, axis=-1, keepdims=True)
        normed = oc * lax.rsqrt(jnp.mean(oc * oc, axis=-1, keepdims=True) + EPS) * rgn_ref[:, sl]
        hr_ref[:, sl] = (normed * (gate * jax.nn.sigmoid(gate))).astype(BF16)
    y_r = _dot(hr_ref[...], wro_ref[...])

    y_m = _dot(om_ref[...], wmo_ref[...])

    for h in range(GLA_HEADS):
        sl = slice(h * GLA_DV, (h + 1) * GLA_DV)
        gate = _dot(xb, wg3_ref[:, _G_GOG + h * GLA_DV:_G_GOG + (h + 1) * GLA_DV])
        o = og_ref[:, sl]
        normed = o * lax.rsqrt(jnp.mean(o * o, axis=-1, keepdims=True) + EPS) * ggn_ref[:, sl]
        hg_ref[:, sl] = (normed * (gate * jax.nn.sigmoid(gate))).astype(BF16)
    y_g = _dot(hg_ref[...], wgo_ref[...])

    def branch_gate(idx):
        lo = _G_BR + idx * D_MODEL
        return jax.nn.sigmoid(_dot(xb, wg3_ref[:, lo:lo + D_MODEL]))

    mix = branch_gate(0) * y_r + branch_gate(1) * y_m + branch_gate(2) * y_g
    y = _dot(mix.astype(BF16), wout_ref[...])
    o_ref[...] = _layer_norm(ALPHA * x + y, g_ref[...], b_ref[...])


def _out_proj_ln(x, o_r, o_m, o_g, lw, g, b):
    n = x.shape[0]
    tm = min(OUT_TM, n)
    weights = [lw["wg3"], lw["w_ret_o"], lw["w_mla_o"], lw["w_gla_o"], lw["w_out"], lw["ret_gn"], lw["gla_gn"], g, b]
    return pl.pallas_call(
        _out_proj_kernel,
        grid=(n // tm,),
        in_specs=[_row_spec(tm, D_MODEL), _row_spec(tm, RET_HEADS * RET_DV), _row_spec(tm, MLA_HEADS * MLA_DV),
                  _row_spec(tm, GLA_HEADS * GLA_DV)] + [_full_spec(w.shape) for w in weights],
        out_specs=_row_spec(tm, D_MODEL),
        out_shape=jax.ShapeDtypeStruct((n, D_MODEL), F32),
        scratch_shapes=[pltpu.VMEM((tm, RET_HEADS * RET_DV), BF16), pltpu.VMEM((tm, GLA_HEADS * GLA_DV), BF16)],
        compiler_params=_params("parallel"),
        name="out_proj_ln",
    )(x, o_r, o_m, o_g, *weights)


def _prep_layer(w, l):
    offs = np.cumsum((0,) + IN_SPLITS)
    w_in = w["w_in"][l]
    (r_q, r_k, r_v, r_g, m_cq, m_ckv, m_kr, g_q, g_k, g_v, g_lr, g_og, br) = [
        w_in[:, offs[i]:offs[i + 1]] for i in range(len(IN_SPLITS))]

    def pad_cols(a, n):
        return jnp.pad(a, ((0, 0), (0, n - a.shape[1])))

    w1 = jnp.concatenate([r_q, r_k, r_v, m_cq, m_ckv, pad_cols(m_kr, LANES), g_q, g_k, g_v,
                          pad_cols(g_lr, LANES)], axis=1).astype(BF16)
    dq = MLA_NOPE + MLA_ROPE
    wq = jnp.pad(w["mla_w_q_up"][l].reshape(MLA_Q_LORA, MLA_HEADS, dq),
                 ((0, 0), (0, 0), (0, MLA_HEAD_PAD - dq))).reshape(MLA_Q_LORA, MLA_HEADS * MLA_HEAD_PAD).astype(BF16)
    kv = w["mla_w_kv_up"][l].reshape(MLA_KV_LORA, MLA_HEADS, MLA_NOPE + MLA_DV)
    wk = jnp.pad(kv[:, :, :MLA_NOPE], ((0, 0), (0, 0), (0, MLA_HEAD_PAD - MLA_NOPE))).reshape(
        MLA_KV_LORA, MLA_HEADS * MLA_HEAD_PAD).astype(BF16)
    wv = kv[:, :, MLA_NOPE:].reshape(MLA_KV_LORA, MLA_HEADS * MLA_DV).astype(BF16)
    place = np.zeros((LANES, MLA_HEADS * MLA_HEAD_PAD), np.float32)
    for h in range(MLA_HEADS):
        place[np.arange(MLA_ROPE), h * MLA_HEAD_PAD + MLA_NOPE + np.arange(MLA_ROPE)] = 1.0
    up1, up2 = w["ffn1_up"][l], w["ffn2_up"][l]
    return {
        "w1": w1, "wq": wq, "wk": wk, "wv": wv, "e": jnp.asarray(place, BF16),
        "wgate": jnp.pad(w["gla_w_gate_up"][l], ((0, LANES - GLA_GATE_RANK), (0, 0))).astype(BF16),
        "bgate": w["gla_b_gate"][l][None, :],
        "qn": w["mla_q_norm_g"][l][None, :], "kvn": w["mla_kv_norm_g"][l][None, :],
        "wg3": jnp.concatenate([r_g, g_og, br], axis=1).astype(BF16),
        "w_ret_o": w["w_ret_o"][l].astype(BF16), "w_mla_o": w["w_mla_o"][l].astype(BF16),
        "w_gla_o": w["w_gla_o"][l].astype(BF16), "w_out": w["w_out"][l].astype(BF16),
        "ret_gn": w["ret_gn_g"][l][None, :], "gla_gn": w["gla_gn_g"][l][None, :],
        "f1g": up1[:, :D_FF].astype(BF16), "f1u": up1[:, D_FF:].astype(BF16), "f1d": w["ffn1_down"][l].astype(BF16),
        "f2g": up2[:, :D_FF].astype(BF16), "f2u": up2[:, D_FF:].astype(BF16), "f2d": w["ffn2_down"][l].astype(BF16),
        "ln_g": w["ln_g"][l], "ln_b": w["ln_b"][l],
    }


def _rope_tables(pos, tm):
    def cos_sin(half):
        inv = ROPE_THETA ** (-jnp.arange(half, dtype=F32) / half)
        ang = pos.astype(F32)[:, None] * inv[None, :]
        return jnp.cos(ang), jnp.sin(ang)

    t = pos.shape[0]
    c32, s32 = cos_sin(RET_DK // 2)
    c16, s16 = cos_sin(MLA_ROPE // 2)
    cr_h = jnp.tile(jnp.concatenate([c32, c32], axis=1), (1, RET_HEADS))
    sr_h = jnp.tile(jnp.concatenate([-s32, s32], axis=1), (1, RET_HEADS))
    k_scale = RET_DK ** -0.5
    q_scale = (MLA_NOPE + MLA_ROPE) ** -0.5
    zeros = lambda n: jnp.zeros((t, n), F32)
    tabs = {
        "cr": jnp.concatenate([cr_h, cr_h * k_scale], axis=1),
        "sr": jnp.concatenate([sr_h, sr_h * k_scale], axis=1),
        "cq": jnp.concatenate([jnp.ones((t, MLA_NOPE), F32), c16, c16, zeros(LANES - MLA_NOPE - MLA_ROPE)], axis=1) * q_scale,
        "sq": jnp.concatenate([zeros(MLA_NOPE), -s16, s16, zeros(LANES - MLA_NOPE - MLA_ROPE)], axis=1) * q_scale,
        "ck": jnp.concatenate([c16, c16, zeros(LANES - MLA_ROPE)], axis=1),
        "sk": jnp.concatenate([-s16, s16, zeros(LANES - MLA_ROPE)], axis=1),
    }
    if t < tm:
        tabs = {k: jnp.tile(v, (tm // t, 1)) for k, v in tabs.items()}
    return tabs


def _group_layer(x, b, t, lw, tabs, past, q_chunk0):
    n = b * t
    x = _ffn_ln(x, lw["f1g"], lw["f1u"], lw["f1d"], lw["ln_g"][0:1], lw["ln_b"][0:1])
    rqk, rv, qm, ckv, krp, gqk, gv, la = _in_proj(x, lw, tabs)

    log_gamma = jnp.log(1.0 - 2.0 ** (-5.0 - jnp.arange(RET_HEADS, dtype=F32)))
    ret_la = jnp.repeat(log_gamma, RET_DK)[None, :]
    s_ret0 = None if past is None else jnp.swapaxes(past[2], -1, -2)
    s_gla0 = None if past is None else jnp.swapaxes(past[3], -1, -2)
    o_r, s_retT = _scan(rqk.reshape(b, t, -1), rv.reshape(b, t, -1), ret_la, s_ret0,
                        heads=RET_HEADS, dk=RET_DK, dv=RET_DV)
    o_g, s_glaT = _scan(gqk.reshape(b, t, -1), gv.reshape(b, t, -1), la.reshape(b, t, -1), s_gla0,
                        heads=GLA_HEADS, dk=GLA_DK, dv=GLA_DV)

    if past is None:
        ckv_all, kr_all, t_k = ckv, krp, t
        tq, tk = min(ATT_TQ, t), min(ATT_TK, t)
    else:
        t_past = past[0].shape[1]
        t_k = -(-(t_past + t) // LANES) * LANES
        pad = t_k - t_past - t
        ckv_all = jnp.concatenate([past[0], ckv.reshape(b, t, -1), jnp.zeros((b, pad, MLA_KV_LORA), F32)], axis=1)
        kr_past = jnp.pad(past[1], ((0, 0), (0, 0), (0, LANES - MLA_ROPE)))
        kr_all = jnp.concatenate([kr_past, krp.reshape(b, t, -1), jnp.zeros((b, pad, LANES), F32)], axis=1)
        ckv_all, kr_all = ckv_all.reshape(b * t_k, -1), kr_all.reshape(b * t_k, -1)
        tq, tk = t, t_k
    k_m, v_m = _kv_up(ckv_all, kr_all, lw)
    o_m = _attention(qm.reshape(b, t, -1), k_m.reshape(b, t_k, -1), v_m.reshape(b, t_k, -1),
                     q_chunk0=q_chunk0, tq=tq, tk=tk)

    x = _out_proj_ln(x, o_r.reshape(n, -1), o_m.reshape(n, -1), o_g.reshape(n, -1), lw,
                     lw["ln_g"][1:2], lw["ln_b"][1:2])
    x = _ffn_ln(x, lw["f2g"], lw["f2u"], lw["f2d"], lw["ln_g"][2:3], lw["ln_b"][2:3])
    new_state = (ckv.reshape(b, t, -1), krp[:, :MLA_ROPE].reshape(b, t, -1),
                 jnp.swapaxes(s_retT, -1, -2), jnp.swapaxes(s_glaT, -1, -2))
    return x, new_state


def kernel(x_prompt, x_sample, cache_mla_ckv, cache_mla_krope, state_ret, state_gla, w_in, ret_gn_g, mla_q_norm_g, mla_w_q_up, mla_kv_norm_g, mla_w_kv_up, gla_w_gate_up, gla_b_gate, gla_gn_g, w_ret_o, w_mla_o, w_gla_o, w_out, ffn1_up, ffn1_down, ffn2_up, ffn2_down, ln_g, ln_b):
    w = dict(w_in=w_in, ret_gn_g=ret_gn_g, mla_q_norm_g=mla_q_norm_g, mla_w_q_up=mla_w_q_up,
             mla_kv_norm_g=mla_kv_norm_g, mla_w_kv_up=mla_w_kv_up, gla_w_gate_up=gla_w_gate_up,
             gla_b_gate=gla_b_gate, gla_gn_g=gla_gn_g, w_ret_o=w_ret_o, w_mla_o=w_mla_o, w_gla_o=w_gla_o,
             w_out=w_out, ffn1_up=ffn1_up, ffn1_down=ffn1_down, ffn2_up=ffn2_up, ffn2_down=ffn2_down,
             ln_g=ln_g, ln_b=ln_b)
    bp, tp, _ = x_prompt.shape
    bs, ts, _ = x_sample.shape
    t_past = cache_mla_ckv.shape[2]
    tabs_p = _rope_tables(jnp.arange(tp), min(INP_TM, bp * tp))
    tabs_s = _rope_tables(t_past + jnp.arange(ts), min(INP_TM, bs * ts))
    xp = x_prompt.reshape(bp * tp, D_MODEL)
    xs = x_sample.reshape(bs * ts, D_MODEL)
    st_p, st_s = [], []
    for l in range(DEPTH):
        lw = _prep_layer(w, l)
        xp, st = _group_layer(xp, bp, tp, lw, tabs_p, None, 0)
        st_p.append(st)
        past = (cache_mla_ckv[l], cache_mla_krope[l], state_ret[l], state_gla[l])
        xs, st = _group_layer(xs, bs, ts, lw, tabs_s, past, t_past // CHUNK)
        st_s.append(st)
    stack = lambda sts, i: jnp.stack([s[i] for s in sts])
    return (xp.reshape(bp, tp, D_MODEL), xs.reshape(bs, ts, D_MODEL),
            stack(st_p, 0), stack(st_p, 1), stack(st_p, 2), stack(st_p, 3),
            stack(st_s, 0), stack(st_s, 1), stack(st_s, 2), stack(st_s, 3))
```

```python
import functools

import numpy as np
import jax
import jax.numpy as jnp
from jax import lax
from jax.experimental import pallas as pl
from jax.experimental.pallas import tpu as pltpu

F32 = jnp.float32
BF16 = jnp.bfloat16

D_MODEL = 1024
DEPTH = 2
CHUNK = 64
CHUNK_SHIFT = 6
ALPHA = (2 * DEPTH) ** 0.25
EPS = 1e-5
ROPE_THETA = 10000.0
RET_HEADS, RET_DK, RET_DV = 4, 64, 128
MLA_HEADS, MLA_Q_LORA, MLA_KV_LORA, MLA_NOPE, MLA_ROPE, MLA_DV = 8, 384, 256, 64, 32, 64
GLA_HEADS, GLA_DK, GLA_DV, GLA_GATE_RANK, GLA_TAU = 4, 128, 256, 16, 16.0
D_FF = 2816
N_BRANCH = 3
IN_SPLITS = (RET_HEADS * RET_DK, RET_HEADS * RET_DK, RET_HEADS * RET_DV, RET_HEADS * RET_DV,
             MLA_Q_LORA, MLA_KV_LORA, MLA_ROPE,
             GLA_HEADS * GLA_DK, GLA_HEADS * GLA_DK, GLA_HEADS * GLA_DV, GLA_GATE_RANK, GLA_HEADS * GLA_DV,
             N_BRANCH * D_MODEL)

LANES = 128
MLA_HEAD_PAD = LANES
VMEM_LIMIT = 56 * 1024 * 1024

_C_RQK, _C_RV, _C_CQ, _C_CKV, _C_KR, _C_GQK, _C_GV, _C_LR, _C_END = (
    0, 512, 1024, 1408, 1664, 1792, 2816, 3840, 3968)
_G_RG, _G_GOG, _G_BR, _G_END = 0, 512, 1536, 4608

_NT = (((1,), (1,)), ((), ()))
_TN = (((0,), (0,)), ((), ()))


def _params(*sem):
    return pltpu.CompilerParams(dimension_semantics=sem, vmem_limit_bytes=VMEM_LIMIT)


def _dot(a, b):
    return jnp.dot(a, b, preferred_element_type=F32)


def _layer_norm(z, g, b):
    mu = jnp.mean(z, axis=-1, keepdims=True)
    zc = z - mu
    var = jnp.mean(zc * zc, axis=-1, keepdims=True)
    return zc * lax.rsqrt(var + EPS) * g + b


def _row_spec(tm, cols):
    return pl.BlockSpec((tm, cols), lambda i: (i, 0))


def _full_spec(shape):
    return pl.BlockSpec(shape, lambda *_: (0,) * len(shape))


FFN_TM = 512
FFN_FC = 256


def _ffn_ln_kernel(x_ref, wg_ref, wu_ref, wd_ref, g_ref, b_ref, o_ref, h_ref):
    x = x_ref[...]
    xb = x.astype(BF16)
    for c in range(D_FF // FFN_FC):
        sl = slice(c * FFN_FC, (c + 1) * FFN_FC)
        gate = _dot(xb, wg_ref[:, sl])
        up = _dot(xb, wu_ref[:, sl])
        h_ref[:, sl] = (gate * jax.nn.sigmoid(gate) * up).astype(BF16)
    y = _dot(h_ref[...], wd_ref[...])
    o_ref[...] = _layer_norm(ALPHA * x + 0.5 * y, g_ref[...], b_ref[...])


def _ffn_ln(x, wg, wu, wd, g, b):
    n = x.shape[0]
    tm = min(FFN_TM, n)
    return pl.pallas_call(
        _ffn_ln_kernel,
        grid=(n // tm,),
        in_specs=[_row_spec(tm, D_MODEL), _full_spec(wg.shape), _full_spec(wu.shape), _full_spec(wd.shape),
                  _full_spec(g.shape), _full_spec(b.shape)],
        out_specs=_row_spec(tm, D_MODEL),
        out_shape=jax.ShapeDtypeStruct((n, D_MODEL), F32),
        scratch_shapes=[pltpu.VMEM((tm, D_FF), BF16)],
        compiler_params=_params("parallel"),
        name="ffn_ln",
    )(x, wg, wu, wd, g, b)


INP_TM = 256


def _swap_halves(x, first_mask, half):
    return jnp.where(first_mask, pltpu.roll(x, LANES - half, 1), pltpu.roll(x, half, 1))


def _in_proj_kernel(x_ref, w_ref, wq_ref, wgate_ref, bgate_ref, qn_ref, kvn_ref,
                    cr_ref, sr_ref, cq_ref, sq_ref, ck_ref, sk_ref,
                    rqk_ref, rv_ref, qm_ref, ckv_ref, kr_ref, gqk_ref, gv_ref, la_ref):
    xb = x_ref[...].astype(BF16)
    tm = xb.shape[0]
    lane = lax.broadcasted_iota(jnp.int32, (tm, LANES), 1)

    ret_first = (lane & (RET_DK - 1)) < RET_DK // 2
    for c in range(2 * RET_HEADS * RET_DK // LANES):
        sl = slice(c * LANES, (c + 1) * LANES)
        h = _dot(xb, w_ref[:, _C_RQK + c * LANES:_C_RQK + (c + 1) * LANES])
        rqk_ref[:, sl] = h * cr_ref[:, sl] + _swap_halves(h, ret_first, RET_DK // 2) * sr_ref[:, sl]
    rv_ref[...] = _dot(xb, w_ref[:, _C_RV:_C_CQ]).astype(BF16)

    hq = _dot(xb, w_ref[:, _C_CQ:_C_CKV])
    cq = hq * lax.rsqrt(jnp.mean(hq * hq, axis=-1, keepdims=True) + EPS) * qn_ref[...]
    q_first = lane < MLA_NOPE + MLA_ROPE // 2
    cqb = cq.astype(BF16)
    for h_i in range(MLA_HEADS):
        sl = slice(h_i * MLA_HEAD_PAD, (h_i + 1) * MLA_HEAD_PAD)
        qh = _dot(cqb, wq_ref[:, sl])
        qm_ref[:, sl] = (qh * cq_ref[...] + _swap_halves(qh, q_first, MLA_ROPE // 2) * sq_ref[...]).astype(BF16)

    hkv = _dot(xb, w_ref[:, _C_CKV:_C_KR])
    ckv_ref[...] = hkv * lax.rsqrt(jnp.mean(hkv * hkv, axis=-1, keepdims=True) + EPS) * kvn_ref[...]
    hkr = _dot(xb, w_ref[:, _C_KR:_C_GQK])
    kr_ref[...] = hkr * ck_ref[...] + _swap_halves(hkr, lane < MLA_ROPE // 2, MLA_ROPE // 2) * sk_ref[...]

    ghd = GLA_HEADS * GLA_DK
    gla_q = _dot(xb, w_ref[:, _C_GQK:_C_GQK + ghd])
    gqk_ref[:, :ghd] = gla_q * (GLA_DK ** -0.5)
    gqk_ref[:, ghd:] = _dot(xb, w_ref[:, _C_GQK + ghd:_C_GV])
    gv_ref[...] = _dot(xb, w_ref[:, _C_GV:_C_LR]).astype(BF16)
    lr = _dot(xb, w_ref[:, _C_LR:_C_END]).astype(BF16)
    logit = _dot(lr, wgate_ref[...]) + bgate_ref[...]
    log_sig = jnp.minimum(logit, 0.0) - jnp.log1p(jnp.exp(-jnp.abs(logit)))
    la_ref[...] = log_sig / GLA_TAU


def _in_proj(x, lw, tabs):
    n = x.shape[0]
    tm = min(INP_TM, n)
    period = tabs["cr"].shape[0] // tm

    def tab_spec(cols):
        return pl.BlockSpec((tm, cols), lambda i: (i % period, 0))

    out_cols = [(2 * RET_HEADS * RET_DK, F32), (RET_HEADS * RET_DV, BF16), (MLA_HEADS * MLA_HEAD_PAD, BF16),
                (MLA_KV_LORA, F32), (LANES, F32), (2 * GLA_HEADS * GLA_DK, F32), (GLA_HEADS * GLA_DV, BF16),
                (GLA_HEADS * GLA_DK, F32)]
    weights = [lw["w1"], lw["wq"], lw["wgate"], lw["bgate"], lw["qn"], lw["kvn"]]
    return pl.pallas_call(
        _in_proj_kernel,
        grid=(n // tm,),
        in_specs=[_row_spec(tm, D_MODEL)] + [_full_spec(w.shape) for w in weights]
        + [tab_spec(512), tab_spec(512), tab_spec(LANES), tab_spec(LANES), tab_spec(LANES), tab_spec(LANES)],
        out_specs=[_row_spec(tm, c) for c, _ in out_cols],
        out_shape=[jax.ShapeDtypeStruct((n, c), d) for c, d in out_cols],
        compiler_params=_params("parallel"),
        name="in_proj",
    )(x, *weights, tabs["cr"], tabs["sr"], tabs["cq"], tabs["sq"], tabs["ck"], tabs["sk"])


KVUP_TM = 512


def _kv_up_kernel(ckv_ref, kr_ref, wk_ref, e_ref, wv_ref, ones_ref, k_ref, v_ref):
    cb = ckv_ref[...].astype(BF16)
    k_ref[...] = (_dot(cb, wk_ref[...]) + _dot(kr_ref[...].astype(BF16), e_ref[...])).astype(BF16)
    v_ref[...] = (_dot(cb, wv_ref[...]) + ones_ref[...]).astype(BF16)


def _kv_up(ckv, krp, lw):
    n = ckv.shape[0]
    tm = KVUP_TM if n % KVUP_TM == 0 else n
    weights = [lw["wk"], lw["e"], lw["wv"], lw["v_ones"]]
    width = MLA_HEADS * MLA_HEAD_PAD
    return pl.pallas_call(
        _kv_up_kernel,
        grid=(n // tm,),
        in_specs=[_row_spec(tm, MLA_KV_LORA), _row_spec(tm, LANES)] + [_full_spec(w.shape) for w in weights],
        out_specs=[_row_spec(tm, width), _row_spec(tm, width)],
        out_shape=[jax.ShapeDtypeStruct((n, width), BF16), jax.ShapeDtypeStruct((n, width), BF16)],
        compiler_params=_params("parallel"),
        name="kv_up",
    )(ckv, krp, *weights)


SCAN_CHUNKS_PER_STEP = 8


def _cumsum_rows(a):
    rows = lax.broadcasted_iota(jnp.int32, a.shape, 0)
    s = 1
    while s < a.shape[0]:
        a = a + jnp.where(rows >= s, pltpu.roll(a, s, 0), 0.0)
        s *= 2
    return a


def _scan_kernel(*refs, heads, dk, dv, n_chunks, has_la, has_s0):
    it = iter(refs)
    qk_ref, v_ref = next(it), next(it)
    la_ref = next(it)
    s0_ref = next(it) if has_s0 else None
    o_ref, sT_ref, st_ref = next(it), next(it), next(it)
    hd = heads * dk
    step = pl.program_id(1)

    @pl.when(step == 0)
    def _():
        if has_s0:
            st_ref[...] = s0_ref[0]
        else:
            st_ref[...] = jnp.zeros_like(st_ref)

    row = lax.broadcasted_iota(jnp.int32, (CHUNK, CHUNK), 0)
    col = lax.broadcasted_iota(jnp.int32, (CHUNK, CHUNK), 1)
    causal = row >= col

    def chunk(c, carry):
        rows = pl.ds(pl.multiple_of(c * CHUNK, CHUNK), CHUNK)
        if has_la:
            bc = _cumsum_rows(la_ref[0, rows, :])
        else:
            steps = lax.broadcasted_iota(jnp.int32, (CHUNK, hd), 0) + 1
            bc = steps.astype(F32) * la_ref[...]
        bl = bc[CHUNK - 1:CHUNK, :]
        q = qk_ref[0, rows, :hd]
        k = qk_ref[0, rows, hd:]
        qe = (q * jnp.exp(bc)).astype(BF16)
        ke = (k * jnp.exp(-bc)).astype(BF16)
        kd = (k * jnp.exp(bl - bc)).astype(BF16)
        el = jnp.exp(bl)
        for h in range(heads):
            ks = slice(h * dk, (h + 1) * dk)
            vs = slice(h * dv, (h + 1) * dv)
            att = lax.dot_general(qe[:, ks], ke[:, ks], _NT, preferred_element_type=F32)
            att = jnp.where(causal, att, 0.0).astype(BF16)
            vh = v_ref[0, rows, vs]
            sT = st_ref[h]
            o = _dot(att, vh) + lax.dot_general(qe[:, ks], sT.astype(BF16), _NT, preferred_element_type=F32)
            o_ref[0, rows, vs] = o
            st_ref[h] = sT * el[:, ks] + lax.dot_general(vh, kd[:, ks], _TN, preferred_element_type=F32)
        return carry

    lax.fori_loop(0, n_chunks, chunk, 0)

    @pl.when(step == pl.num_programs(1) - 1)
    def _():
        sT_ref[0] = st_ref[...]


def _scan(qk, v, la, s0T, *, heads, dk, dv):
    b, t, _ = qk.shape
    has_la = la.ndim == 3
    has_s0 = s0T is not None
    ncs = min(SCAN_CHUNKS_PER_STEP, t // CHUNK)
    rows = ncs * CHUNK
    hd, hv = heads * dk, heads * dv

    def seq_spec(cols):
        return pl.BlockSpec((1, rows, cols), lambda bi, si: (bi, si, 0))

    st_spec = pl.BlockSpec((1, heads, dv, dk), lambda bi, si: (bi, 0, 0, 0))
    in_specs = [seq_spec(2 * hd), seq_spec(hv), seq_spec(hd) if has_la else _full_spec(la.shape)]
    args = [qk, v, la]
    if has_s0:
        in_specs.append(st_spec)
        args.append(s0T)
    kern = functools.partial(_scan_kernel, heads=heads, dk=dk, dv=dv, n_chunks=ncs, has_la=has_la, has_s0=has_s0)
    return pl.pallas_call(
        kern,
        grid=(b, t // rows),
        in_specs=in_specs,
        out_specs=[seq_spec(hv), st_spec],
        out_shape=[jax.ShapeDtypeStruct((b, t, hv), F32), jax.ShapeDtypeStruct((b, heads, dv, dk), F32)],
        scratch_shapes=[pltpu.VMEM((heads, dv, dk), F32)],
        compiler_params=_params("parallel", "arbitrary"),
        name="scan_h%d_dk%d" % (heads, dk),
    )(*args)


ATT_TQ = 512
ATT_TK = 512


def _attn_kernel(q_ref, k_ref, v_ref, o_ref, m_ref, acc_ref, *, tq, tk, q_chunk0, nk):
    i, j = pl.program_id(1), pl.program_id(2)
    q_lo = q_chunk0 + (i * tq) // CHUNK
    q_hi = q_chunk0 + (i * tq + tq - 1) // CHUNK
    k_lo = (j * tk) // CHUNK
    k_hi = (j * tk + tk - 1) // CHUNK
    j_last = jnp.minimum(nk - 1, ((q_hi + 1) * CHUNK - 1) // tk)

    @pl.when(j == 0)
    def _():
        m_ref[...] = jnp.full_like(m_ref, -jnp.inf)
        acc_ref[...] = jnp.zeros_like(acc_ref)

    def tile(masked):
        if masked:
            qc = q_chunk0 + ((i * tq + lax.broadcasted_iota(jnp.int32, (tq, tk), 0)) >> CHUNK_SHIFT)
            kc = (j * tk + lax.broadcasted_iota(jnp.int32, (tq, tk), 1)) >> CHUNK_SHIFT
            vis = kc <= qc
        for h in range(MLA_HEADS):
            hs = slice(h * MLA_HEAD_PAD, (h + 1) * MLA_HEAD_PAD)
            s = lax.dot_general(q_ref[0, :, hs], k_ref[0, :, hs], _NT, preferred_element_type=F32)
            if masked:
                s = jnp.where(vis, s, -jnp.inf)
            m_prev = m_ref[h]
            m_new = jnp.maximum(m_prev, jnp.max(s, axis=-1, keepdims=True))
            alpha = jnp.exp2(m_prev - m_new)
            p = jnp.concatenate([jnp.exp2(s[:, c * LANES:(c + 1) * LANES] - m_new).astype(BF16)
                                 for c in range(tk // LANES)], axis=1)
            acc_ref[h] = alpha * acc_ref[h] + _dot(p, v_ref[0, :, hs])
            m_ref[h] = m_new

    @pl.when(k_hi <= q_lo)
    def _():
        tile(False)

    @pl.when(jnp.logical_and(k_hi > q_lo, k_lo <= q_hi))
    def _():
        tile(True)

    @pl.when(j == j_last)
    def _():
        for h in range(MLA_HEADS):
            a = acc_ref[h]
            o_ref[0, :, h * MLA_DV:(h + 1) * MLA_DV] = (a[:, :MLA_DV] / a[:, MLA_DV:MLA_DV + 1]).astype(o_ref.dtype)


def _attention(q, k, v, *, q_chunk0, tq, tk):
    b, t_q, _ = q.shape
    t_k = k.shape[1]
    nq, nk = t_q // tq, t_k // tk
    width = MLA_HEADS * MLA_HEAD_PAD

    def kv_map(bi, i, j):
        q_hi = q_chunk0 + (i * tq + tq - 1) // CHUNK
        return (bi, jnp.minimum(j, jnp.minimum(nk - 1, ((q_hi + 1) * CHUNK - 1) // tk)), 0)

    kern = functools.partial(_attn_kernel, tq=tq, tk=tk, q_chunk0=q_chunk0, nk=nk)
    return pl.pallas_call(
        kern,
        grid=(b, nq, nk),
        in_specs=[pl.BlockSpec((1, tq, width), lambda bi, i, j: (bi, i, 0)),
                  pl.BlockSpec((1, tk, width), kv_map),
                  pl.BlockSpec((1, tk, width), kv_map)],
        out_specs=pl.BlockSpec((1, tq, MLA_HEADS * MLA_DV), lambda bi, i, j: (bi, i, 0)),
        out_shape=jax.ShapeDtypeStruct((b, t_q, MLA_HEADS * MLA_DV), BF16),
        scratch_shapes=[pltpu.VMEM((MLA_HEADS, tq, LANES), F32), pltpu.VMEM((MLA_HEADS, tq, MLA_HEAD_PAD), F32)],
        compiler_params=_params("parallel", "parallel", "arbitrary"),
        name="mla_attention",
    )(q, k, v)


OUT_TM = 256


def _out_proj_kernel(x_ref, or_ref, om_ref, og_ref, wg3_ref, wro_ref, wmo_ref, wgo_ref, wout_ref,
                     rgn_ref, ggn_ref, g_ref, b_ref, o_ref, hr_ref, hg_ref):
    x = x_ref[...]
    xb = x.astype(BF16)

    for h in range(RET_HEADS):
        sl = slice(h * RET_DV, (h + 1) * RET_DV)
        gate = _dot(xb, wg3_ref[:, _G_RG + h * RET_DV:_G_RG + (h + 1) * RET_DV])
        o = or_ref[:, sl]
        oc = o - jnp.mean(o, axis=-1, keepdims=True)
        normed = oc * lax.rsqrt(jnp.mean(oc * oc, axis=-1, keepdims=True) + EPS) * rgn_ref[:, sl]
        hr_ref[:, sl] = (normed * (gate * jax.nn.sigmoid(gate))).astype(BF16)
    y_r = _dot(hr_ref[...], wro_ref[...])

    y_m = _dot(om_ref[...], wmo_ref[...])

    for h in range(GLA_HEADS):
        sl = slice(h * GLA_DV, (h + 1) * GLA_DV)
        gate = _dot(xb, wg3_ref[:, _G_GOG + h * GLA_DV:_G_GOG + (h + 1) * GLA_DV])
        o = og_ref[:, sl]
        normed = o * lax.rsqrt(jnp.mean(o * o, axis=-1, keepdims=True) + EPS) * ggn_ref[:, sl]
        hg_ref[:, sl] = (normed * (gate * jax.nn.sigmoid(gate))).astype(BF16)
    y_g = _dot(hg_ref[...], wgo_ref[...])

    def branch_gate(idx):
        lo = _G_BR + idx * D_MODEL
        return jax.nn.sigmoid(_dot(xb, wg3_ref[:, lo:lo + D_MODEL]))

    mix = branch_gate(0) * y_r + branch_gate(1) * y_m + branch_gate(2) * y_g
    y = _dot(mix.astype(BF16), wout_ref[...])
    o_ref[...] = _layer_norm(ALPHA * x + y, g_ref[...], b_ref[...])


def _out_proj_ln(x, o_r, o_m, o_g, lw, g, b):
    n = x.shape[0]
    tm = min(OUT_TM, n)
    weights = [lw["wg3"], lw["w_ret_o"], lw["w_mla_o"], lw["w_gla_o"], lw["w_out"], lw["ret_gn"], lw["gla_gn"], g, b]
    return pl.pallas_call(
        _out_proj_kernel,
        grid=(n // tm,),
        in_specs=[_row_spec(tm, D_MODEL), _row_spec(tm, RET_HEADS * RET_DV), _row_spec(tm, MLA_HEADS * MLA_DV),
                  _row_spec(tm, GLA_HEADS * GLA_DV)] + [_full_spec(w.shape) for w in weights],
        out_specs=_row_spec(tm, D_MODEL),
        out_shape=jax.ShapeDtypeStruct((n, D_MODEL), F32),
        scratch_shapes=[pltpu.VMEM((tm, RET_HEADS * RET_DV), BF16), pltpu.VMEM((tm, GLA_HEADS * GLA_DV), BF16)],
        compiler_params=_params("parallel"),
        name="out_proj_ln",
    )(x, o_r, o_m, o_g, *weights)


def _prep_layer(w, l):
    offs = np.cumsum((0,) + IN_SPLITS)
    w_in = w["w_in"][l]
    (r_q, r_k, r_v, r_g, m_cq, m_ckv, m_kr, g_q, g_k, g_v, g_lr, g_og, br) = [
        w_in[:, offs[i]:offs[i + 1]] for i in range(len(IN_SPLITS))]

    def pad_cols(a, n):
        return jnp.pad(a, ((0, 0), (0, n - a.shape[1])))

    w1 = jnp.concatenate([r_q, r_k, r_v, m_cq, m_ckv, pad_cols(m_kr, LANES), g_q, g_k, g_v,
                          pad_cols(g_lr, LANES)], axis=1).astype(BF16)
    dq = MLA_NOPE + MLA_ROPE
    wq = jnp.pad(w["mla_w_q_up"][l].reshape(MLA_Q_LORA, MLA_HEADS, dq),
                 ((0, 0), (0, 0), (0, MLA_HEAD_PAD - dq))).reshape(MLA_Q_LORA, MLA_HEADS * MLA_HEAD_PAD).astype(BF16)
    kv = w["mla_w_kv_up"][l].reshape(MLA_KV_LORA, MLA_HEADS, MLA_NOPE + MLA_DV)
    wk = jnp.pad(kv[:, :, :MLA_NOPE], ((0, 0), (0, 0), (0, MLA_HEAD_PAD - MLA_NOPE))).reshape(
        MLA_KV_LORA, MLA_HEADS * MLA_HEAD_PAD).astype(BF16)
    wv = jnp.pad(kv[:, :, MLA_NOPE:], ((0, 0), (0, 0), (0, MLA_HEAD_PAD - MLA_DV))).reshape(
        MLA_KV_LORA, MLA_HEADS * MLA_HEAD_PAD).astype(BF16)
    place = np.zeros((LANES, MLA_HEADS * MLA_HEAD_PAD), np.float32)
    v_ones = np.zeros((1, MLA_HEADS * MLA_HEAD_PAD), np.float32)
    for h in range(MLA_HEADS):
        place[np.arange(MLA_ROPE), h * MLA_HEAD_PAD + MLA_NOPE + np.arange(MLA_ROPE)] = 1.0
        v_ones[0, h * MLA_HEAD_PAD + MLA_DV] = 1.0
    up1, up2 = w["ffn1_up"][l], w["ffn2_up"][l]
    return {
        "w1": w1, "wq": wq, "wk": wk, "wv": wv, "e": jnp.asarray(place, BF16), "v_ones": jnp.asarray(v_ones),
        "wgate": jnp.pad(w["gla_w_gate_up"][l], ((0, LANES - GLA_GATE_RANK), (0, 0))).astype(BF16),
        "bgate": w["gla_b_gate"][l][None, :],
        "qn": w["mla_q_norm_g"][l][None, :], "kvn": w["mla_kv_norm_g"][l][None, :],
        "wg3": jnp.concatenate([r_g, g_og, br], axis=1).astype(BF16),
        "w_ret_o": w["w_ret_o"][l].astype(BF16), "w_mla_o": w["w_mla_o"][l].astype(BF16),
        "w_gla_o": w["w_gla_o"][l].astype(BF16), "w_out": w["w_out"][l].astype(BF16),
        "ret_gn": w["ret_gn_g"][l][None, :], "gla_gn": w["gla_gn_g"][l][None, :],
        "f1g": up1[:, :D_FF].astype(BF16), "f1u": up1[:, D_FF:].astype(BF16), "f1d": w["ffn1_down"][l].astype(BF16),
        "f2g": up2[:, :D_FF].astype(BF16), "f2u": up2[:, D_FF:].astype(BF16), "f2d": w["ffn2_down"][l].astype(BF16),
        "ln_g": w["ln_g"][l], "ln_b": w["ln_b"][l],
    }


def _rope_tables(pos, tm):
    def cos_sin(half):
        inv = ROPE_THETA ** (-jnp.arange(half, dtype=F32) / half)
        ang = pos.astype(F32)[:, None] * inv[None, :]
        return jnp.cos(ang), jnp.sin(ang)

    t = pos.shape[0]
    c32, s32 = cos_sin(RET_DK // 2)
    c16, s16 = cos_sin(MLA_ROPE // 2)
    cr_h = jnp.tile(jnp.concatenate([c32, c32], axis=1), (1, RET_HEADS))
    sr_h = jnp.tile(jnp.concatenate([-s32, s32], axis=1), (1, RET_HEADS))
    k_scale = RET_DK ** -0.5
    q_scale = (MLA_NOPE + MLA_ROPE) ** -0.5 * float(np.log2(np.e))
    zeros = lambda n: jnp.zeros((t, n), F32)
    tabs = {
        "cr": jnp.concatenate([cr_h, cr_h * k_scale], axis=1),
        "sr": jnp.concatenate([sr_h, sr_h * k_scale], axis=1),
        "cq": jnp.concatenate([jnp.ones((t, MLA_NOPE), F32), c16, c16, zeros(LANES - MLA_NOPE - MLA_ROPE)], axis=1) * q_scale,
        "sq": jnp.concatenate([zeros(MLA_NOPE), -s16, s16, zeros(LANES - MLA_NOPE - MLA_ROPE)], axis=1) * q_scale,
        "ck": jnp.concatenate([c16, c16, zeros(LANES - MLA_ROPE)], axis=1),
        "sk": jnp.concatenate([-s16, s16, zeros(LANES - MLA_ROPE)], axis=1),
    }
    if t < tm:
        tabs = {k: jnp.tile(v, (tm // t, 1)) for k, v in tabs.items()}
    return tabs


def _group_layer(x, b, t, lw, tabs, past, q_chunk0):
    n = b * t
    x = _ffn_ln(x, lw["f1g"], lw["f1u"], lw["f1d"], lw["ln_g"][0:1], lw["ln_b"][0:1])
    rqk, rv, qm, ckv, krp, gqk, gv, la = _in_proj(x, lw, tabs)

    log_gamma = jnp.log(1.0 - 2.0 ** (-5.0 - jnp.arange(RET_HEADS, dtype=F32)))
    ret_la = jnp.repeat(log_gamma, RET_DK)[None, :]
    s_ret0 = None if past is None else jnp.swapaxes(past[2], -1, -2)
    s_gla0 = None if past is None else jnp.swapaxes(past[3], -1, -2)
    o_r, s_retT = _scan(rqk.reshape(b, t, -1), rv.reshape(b, t, -1), ret_la, s_ret0,
                        heads=RET_HEADS, dk=RET_DK, dv=RET_DV)
    o_g, s_glaT = _scan(gqk.reshape(b, t, -1), gv.reshape(b, t, -1), la.reshape(b, t, -1), s_gla0,
                        heads=GLA_HEADS, dk=GLA_DK, dv=GLA_DV)

    if past is None:
        ckv_all, kr_all, t_k = ckv, krp, t
        tq, tk = min(ATT_TQ, t), min(ATT_TK, t)
    else:
        t_past = past[0].shape[1]
        t_k = -(-(t_past + t) // LANES) * LANES
        pad = t_k - t_past - t
        ckv_all = jnp.concatenate([past[0], ckv.reshape(b, t, -1), jnp.zeros((b, pad, MLA_KV_LORA), F32)], axis=1)
        kr_past = jnp.pad(past[1], ((0, 0), (0, 0), (0, LANES - MLA_ROPE)))
        kr_all = jnp.concatenate([kr_past, krp.reshape(b, t, -1), jnp.zeros((b, pad, LANES), F32)], axis=1)
        ckv_all, kr_all = ckv_all.reshape(b * t_k, -1), kr_all.reshape(b * t_k, -1)
        tq, tk = t, t_k
    k_m, v_m = _kv_up(ckv_all, kr_all, lw)
    o_m = _attention(qm.reshape(b, t, -1), k_m.reshape(b, t_k, -1), v_m.reshape(b, t_k, -1),
                     q_chunk0=q_chunk0, tq=tq, tk=tk)

    x = _out_proj_ln(x, o_r.reshape(n, -1), o_m.reshape(n, -1), o_g.reshape(n, -1), lw,
                     lw["ln_g"][1:2], lw["ln_b"][1:2])
    x = _ffn_ln(x, lw["f2g"], lw["f2u"], lw["f2d"], lw["ln_g"][2:3], lw["ln_b"][2:3])
    new_state = (ckv.reshape(b, t, -1), krp[:, :MLA_ROPE].reshape(b, t, -1),
                 jnp.swapaxes(s_retT, -1, -2), jnp.swapaxes(s_glaT, -1, -2))
    return x, new_state


def kernel(x_prompt, x_sample, cache_mla_ckv, cache_mla_krope, state_ret, state_gla, w_in, ret_gn_g, mla_q_norm_g, mla_w_q_up, mla_kv_norm_g, mla_w_kv_up, gla_w_gate_up, gla_b_gate, gla_gn_g, w_ret_o, w_mla_o, w_gla_o, w_out, ffn1_up, ffn1_down, ffn2_up, ffn2_down, ln_g, ln_b):
    w = dict(w_in=w_in, ret_gn_g=ret_gn_g, mla_q_norm_g=mla_q_norm_g, mla_w_q_up=mla_w_q_up,
             mla_kv_norm_g=mla_kv_norm_g, mla_w_kv_up=mla_w_kv_up, gla_w_gate_up=gla_w_gate_up,
             gla_b_gate=gla_b_gate, gla_gn_g=gla_gn_g, w_ret_o=w_ret_o, w_mla_o=w_mla_o, w_gla_o=w_gla_o,
             w_out=w_out, ffn1_up=ffn1_up, ffn1_down=ffn1_down, ffn2_up=ffn2_up, ffn2_down=ffn2_down,
             ln_g=ln_g, ln_b=ln_b)
    bp, tp, _ = x_prompt.shape
    bs, ts, _ = x_sample.shape
    t_past = cache_mla_ckv.shape[2]
    tabs_p = _rope_tables(jnp.arange(tp), min(INP_TM, bp * tp))
    tabs_s = _rope_tables(t_past + jnp.arange(ts), min(INP_TM, bs * ts))
    xp = x_prompt.reshape(bp * tp, D_MODEL)
    xs = x_sample.reshape(bs * ts, D_MODEL)
    st_p, st_s = [], []
    for l in range(DEPTH):
        lw = _prep_layer(w, l)
        xp, st = _group_layer(xp, bp, tp, lw, tabs_p, None, 0)
        st_p.append(st)
        past = (cache_mla_ckv[l], cache_mla_krope[l], state_ret[l], state_gla[l])
        xs, st = _group_layer(xs, bs, ts, lw, tabs_s, past, t_past // CHUNK)
        st_s.append(st)
    stack = lambda sts, i: jnp.stack([s[i] for s in sts])
    return (xp.reshape(bp, tp, D_MODEL), xs.reshape(bs, ts, D_MODEL),
            stack(st_p, 0), stack(st_p, 1), stack(st_p, 2), stack(st_p, 3),
            stack(st_s, 0), stack(st_s, 1), stack(st_s, 2), stack(st_s, 3))
```

```python
import functools

import numpy as np
import jax
import jax.numpy as jnp
from jax import lax
from jax.experimental import pallas as pl
from jax.experimental.pallas import tpu as pltpu

F32 = jnp.float32
BF16 = jnp.bfloat16

D_MODEL = 1024
DEPTH = 2
CHUNK = 64
CHUNK_SHIFT = 6
ALPHA = (2 * DEPTH) ** 0.25
EPS = 1e-5
ROPE_THETA = 10000.0
RET_HEADS, RET_DK, RET_DV = 4, 64, 128
MLA_HEADS, MLA_Q_LORA, MLA_KV_LORA, MLA_NOPE, MLA_ROPE, MLA_DV = 8, 384, 256, 64, 32, 64
GLA_HEADS, GLA_DK, GLA_DV, GLA_GATE_RANK, GLA_TAU = 4, 128, 256, 16, 16.0
D_FF = 2816
N_BRANCH = 3
IN_SPLITS = (RET_HEADS * RET_DK, RET_HEADS * RET_DK, RET_HEADS * RET_DV, RET_HEADS * RET_DV,
             MLA_Q_LORA, MLA_KV_LORA, MLA_ROPE,
             GLA_HEADS * GLA_DK, GLA_HEADS * GLA_DK, GLA_HEADS * GLA_DV, GLA_GATE_RANK, GLA_HEADS * GLA_DV,
             N_BRANCH * D_MODEL)

LANES = 128
MLA_HEAD_PAD = LANES
VMEM_LIMIT = 56 * 1024 * 1024

_C_RQK, _C_RV, _C_CQ, _C_CKV, _C_KR, _C_GQK, _C_GV, _C_LR, _C_END = (
    0, 512, 1024, 1408, 1664, 1792, 2816, 3840, 3968)
_G_RG, _G_GOG, _G_BR, _G_END = 0, 512, 1536, 4608

_NT = (((1,), (1,)), ((), ()))
_TN = (((0,), (0,)), ((), ()))


def _params(*sem):
    return pltpu.CompilerParams(dimension_semantics=sem, vmem_limit_bytes=VMEM_LIMIT)


def _dot(a, b):
    return jnp.dot(a, b, preferred_element_type=F32)


def _layer_norm(z, g, b):
    mu = jnp.mean(z, axis=-1, keepdims=True)
    zc = z - mu
    var = jnp.mean(zc * zc, axis=-1, keepdims=True)
    return zc * lax.rsqrt(var + EPS) * g + b


def _row_spec(tm, cols):
    return pl.BlockSpec((tm, cols), lambda i: (i, 0))


def _full_spec(shape):
    return pl.BlockSpec(shape, lambda *_: (0,) * len(shape))


FFN_TM = 512
FFN_FC = 256


def _ffn_ln_kernel(x_ref, wg_ref, wu_ref, wd_ref, g_ref, b_ref, o_ref, h_ref):
    x = x_ref[...]
    xb = x.astype(BF16)
    for c in range(D_FF // FFN_FC):
        sl = slice(c * FFN_FC, (c + 1) * FFN_FC)
        gate = _dot(xb, wg_ref[:, sl])
        up = _dot(xb, wu_ref[:, sl])
        h_ref[:, sl] = (gate * jax.nn.sigmoid(gate) * up).astype(BF16)
    y = _dot(h_ref[...], wd_ref[...])
    o_ref[...] = _layer_norm(ALPHA * x + 0.5 * y, g_ref[...], b_ref[...])


def _ffn_ln(x, wg, wu, wd, g, b):
    n = x.shape[0]
    tm = min(FFN_TM, n)
    return pl.pallas_call(
        _ffn_ln_kernel,
        grid=(n // tm,),
        in_specs=[_row_spec(tm, D_MODEL), _full_spec(wg.shape), _full_spec(wu.shape), _full_spec(wd.shape),
                  _full_spec(g.shape), _full_spec(b.shape)],
        out_specs=_row_spec(tm, D_MODEL),
        out_shape=jax.ShapeDtypeStruct((n, D_MODEL), F32),
        scratch_shapes=[pltpu.VMEM((tm, D_FF), BF16)],
        compiler_params=_params("parallel"),
        name="ffn_ln",
    )(x, wg, wu, wd, g, b)


INP_TM = 256


def _swap_halves(x, first_mask, half):
    return jnp.where(first_mask, pltpu.roll(x, LANES - half, 1), pltpu.roll(x, half, 1))


def _in_proj_kernel(x_ref, w_ref, wq_ref, wgate_ref, bgate_ref, qn_ref, kvn_ref,
                    cr_ref, sr_ref, cq_ref, sq_ref, ck_ref, sk_ref,
                    rqk_ref, rv_ref, qm_ref, ckv_ref, kr_ref, gqk_ref, gv_ref, la_ref):
    xb = x_ref[...].astype(BF16)
    tm = xb.shape[0]
    lane = lax.broadcasted_iota(jnp.int32, (tm, LANES), 1)

    ret_first = (lane & (RET_DK - 1)) < RET_DK // 2
    for c in range(2 * RET_HEADS * RET_DK // LANES):
        sl = slice(c * LANES, (c + 1) * LANES)
        h = _dot(xb, w_ref[:, _C_RQK + c * LANES:_C_RQK + (c + 1) * LANES])
        rqk_ref[:, sl] = h * cr_ref[:, sl] + _swap_halves(h, ret_first, RET_DK // 2) * sr_ref[:, sl]
    rv_ref[...] = _dot(xb, w_ref[:, _C_RV:_C_CQ]).astype(BF16)

    hq = _dot(xb, w_ref[:, _C_CQ:_C_CKV])
    cq = hq * lax.rsqrt(jnp.mean(hq * hq, axis=-1, keepdims=True) + EPS) * qn_ref[...]
    q_first = lane < MLA_NOPE + MLA_ROPE // 2
    cqb = cq.astype(BF16)
    for h_i in range(MLA_HEADS):
        sl = slice(h_i * MLA_HEAD_PAD, (h_i + 1) * MLA_HEAD_PAD)
        qh = _dot(cqb, wq_ref[:, sl])
        qm_ref[:, sl] = (qh * cq_ref[...] + _swap_halves(qh, q_first, MLA_ROPE // 2) * sq_ref[...]).astype(BF16)

    hkv = _dot(xb, w_ref[:, _C_CKV:_C_KR])
    ckv_ref[...] = hkv * lax.rsqrt(jnp.mean(hkv * hkv, axis=-1, keepdims=True) + EPS) * kvn_ref[...]
    hkr = _dot(xb, w_ref[:, _C_KR:_C_GQK])
    kr_ref[...] = hkr * ck_ref[...] + _swap_halves(hkr, lane < MLA_ROPE // 2, MLA_ROPE // 2) * sk_ref[...]

    ghd = GLA_HEADS * GLA_DK
    gla_q = _dot(xb, w_ref[:, _C_GQK:_C_GQK + ghd])
    gqk_ref[:, :ghd] = gla_q * (GLA_DK ** -0.5)
    gqk_ref[:, ghd:] = _dot(xb, w_ref[:, _C_GQK + ghd:_C_GV])
    gv_ref[...] = _dot(xb, w_ref[:, _C_GV:_C_LR]).astype(BF16)
    lr = _dot(xb, w_ref[:, _C_LR:_C_END]).astype(BF16)
    logit = _dot(lr, wgate_ref[...]) + bgate_ref[...]
    log_sig = jnp.minimum(logit, 0.0) - jnp.log1p(jnp.exp(-jnp.abs(logit)))
    la_ref[...] = log_sig / GLA_TAU


def _in_proj(x, lw, tabs):
    n = x.shape[0]
    tm = min(INP_TM, n)
    period = tabs["cr"].shape[0] // tm

    def tab_spec(cols):
        return pl.BlockSpec((tm, cols), lambda i: (i % period, 0))

    out_cols = [(2 * RET_HEADS * RET_DK, F32), (RET_HEADS * RET_DV, BF16), (MLA_HEADS * MLA_HEAD_PAD, BF16),
                (MLA_KV_LORA, F32), (LANES, F32), (2 * GLA_HEADS * GLA_DK, F32), (GLA_HEADS * GLA_DV, BF16),
                (GLA_HEADS * GLA_DK, F32)]
    weights = [lw["w1"], lw["wq"], lw["wgate"], lw["bgate"], lw["qn"], lw["kvn"]]
    return pl.pallas_call(
        _in_proj_kernel,
        grid=(n // tm,),
        in_specs=[_row_spec(tm, D_MODEL)] + [_full_spec(w.shape) for w in weights]
        + [tab_spec(512), tab_spec(512), tab_spec(LANES), tab_spec(LANES), tab_spec(LANES), tab_spec(LANES)],
        out_specs=[_row_spec(tm, c) for c, _ in out_cols],
        out_shape=[jax.ShapeDtypeStruct((n, c), d) for c, d in out_cols],
        compiler_params=_params("parallel"),
        name="in_proj",
    )(x, *weights, tabs["cr"], tabs["sr"], tabs["cq"], tabs["sq"], tabs["ck"], tabs["sk"])


KVUP_TM = 512


def _kv_up_kernel(ckv_ref, kr_ref, wk_ref, e_ref, wv_ref, ones_ref, k_ref, v_ref):
    cb = ckv_ref[...].astype(BF16)
    k_ref[...] = (_dot(cb, wk_ref[...]) + _dot(kr_ref[...].astype(BF16), e_ref[...])).astype(BF16)
    v_ref[...] = (_dot(cb, wv_ref[...]) + ones_ref[...]).astype(BF16)


def _kv_up(ckv, krp, lw):
    n = ckv.shape[0]
    tm = KVUP_TM if n % KVUP_TM == 0 else n
    weights = [lw["wk"], lw["e"], lw["wv"], lw["v_ones"]]
    width = MLA_HEADS * MLA_HEAD_PAD
    return pl.pallas_call(
        _kv_up_kernel,
        grid=(n // tm,),
        in_specs=[_row_spec(tm, MLA_KV_LORA), _row_spec(tm, LANES)] + [_full_spec(w.shape) for w in weights],
        out_specs=[_row_spec(tm, width), _row_spec(tm, width)],
        out_shape=[jax.ShapeDtypeStruct((n, width), BF16), jax.ShapeDtypeStruct((n, width), BF16)],
        compiler_params=_params("parallel"),
        name="kv_up",
    )(ckv, krp, *weights)


SCAN_CHUNKS_PER_STEP = 4
SCAN_STREAMS_PER_STEP = 4


def _cumsum_rows(a):
    rows = lax.broadcasted_iota(jnp.int32, a.shape, 0)
    s = 1
    while s < a.shape[0]:
        a = a + jnp.where(rows >= s, pltpu.roll(a, s, 0), 0.0)
        s *= 2
    return a


def _scan_kernel(*refs, heads, dk, dv, n_chunks, nb, has_la, has_s0):
    it = iter(refs)
    qk_ref, v_ref = next(it), next(it)
    la_ref = next(it)
    s0_ref = next(it) if has_s0 else None
    o_ref, sT_ref, st_ref = next(it), next(it), next(it)
    hd = heads * dk
    step = pl.program_id(1)

    @pl.when(step == 0)
    def _():
        if has_s0:
            st_ref[...] = s0_ref[...]
        else:
            st_ref[...] = jnp.zeros_like(st_ref)

    row = lax.broadcasted_iota(jnp.int32, (CHUNK, CHUNK), 0)
    col = lax.broadcasted_iota(jnp.int32, (CHUNK, CHUNK), 1)
    causal = row >= col
    ksl = lambda h: slice(h * dk, (h + 1) * dk)
    vsl = lambda h: slice(h * dv, (h + 1) * dv)
    chains = [(bi, h) for h in range(heads) for bi in range(nb)]

    def chunk(c, carry):
        rows = pl.ds(pl.multiple_of(c * CHUNK, CHUNK), CHUNK)
        qe, ke, kd, el = [], [], [], []
        for bi in range(nb):
            if has_la:
                bc = _cumsum_rows(la_ref[bi, rows, :])
            else:
                steps = lax.broadcasted_iota(jnp.int32, (CHUNK, hd), 0) + 1
                bc = steps.astype(F32) * la_ref[...]
            bl = bc[CHUNK - 1:CHUNK, :]
            q = qk_ref[bi, rows, :hd]
            k = qk_ref[bi, rows, hd:]
            qe.append((q * jnp.exp(bc)).astype(BF16))
            ke.append((k * jnp.exp(-bc)).astype(BF16))
            kd.append((k * jnp.exp(bl - bc)).astype(BF16))
            el.append(jnp.exp(bl))
        att = [lax.dot_general(qe[bi][:, ksl(h)], ke[bi][:, ksl(h)], _NT, preferred_element_type=F32)
               for bi, h in chains]
        cross = [lax.dot_general(qe[bi][:, ksl(h)], st_ref[bi, h].astype(BF16), _NT, preferred_element_type=F32)
                 for bi, h in chains]
        upd = [lax.dot_general(v_ref[bi, rows, vsl(h)], kd[bi][:, ksl(h)], _TN, preferred_element_type=F32)
               for bi, h in chains]
        for n, (bi, h) in enumerate(chains):
            a = jnp.where(causal, att[n], 0.0).astype(BF16)
            o_ref[bi, rows, vsl(h)] = _dot(a, v_ref[bi, rows, vsl(h)]) + cross[n]
        for n, (bi, h) in enumerate(chains):
            st_ref[bi, h] = st_ref[bi, h] * el[bi][:, ksl(h)] + upd[n]
        return carry

    lax.fori_loop(0, n_chunks, chunk, 0)

    @pl.when(step == pl.num_programs(1) - 1)
    def _():
        sT_ref[...] = st_ref[...]


def _scan(qk, v, la, s0T, *, heads, dk, dv):
    b, t, _ = qk.shape
    has_la = la.ndim == 3
    has_s0 = s0T is not None
    nb = min(SCAN_STREAMS_PER_STEP, b)
    ncs = min(SCAN_CHUNKS_PER_STEP, t // CHUNK)
    rows = ncs * CHUNK
    hd, hv = heads * dk, heads * dv

    def seq_spec(cols):
        return pl.BlockSpec((nb, rows, cols), lambda bi, si: (bi, si, 0))

    st_spec = pl.BlockSpec((nb, heads, dv, dk), lambda bi, si: (bi, 0, 0, 0))
    in_specs = [seq_spec(2 * hd), seq_spec(hv), seq_spec(hd) if has_la else _full_spec(la.shape)]
    args = [qk, v, la]
    if has_s0:
        in_specs.append(st_spec)
        args.append(s0T)
    kern = functools.partial(_scan_kernel, heads=heads, dk=dk, dv=dv, n_chunks=ncs, nb=nb,
                             has_la=has_la, has_s0=has_s0)
    return pl.pallas_call(
        kern,
        grid=(b // nb, t // rows),
        in_specs=in_specs,
        out_specs=[seq_spec(hv), st_spec],
        out_shape=[jax.ShapeDtypeStruct((b, t, hv), F32), jax.ShapeDtypeStruct((b, heads, dv, dk), F32)],
        scratch_shapes=[pltpu.VMEM((nb, heads, dv, dk), F32)],
        compiler_params=_params("parallel", "arbitrary"),
        name="scan_h%d_dk%d" % (heads, dk),
    )(*args)


ATT_TQ = 512
ATT_TK = 512
ATT_LOOKAHEAD = 2


def _attn_kernel(q_ref, k_ref, v_ref, o_ref, m_ref, acc_ref, *, tq, tk, q_chunk0, nk):
    i, j = pl.program_id(1), pl.program_id(2)
    q_lo = q_chunk0 + (i * tq) // CHUNK
    q_hi = q_chunk0 + (i * tq + tq - 1) // CHUNK
    k_lo = (j * tk) // CHUNK
    k_hi = (j * tk + tk - 1) // CHUNK
    j_last = jnp.minimum(nk - 1, ((q_hi + 1) * CHUNK - 1) // tk)

    @pl.when(j == 0)
    def _():
        m_ref[...] = jnp.full_like(m_ref, -jnp.inf)
        acc_ref[...] = jnp.zeros_like(acc_ref)

    def tile(masked):
        if masked:
            qc = q_chunk0 + ((i * tq + lax.broadcasted_iota(jnp.int32, (tq, tk), 0)) >> CHUNK_SHIFT)
            kc = (j * tk + lax.broadcasted_iota(jnp.int32, (tq, tk), 1)) >> CHUNK_SHIFT
            vis = kc <= qc
        def scores(h):
            hs = slice(h * MLA_HEAD_PAD, (h + 1) * MLA_HEAD_PAD)
            return lax.dot_general(q_ref[0, :, hs], k_ref[0, :, hs], _NT, preferred_element_type=F32)

        pending = [scores(h) for h in range(ATT_LOOKAHEAD)]
        for h in range(MLA_HEADS):
            hs = slice(h * MLA_HEAD_PAD, (h + 1) * MLA_HEAD_PAD)
            s = pending.pop(0)
            if h + ATT_LOOKAHEAD < MLA_HEADS:
                pending.append(scores(h + ATT_LOOKAHEAD))
            if masked:
                s = jnp.where(vis, s, -jnp.inf)
            m_prev = m_ref[h]
            m_new = jnp.maximum(m_prev, jnp.max(s, axis=-1, keepdims=True))
            alpha = jnp.exp2(m_prev - m_new)
            p = jnp.concatenate([jnp.exp2(s[:, c * LANES:(c + 1) * LANES] - m_new).astype(BF16)
                                 for c in range(tk // LANES)], axis=1)
            acc_ref[h] = alpha * acc_ref[h] + _dot(p, v_ref[0, :, hs])
            m_ref[h] = m_new

    @pl.when(k_hi <= q_lo)
    def _():
        tile(False)

    @pl.when(jnp.logical_and(k_hi > q_lo, k_lo <= q_hi))
    def _():
        tile(True)

    @pl.when(j == j_last)
    def _():
        for h in range(MLA_HEADS):
            a = acc_ref[h]
            o_ref[0, :, h * MLA_DV:(h + 1) * MLA_DV] = (a[:, :MLA_DV] / a[:, MLA_DV:MLA_DV + 1]).astype(o_ref.dtype)


def _attention(q, k, v, *, q_chunk0, tq, tk):
    b, t_q, _ = q.shape
    t_k = k.shape[1]
    nq, nk = t_q // tq, t_k // tk
    width = MLA_HEADS * MLA_HEAD_PAD

    def kv_map(bi, i, j):
        q_hi = q_chunk0 + (i * tq + tq - 1) // CHUNK
        return (bi, jnp.minimum(j, jnp.minimum(nk - 1, ((q_hi + 1) * CHUNK - 1) // tk)), 0)

    kern = functools.partial(_attn_kernel, tq=tq, tk=tk, q_chunk0=q_chunk0, nk=nk)
    return pl.pallas_call(
        kern,
        grid=(b, nq, nk),
        in_specs=[pl.BlockSpec((1, tq, width), lambda bi, i, j: (bi, i, 0)),
                  pl.BlockSpec((1, tk, width), kv_map),
                  pl.BlockSpec((1, tk, width), kv_map)],
        out_specs=pl.BlockSpec((1, tq, MLA_HEADS * MLA_DV), lambda bi, i, j: (bi, i, 0)),
        out_shape=jax.ShapeDtypeStruct((b, t_q, MLA_HEADS * MLA_DV), BF16),
        scratch_shapes=[pltpu.VMEM((MLA_HEADS, tq, LANES), F32), pltpu.VMEM((MLA_HEADS, tq, MLA_HEAD_PAD), F32)],
        compiler_params=_params("parallel", "parallel", "arbitrary"),
        name="mla_attention",
    )(q, k, v)


OUT_TM = 256


def _out_proj_kernel(x_ref, or_ref, om_ref, og_ref, wg3_ref, wro_ref, wmo_ref, wgo_ref, wout_ref,
                     rgn_ref, ggn_ref, g_ref, b_ref, o_ref, hr_ref, hg_ref):
    x = x_ref[...]
    xb = x.astype(BF16)

    for h in range(RET_HEADS):
        sl = slice(h * RET_DV, (h + 1) * RET_DV)
        gate = _dot(xb, wg3_ref[:, _G_RG + h * RET_DV:_G_RG + (h + 1) * RET_DV])
        o = or_ref[:, sl]
        oc = o - jnp.mean(o, axis=-1, keepdims=True)
        normed = oc * lax.rsqrt(jnp.mean(oc * oc, axis=-1, keepdims=True) + EPS) * rgn_ref[:, sl]
        hr_ref[:, sl] = (normed * (gate * jax.nn.sigmoid(gate))).astype(BF16)
    y_r = _dot(hr_ref[...], wro_ref[...])

    y_m = _dot(om_ref[...], wmo_ref[...])

    for h in range(GLA_HEADS):
        sl = slice(h * GLA_DV, (h + 1) * GLA_DV)
        gate = _dot(xb, wg3_ref[:, _G_GOG + h * GLA_DV:_G_GOG + (h + 1) * GLA_DV])
        o = og_ref[:, sl]
        normed = o * lax.rsqrt(jnp.mean(o * o, axis=-1, keepdims=True) + EPS) * ggn_ref[:, sl]
        hg_ref[:, sl] = (normed * (gate * jax.nn.sigmoid(gate))).astype(BF16)
    y_g = _dot(hg_ref[...], wgo_ref[...])

    def branch_gate(idx):
        lo = _G_BR + idx * D_MODEL
        return jax.nn.sigmoid(_dot(xb, wg3_ref[:, lo:lo + D_MODEL]))

    mix = branch_gate(0) * y_r + branch_gate(1) * y_m + branch_gate(2) * y_g
    y = _dot(mix.astype(BF16), wout_ref[...])
    o_ref[...] = _layer_norm(ALPHA * x + y, g_ref[...], b_ref[...])


def _out_proj_ln(x, o_r, o_m, o_g, lw, g, b):
    n = x.shape[0]
    tm = min(OUT_TM, n)
    weights = [lw["wg3"], lw["w_ret_o"], lw["w_mla_o"], lw["w_gla_o"], lw["w_out"], lw["ret_gn"], lw["gla_gn"], g, b]
    return pl.pallas_call(
        _out_proj_kernel,
        grid=(n // tm,),
        in_specs=[_row_spec(tm, D_MODEL), _row_spec(tm, RET_HEADS * RET_DV), _row_spec(tm, MLA_HEADS * MLA_DV),
                  _row_spec(tm, GLA_HEADS * GLA_DV)] + [_full_spec(w.shape) for w in weights],
        out_specs=_row_spec(tm, D_MODEL),
        out_shape=jax.ShapeDtypeStruct((n, D_MODEL), F32),
        scratch_shapes=[pltpu.VMEM((tm, RET_HEADS * RET_DV), BF16), pltpu.VMEM((tm, GLA_HEADS * GLA_DV), BF16)],
        compiler_params=_params("parallel"),
        name="out_proj_ln",
    )(x, o_r, o_m, o_g, *weights)


def _prep_layer(w, l):
    offs = np.cumsum((0,) + IN_SPLITS)
    w_in = w["w_in"][l]
    (r_q, r_k, r_v, r_g, m_cq, m_ckv, m_kr, g_q, g_k, g_v, g_lr, g_og, br) = [
        w_in[:, offs[i]:offs[i + 1]] for i in range(len(IN_SPLITS))]

    def pad_cols(a, n):
        return jnp.pad(a, ((0, 0), (0, n - a.shape[1])))

    w1 = jnp.concatenate([r_q, r_k, r_v, m_cq, m_ckv, pad_cols(m_kr, LANES), g_q, g_k, g_v,
                          pad_cols(g_lr, LANES)], axis=1).astype(BF16)
    dq = MLA_NOPE + MLA_ROPE
    wq = jnp.pad(w["mla_w_q_up"][l].reshape(MLA_Q_LORA, MLA_HEADS, dq),
                 ((0, 0), (0, 0), (0, MLA_HEAD_PAD - dq))).reshape(MLA_Q_LORA, MLA_HEADS * MLA_HEAD_PAD).astype(BF16)
    kv = w["mla_w_kv_up"][l].reshape(MLA_KV_LORA, MLA_HEADS, MLA_NOPE + MLA_DV)
    wk = jnp.pad(kv[:, :, :MLA_NOPE], ((0, 0), (0, 0), (0, MLA_HEAD_PAD - MLA_NOPE))).reshape(
        MLA_KV_LORA, MLA_HEADS * MLA_HEAD_PAD).astype(BF16)
    wv = jnp.pad(kv[:, :, MLA_NOPE:], ((0, 0), (0, 0), (0, MLA_HEAD_PAD - MLA_DV))).reshape(
        MLA_KV_LORA, MLA_HEADS * MLA_HEAD_PAD).astype(BF16)
    place = np.zeros((LANES, MLA_HEADS * MLA_HEAD_PAD), np.float32)
    v_ones = np.zeros((1, MLA_HEADS * MLA_HEAD_PAD), np.float32)
    for h in range(MLA_HEADS):
        place[np.arange(MLA_ROPE), h * MLA_HEAD_PAD + MLA_NOPE + np.arange(MLA_ROPE)] = 1.0
        v_ones[0, h * MLA_HEAD_PAD + MLA_DV] = 1.0
    up1, up2 = w["ffn1_up"][l], w["ffn2_up"][l]
    return {
        "w1": w1, "wq": wq, "wk": wk, "wv": wv, "e": jnp.asarray(place, BF16), "v_ones": jnp.asarray(v_ones),
        "wgate": jnp.pad(w["gla_w_gate_up"][l], ((0, LANES - GLA_GATE_RANK), (0, 0))).astype(BF16),
        "bgate": w["gla_b_gate"][l][None, :],
        "qn": w["mla_q_norm_g"][l][None, :], "kvn": w["mla_kv_norm_g"][l][None, :],
        "wg3": jnp.concatenate([r_g, g_og, br], axis=1).astype(BF16),
        "w_ret_o": w["w_ret_o"][l].astype(BF16), "w_mla_o": w["w_mla_o"][l].astype(BF16),
        "w_gla_o": w["w_gla_o"][l].astype(BF16), "w_out": w["w_out"][l].astype(BF16),
        "ret_gn": w["ret_gn_g"][l][None, :], "gla_gn": w["gla_gn_g"][l][None, :],
        "f1g": up1[:, :D_FF].astype(BF16), "f1u": up1[:, D_FF:].astype(BF16), "f1d": w["ffn1_down"][l].astype(BF16),
        "f2g": up2[:, :D_FF].astype(BF16), "f2u": up2[:, D_FF:].astype(BF16), "f2d": w["ffn2_down"][l].astype(BF16),
        "ln_g": w["ln_g"][l], "ln_b": w["ln_b"][l],
    }


def _rope_tables(pos, tm):
    def cos_sin(half):
        inv = ROPE_THETA ** (-jnp.arange(half, dtype=F32) / half)
        ang = pos.astype(F32)[:, None] * inv[None, :]
        return jnp.cos(ang), jnp.sin(ang)

    t = pos.shape[0]
    c32, s32 = cos_sin(RET_DK // 2)
    c16, s16 = cos_sin(MLA_ROPE // 2)
    cr_h = jnp.tile(jnp.concatenate([c32, c32], axis=1), (1, RET_HEADS))
    sr_h = jnp.tile(jnp.concatenate([-s32, s32], axis=1), (1, RET_HEADS))
    k_scale = RET_DK ** -0.5
    q_scale = (MLA_NOPE + MLA_ROPE) ** -0.5 * float(np.log2(np.e))
    zeros = lambda n: jnp.zeros((t, n), F32)
    tabs = {
        "cr": jnp.concatenate([cr_h, cr_h * k_scale], axis=1),
        "sr": jnp.concatenate([sr_h, sr_h * k_scale], axis=1),
        "cq": jnp.concatenate([jnp.ones((t, MLA_NOPE), F32), c16, c16, zeros(LANES - MLA_NOPE - MLA_ROPE)], axis=1) * q_scale,
        "sq": jnp.concatenate([zeros(MLA_NOPE), -s16, s16, zeros(LANES - MLA_NOPE - MLA_ROPE)], axis=1) * q_scale,
        "ck": jnp.concatenate([c16, c16, zeros(LANES - MLA_ROPE)], axis=1),
        "sk": jnp.concatenate([-s16, s16, zeros(LANES - MLA_ROPE)], axis=1),
    }
    if t < tm:
        tabs = {k: jnp.tile(v, (tm // t, 1)) for k, v in tabs.items()}
    return tabs


def _group_layer(x, b, t, lw, tabs, past, q_chunk0):
    n = b * t
    x = _ffn_ln(x, lw["f1g"], lw["f1u"], lw["f1d"], lw["ln_g"][0:1], lw["ln_b"][0:1])
    rqk, rv, qm, ckv, krp, gqk, gv, la = _in_proj(x, lw, tabs)

    log_gamma = jnp.log(1.0 - 2.0 ** (-5.0 - jnp.arange(RET_HEADS, dtype=F32)))
    ret_la = jnp.repeat(log_gamma, RET_DK)[None, :]
    s_ret0 = None if past is None else jnp.swapaxes(past[2], -1, -2)
    s_gla0 = None if past is None else jnp.swapaxes(past[3], -1, -2)
    o_r, s_retT = _scan(rqk.reshape(b, t, -1), rv.reshape(b, t, -1), ret_la, s_ret0,
                        heads=RET_HEADS, dk=RET_DK, dv=RET_DV)
    o_g, s_glaT = _scan(gqk.reshape(b, t, -1), gv.reshape(b, t, -1), la.reshape(b, t, -1), s_gla0,
                        heads=GLA_HEADS, dk=GLA_DK, dv=GLA_DV)

    if past is None:
        ckv_all, kr_all, t_k = ckv, krp, t
        tq, tk = min(ATT_TQ, t), min(ATT_TK, t)
    else:
        t_past = past[0].shape[1]
        t_k = -(-(t_past + t) // LANES) * LANES
        pad = t_k - t_past - t
        ckv_all = jnp.concatenate([past[0], ckv.reshape(b, t, -1), jnp.zeros((b, pad, MLA_KV_LORA), F32)], axis=1)
        kr_past = jnp.pad(past[1], ((0, 0), (0, 0), (0, LANES - MLA_ROPE)))
        kr_all = jnp.concatenate([kr_past, krp.reshape(b, t, -1), jnp.zeros((b, pad, LANES), F32)], axis=1)
        ckv_all, kr_all = ckv_all.reshape(b * t_k, -1), kr_all.reshape(b * t_k, -1)
        tq, tk = t, t_k
    k_m, v_m = _kv_up(ckv_all, kr_all, lw)
    o_m = _attention(qm.reshape(b, t, -1), k_m.reshape(b, t_k, -1), v_m.reshape(b, t_k, -1),
                     q_chunk0=q_chunk0, tq=tq, tk=tk)

    x = _out_proj_ln(x, o_r.reshape(n, -1), o_m.reshape(n, -1), o_g.reshape(n, -1), lw,
                     lw["ln_g"][1:2], lw["ln_b"][1:2])
    x = _ffn_ln(x, lw["f2g"], lw["f2u"], lw["f2d"], lw["ln_g"][2:3], lw["ln_b"][2:3])
    new_state = (ckv.reshape(b, t, -1), krp[:, :MLA_ROPE].reshape(b, t, -1),
                 jnp.swapaxes(s_retT, -1, -2), jnp.swapaxes(s_glaT, -1, -2))
    return x, new_state


def kernel(x_prompt, x_sample, cache_mla_ckv, cache_mla_krope, state_ret, state_gla, w_in, ret_gn_g, mla_q_norm_g, mla_w_q_up, mla_kv_norm_g, mla_w_kv_up, gla_w_gate_up, gla_b_gate, gla_gn_g, w_ret_o, w_mla_o, w_gla_o, w_out, ffn1_up, ffn1_down, ffn2_up, ffn2_down, ln_g, ln_b):
    w = dict(w_in=w_in, ret_gn_g=ret_gn_g, mla_q_norm_g=mla_q_norm_g, mla_w_q_up=mla_w_q_up,
             mla_kv_norm_g=mla_kv_norm_g, mla_w_kv_up=mla_w_kv_up, gla_w_gate_up=gla_w_gate_up,
             gla_b_gate=gla_b_gate, gla_gn_g=gla_gn_g, w_ret_o=w_ret_o, w_mla_o=w_mla_o, w_gla_o=w_gla_o,
             w_out=w_out, ffn1_up=ffn1_up, ffn1_down=ffn1_down, ffn2_up=ffn2_up, ffn2_down=ffn2_down,
             ln_g=ln_g, ln_b=ln_b)
    bp, tp, _ = x_prompt.shape
    bs, ts, _ = x_sample.shape
    t_past = cache_mla_ckv.shape[2]
    tabs_p = _rope_tables(jnp.arange(tp), min(INP_TM, bp * tp))
    tabs_s = _rope_tables(t_past + jnp.arange(ts), min(INP_TM, bs * ts))
    xp = x_prompt.reshape(bp * tp, D_MODEL)
    xs = x_sample.reshape(bs * ts, D_MODEL)
    st_p, st_s = [], []
    for l in range(DEPTH):
        lw = _prep_layer(w, l)
        xp, st = _group_layer(xp, bp, tp, lw, tabs_p, None, 0)
        st_p.append(st)
        past = (cache_mla_ckv[l], cache_mla_krope[l], state_ret[l], state_gla[l])
        xs, st = _group_layer(xs, bs, ts, lw, tabs_s, past, t_past // CHUNK)
        st_s.append(st)
    stack = lambda sts, i: jnp.stack([s[i] for s in sts])
    return (xp.reshape(bp, tp, D_MODEL), xs.reshape(bs, ts, D_MODEL),
            stack(st_p, 0), stack(st_p, 1), stack(st_p, 2), stack(st_p, 3),
            stack(st_s, 0), stack(st_s, 1), stack(st_s, 2), stack(st_s, 3))
```

```python
import functools

import numpy as np
import jax
import jax.numpy as jnp
from jax import lax
from jax.experimental import pallas as pl
from jax.experimental.pallas import tpu as pltpu

F32 = jnp.float32
BF16 = jnp.bfloat16

D_MODEL = 1024
DEPTH = 2
CHUNK = 64
CHUNK_SHIFT = 6
ALPHA = (2 * DEPTH) ** 0.25
EPS = 1e-5
ROPE_THETA = 10000.0
RET_HEADS, RET_DK, RET_DV = 4, 64, 128
MLA_HEADS, MLA_Q_LORA, MLA_KV_LORA, MLA_NOPE, MLA_ROPE, MLA_DV = 8, 384, 256, 64, 32, 64
GLA_HEADS, GLA_DK, GLA_DV, GLA_GATE_RANK, GLA_TAU = 4, 128, 256, 16, 16.0
D_FF = 2816
N_BRANCH = 3
IN_SPLITS = (RET_HEADS * RET_DK, RET_HEADS * RET_DK, RET_HEADS * RET_DV, RET_HEADS * RET_DV,
             MLA_Q_LORA, MLA_KV_LORA, MLA_ROPE,
             GLA_HEADS * GLA_DK, GLA_HEADS * GLA_DK, GLA_HEADS * GLA_DV, GLA_GATE_RANK, GLA_HEADS * GLA_DV,
             N_BRANCH * D_MODEL)

LANES = 128
MLA_HEAD_PAD = LANES
VMEM_LIMIT = 56 * 1024 * 1024

_C_RQK, _C_RV, _C_CQ, _C_KR, _C_CKV, _C_GQK, _C_GV, _C_LR, _C_END = (
    0, 512, 1024, 1408, 1536, 1792, 2816, 3840, 3968)
_G_RG, _G_GOG, _G_BR, _G_END = 0, 512, 1536, 4608

_NT = (((1,), (1,)), ((), ()))
_TN = (((0,), (0,)), ((), ()))


def _params(*sem):
    return pltpu.CompilerParams(dimension_semantics=sem, vmem_limit_bytes=VMEM_LIMIT)


def _dot(a, b):
    return jnp.dot(a, b, preferred_element_type=F32)


def _layer_norm(z, g, b):
    mu = jnp.mean(z, axis=-1, keepdims=True)
    zc = z - mu
    var = jnp.mean(zc * zc, axis=-1, keepdims=True)
    return zc * lax.rsqrt(var + EPS) * g + b


def _row_spec(tm, cols):
    return pl.BlockSpec((tm, cols), lambda i: (i, 0))


def _full_spec(shape):
    return pl.BlockSpec(shape, lambda *_: (0,) * len(shape), pipeline_mode=pl.Buffered(1))


FFN_TM = 512
FFN_FC = 256


def _ffn_ln_kernel(x_ref, wg_ref, wu_ref, wd_ref, g_ref, b_ref, o_ref, h_ref):
    x = x_ref[...]
    xb = x.astype(BF16)
    for c in range(D_FF // FFN_FC):
        sl = slice(c * FFN_FC, (c + 1) * FFN_FC)
        gate = _dot(xb, wg_ref[:, sl])
        up = _dot(xb, wu_ref[:, sl])
        h_ref[:, sl] = (gate * jax.nn.sigmoid(gate) * up).astype(BF16)
    y = _dot(h_ref[...], wd_ref[...])
    o_ref[...] = _layer_norm(ALPHA * x + 0.5 * y, g_ref[...], b_ref[...])


def _ffn_ln(x, wg, wu, wd, g, b):
    n = x.shape[0]
    tm = min(FFN_TM, n)
    return pl.pallas_call(
        _ffn_ln_kernel,
        grid=(n // tm,),
        in_specs=[_row_spec(tm, D_MODEL), _full_spec(wg.shape), _full_spec(wu.shape), _full_spec(wd.shape),
                  _full_spec(g.shape), _full_spec(b.shape)],
        out_specs=_row_spec(tm, D_MODEL),
        out_shape=jax.ShapeDtypeStruct((n, D_MODEL), F32),
        scratch_shapes=[pltpu.VMEM((tm, D_FF), BF16)],
        compiler_params=_params("parallel"),
        name="ffn_ln",
    )(x, wg, wu, wd, g, b)


INP_TM = 512


def _swap_halves(x, first_mask, half):
    return jnp.where(first_mask, pltpu.roll(x, LANES - half, 1), pltpu.roll(x, half, 1))


def _in_proj_kernel(x_ref, w_ref, wq_ref, wgate_ref, bgate_ref, qn_ref, kvn_ref,
                    cr_ref, sr_ref, cq_ref, sq_ref, ck_ref, sk_ref,
                    rqk_ref, rv_ref, qm_ref, ckv_ref, kr_ref, gqk_ref, gv_ref, la_ref):
    xb = x_ref[...].astype(BF16)
    tm = xb.shape[0]
    lane = lax.broadcasted_iota(jnp.int32, (tm, LANES), 1)

    ret_first = (lane & (RET_DK - 1)) < RET_DK // 2
    h_rqk = _dot(xb, w_ref[:, _C_RQK:_C_RV])
    for c in range(2 * RET_HEADS * RET_DK // LANES):
        sl = slice(c * LANES, (c + 1) * LANES)
        h = h_rqk[:, sl]
        rqk_ref[:, sl] = h * cr_ref[:, sl] + _swap_halves(h, ret_first, RET_DK // 2) * sr_ref[:, sl]
    rv_ref[...] = _dot(xb, w_ref[:, _C_RV:_C_CQ]).astype(BF16)

    h_cq_kr = _dot(xb, w_ref[:, _C_CQ:_C_CKV])
    hq = h_cq_kr[:, :MLA_Q_LORA]
    cq = hq * lax.rsqrt(jnp.mean(hq * hq, axis=-1, keepdims=True) + EPS) * qn_ref[...]
    q_first = lane < MLA_NOPE + MLA_ROPE // 2
    q_up = _dot(cq.astype(BF16), wq_ref[...])
    for h_i in range(MLA_HEADS):
        sl = slice(h_i * MLA_HEAD_PAD, (h_i + 1) * MLA_HEAD_PAD)
        qh = q_up[:, sl]
        qm_ref[:, sl] = (qh * cq_ref[...] + _swap_halves(qh, q_first, MLA_ROPE // 2) * sq_ref[...]).astype(BF16)

    hkr = h_cq_kr[:, MLA_Q_LORA:]
    kr_ref[...] = hkr * ck_ref[...] + _swap_halves(hkr, lane < MLA_ROPE // 2, MLA_ROPE // 2) * sk_ref[...]
    hkv = _dot(xb, w_ref[:, _C_CKV:_C_GQK])
    ckv_ref[...] = hkv * lax.rsqrt(jnp.mean(hkv * hkv, axis=-1, keepdims=True) + EPS) * kvn_ref[...]

    ghd = GLA_HEADS * GLA_DK
    gla_q = _dot(xb, w_ref[:, _C_GQK:_C_GQK + ghd])
    gqk_ref[:, :ghd] = gla_q * (GLA_DK ** -0.5)
    gqk_ref[:, ghd:] = _dot(xb, w_ref[:, _C_GQK + ghd:_C_GV])
    gv_ref[...] = _dot(xb, w_ref[:, _C_GV:_C_LR]).astype(BF16)
    lr = _dot(xb, w_ref[:, _C_LR:_C_END]).astype(BF16)
    logit = _dot(lr, wgate_ref[...]) + bgate_ref[...]
    log_sig = jnp.minimum(logit, 0.0) - jnp.log1p(jnp.exp(-jnp.abs(logit)))
    la_ref[...] = log_sig / GLA_TAU


def _in_proj(x, lw, tabs):
    n = x.shape[0]
    tm = min(INP_TM, n)
    period = tabs["cr"].shape[0] // tm

    def tab_spec(cols):
        return pl.BlockSpec((tm, cols), lambda i: (i % period, 0))

    out_cols = [(2 * RET_HEADS * RET_DK, F32), (RET_HEADS * RET_DV, BF16), (MLA_HEADS * MLA_HEAD_PAD, BF16),
                (MLA_KV_LORA, F32), (LANES, F32), (2 * GLA_HEADS * GLA_DK, F32), (GLA_HEADS * GLA_DV, BF16),
                (GLA_HEADS * GLA_DK, F32)]
    weights = [lw["w1"], lw["wq"], lw["wgate"], lw["bgate"], lw["qn"], lw["kvn"]]
    return pl.pallas_call(
        _in_proj_kernel,
        grid=(n // tm,),
        in_specs=[_row_spec(tm, D_MODEL)] + [_full_spec(w.shape) for w in weights]
        + [tab_spec(512), tab_spec(512), tab_spec(LANES), tab_spec(LANES), tab_spec(LANES), tab_spec(LANES)],
        out_specs=[_row_spec(tm, c) for c, _ in out_cols],
        out_shape=[jax.ShapeDtypeStruct((n, c), d) for c, d in out_cols],
        compiler_params=_params("parallel"),
        name="in_proj",
    )(x, *weights, tabs["cr"], tabs["sr"], tabs["cq"], tabs["sq"], tabs["ck"], tabs["sk"])


KVUP_TM = 512


def _kv_up_kernel(ckv_ref, kr_ref, wk_ref, e_ref, wv_ref, ones_ref, k_ref, v_ref):
    cb = ckv_ref[...].astype(BF16)
    k_ref[...] = (_dot(cb, wk_ref[...]) + _dot(kr_ref[...].astype(BF16), e_ref[...])).astype(BF16)
    v_ref[...] = (_dot(cb, wv_ref[...]) + ones_ref[...]).astype(BF16)


def _kv_up(ckv, krp, lw):
    n = ckv.shape[0]
    tm = KVUP_TM if n % KVUP_TM == 0 else n
    weights = [lw["wk"], lw["e"], lw["wv"], lw["v_ones"]]
    width = MLA_HEADS * MLA_HEAD_PAD
    return pl.pallas_call(
        _kv_up_kernel,
        grid=(n // tm,),
        in_specs=[_row_spec(tm, MLA_KV_LORA), _row_spec(tm, LANES)] + [_full_spec(w.shape) for w in weights],
        out_specs=[_row_spec(tm, width), _row_spec(tm, width)],
        out_shape=[jax.ShapeDtypeStruct((n, width), BF16), jax.ShapeDtypeStruct((n, width), BF16)],
        compiler_params=_params("parallel"),
        name="kv_up",
    )(ckv, krp, *weights)


SCAN_CHUNKS_PER_STEP = 4
SCAN_STREAMS_PER_STEP = 4


def _cumsum_rows(a):
    rows = lax.broadcasted_iota(jnp.int32, a.shape, 0)
    s = 1
    while s < a.shape[0]:
        a = a + jnp.where(rows >= s, pltpu.roll(a, s, 0), 0.0)
        s *= 2
    return a


def _scan_kernel(*refs, heads, dk, dv, n_chunks, nb, has_la, has_s0):
    it = iter(refs)
    qk_ref, v_ref = next(it), next(it)
    la_ref = next(it)
    s0_ref = next(it) if has_s0 else None
    o_ref, sT_ref, st_ref = next(it), next(it), next(it)
    hd = heads * dk
    step = pl.program_id(1)

    @pl.when(step == 0)
    def _():
        if has_s0:
            st_ref[...] = s0_ref[...]
        else:
            st_ref[...] = jnp.zeros_like(st_ref)

    row = lax.broadcasted_iota(jnp.int32, (CHUNK, CHUNK), 0)
    col = lax.broadcasted_iota(jnp.int32, (CHUNK, CHUNK), 1)
    causal = row >= col
    ksl = lambda h: slice(h * dk, (h + 1) * dk)
    vsl = lambda h: slice(h * dv, (h + 1) * dv)
    chains = [(bi, h) for h in range(heads) for bi in range(nb)]

    def chunk(c, carry):
        rows = pl.ds(pl.multiple_of(c * CHUNK, CHUNK), CHUNK)
        qe, ke, kd, el = [], [], [], []
        for bi in range(nb):
            if has_la:
                bc = _cumsum_rows(la_ref[bi, rows, :])
            else:
                steps = lax.broadcasted_iota(jnp.int32, (CHUNK, hd), 0) + 1
                bc = steps.astype(F32) * la_ref[...]
            bl = bc[CHUNK - 1:CHUNK, :]
            q = qk_ref[bi, rows, :hd]
            k = qk_ref[bi, rows, hd:]
            qe.append((q * jnp.exp(bc)).astype(BF16))
            ke.append((k * jnp.exp(-bc)).astype(BF16))
            kd.append((k * jnp.exp(bl - bc)).astype(BF16))
            el.append(jnp.exp(bl))
        att = [lax.dot_general(qe[bi][:, ksl(h)], ke[bi][:, ksl(h)], _NT, preferred_element_type=F32)
               for bi, h in chains]
        cross = [lax.dot_general(qe[bi][:, ksl(h)], st_ref[bi, h].astype(BF16), _NT, preferred_element_type=F32)
                 for bi, h in chains]
        upd = [lax.dot_general(v_ref[bi, rows, vsl(h)], kd[bi][:, ksl(h)], _TN, preferred_element_type=F32)
               for bi, h in chains]
        for n, (bi, h) in enumerate(chains):
            a = jnp.where(causal, att[n], 0.0).astype(BF16)
            o_ref[bi, rows, vsl(h)] = _dot(a, v_ref[bi, rows, vsl(h)]) + cross[n]
        for n, (bi, h) in enumerate(chains):
            st_ref[bi, h] = st_ref[bi, h] * el[bi][:, ksl(h)] + upd[n]
        return carry

    lax.fori_loop(0, n_chunks, chunk, 0)

    @pl.when(step == pl.num_programs(1) - 1)
    def _():
        sT_ref[...] = st_ref[...]


def _scan(qk, v, la, s0T, *, heads, dk, dv):
    b, t, _ = qk.shape
    has_la = la.ndim == 3
    has_s0 = s0T is not None
    nb = min(SCAN_STREAMS_PER_STEP, b)
    ncs = min(SCAN_CHUNKS_PER_STEP, t // CHUNK)
    rows = ncs * CHUNK
    hd, hv = heads * dk, heads * dv

    def seq_spec(cols):
        return pl.BlockSpec((nb, rows, cols), lambda bi, si: (bi, si, 0))

    st_spec = pl.BlockSpec((nb, heads, dv, dk), lambda bi, si: (bi, 0, 0, 0))
    in_specs = [seq_spec(2 * hd), seq_spec(hv), seq_spec(hd) if has_la else _full_spec(la.shape)]
    args = [qk, v, la]
    if has_s0:
        in_specs.append(st_spec)
        args.append(s0T)
    kern = functools.partial(_scan_kernel, heads=heads, dk=dk, dv=dv, n_chunks=ncs, nb=nb,
                             has_la=has_la, has_s0=has_s0)
    return pl.pallas_call(
        kern,
        grid=(b // nb, t // rows),
        in_specs=in_specs,
        out_specs=[seq_spec(hv), st_spec],
        out_shape=[jax.ShapeDtypeStruct((b, t, hv), F32), jax.ShapeDtypeStruct((b, heads, dv, dk), F32)],
        scratch_shapes=[pltpu.VMEM((nb, heads, dv, dk), F32)],
        compiler_params=_params("parallel", "arbitrary"),
        name="scan_h%d_dk%d" % (heads, dk),
    )(*args)


ATT_TQ = 512
ATT_TK = 512
ATT_LOOKAHEAD = 2


def _attn_kernel(i_ref, j_ref, q_ref, k_ref, v_ref, o_ref, m_ref, acc_ref, *, tq, tk, q_chunk0, nk):
    i, j = i_ref[pl.program_id(1)], j_ref[pl.program_id(1)]
    q_lo = q_chunk0 + (i * tq) // CHUNK
    q_hi = q_chunk0 + (i * tq + tq - 1) // CHUNK
    k_lo = (j * tk) // CHUNK
    k_hi = (j * tk + tk - 1) // CHUNK
    j_last = jnp.minimum(nk - 1, ((q_hi + 1) * CHUNK - 1) // tk)

    @pl.when(j == 0)
    def _():
        m_ref[...] = jnp.full_like(m_ref, -jnp.inf)
        acc_ref[...] = jnp.zeros_like(acc_ref)

    def tile(masked):
        if masked:
            qc = q_chunk0 + ((i * tq + lax.broadcasted_iota(jnp.int32, (tq, tk), 0)) >> CHUNK_SHIFT)
            kc = (j * tk + lax.broadcasted_iota(jnp.int32, (tq, tk), 1)) >> CHUNK_SHIFT
            vis = kc <= qc
        def scores(h):
            hs = slice(h * MLA_HEAD_PAD, (h + 1) * MLA_HEAD_PAD)
            return lax.dot_general(q_ref[0, :, hs], k_ref[0, :, hs], _NT, preferred_element_type=F32)

        pending = [scores(h) for h in range(ATT_LOOKAHEAD)]
        for h in range(MLA_HEADS):
            hs = slice(h * MLA_HEAD_PAD, (h + 1) * MLA_HEAD_PAD)
            s = pending.pop(0)
            if h + ATT_LOOKAHEAD < MLA_HEADS:
                pending.append(scores(h + ATT_LOOKAHEAD))
            if masked:
                s = jnp.where(vis, s, -jnp.inf)
            m_prev = m_ref[h]
            m_new = jnp.maximum(m_prev, jnp.max(s, axis=-1, keepdims=True))
            alpha = jnp.exp2(m_prev - m_new)
            p = jnp.concatenate([jnp.exp2(s[:, c * LANES:(c + 1) * LANES] - m_new).astype(BF16)
                                 for c in range(tk // LANES)], axis=1)
            acc_ref[h] = alpha * acc_ref[h] + _dot(p, v_ref[0, :, hs])
            m_ref[h] = m_new

    @pl.when(k_hi <= q_lo)
    def _():
        tile(False)

    @pl.when(jnp.logical_and(k_hi > q_lo, k_lo <= q_hi))
    def _():
        tile(True)

    @pl.when(j == j_last)
    def _():
        for h in range(MLA_HEADS):
            a = acc_ref[h]
            o_ref[0, :, h * MLA_DV:(h + 1) * MLA_DV] = (a[:, :MLA_DV] / a[:, MLA_DV:MLA_DV + 1]).astype(o_ref.dtype)


def _attention(q, k, v, *, q_chunk0, tq, tk):
    b, t_q, _ = q.shape
    t_k = k.shape[1]
    nq, nk = t_q // tq, t_k // tk
    width = MLA_HEADS * MLA_HEAD_PAD
    pairs = [(i, j) for i in range(nq)
             for j in range(min(nk - 1, ((q_chunk0 + (i * tq + tq - 1) // CHUNK + 1) * CHUNK - 1) // tk) + 1)]
    i_tab = jnp.asarray([p[0] for p in pairs], jnp.int32)
    j_tab = jnp.asarray([p[1] for p in pairs], jnp.int32)

    kern = functools.partial(_attn_kernel, tq=tq, tk=tk, q_chunk0=q_chunk0, nk=nk)
    grid_spec = pltpu.PrefetchScalarGridSpec(
        num_scalar_prefetch=2,
        grid=(b, len(pairs)),
        in_specs=[pl.BlockSpec((1, tq, width), lambda bi, p, it, jt: (bi, it[p], 0)),
                  pl.BlockSpec((1, tk, width), lambda bi, p, it, jt: (bi, jt[p], 0)),
                  pl.BlockSpec((1, tk, width), lambda bi, p, it, jt: (bi, jt[p], 0))],
        out_specs=pl.BlockSpec((1, tq, MLA_HEADS * MLA_DV), lambda bi, p, it, jt: (bi, it[p], 0)),
        scratch_shapes=[pltpu.VMEM((MLA_HEADS, tq, LANES), F32), pltpu.VMEM((MLA_HEADS, tq, MLA_HEAD_PAD), F32)],
    )
    return pl.pallas_call(
        kern,
        grid_spec=grid_spec,
        out_shape=jax.ShapeDtypeStruct((b, t_q, MLA_HEADS * MLA_DV), BF16),
        compiler_params=_params("parallel", "arbitrary"),
        name="mla_attention",
    )(i_tab, j_tab, q, k, v)


OUT_TM = 512
OUT_SUB = 256


def _out_proj_kernel(x_ref, or_ref, om_ref, og_ref, wg3_ref, wro_ref, wmo_ref, wgo_ref, wout_ref,
                     rgn_ref, ggn_ref, g_ref, b_ref, o_ref, hr_ref, hg_ref):
    tm = x_ref.shape[0]
    sub = min(OUT_SUB, tm)
    for r in range(tm // sub):
        rs = slice(r * sub, (r + 1) * sub)
        x = x_ref[rs, :]
        xb = x.astype(BF16)

        ret_gate = _dot(xb, wg3_ref[:, _G_RG:_G_GOG])
        for h in range(RET_HEADS):
            sl = slice(h * RET_DV, (h + 1) * RET_DV)
            gate = ret_gate[:, sl]
            o = or_ref[rs, sl]
            oc = o - jnp.mean(o, axis=-1, keepdims=True)
            normed = oc * lax.rsqrt(jnp.mean(oc * oc, axis=-1, keepdims=True) + EPS) * rgn_ref[:, sl]
            hr_ref[rs, sl] = (normed * (gate * jax.nn.sigmoid(gate))).astype(BF16)
        y_r = _dot(hr_ref[rs, :], wro_ref[...])

        y_m = _dot(om_ref[rs, :], wmo_ref[...])

        for h in range(GLA_HEADS):
            sl = slice(h * GLA_DV, (h + 1) * GLA_DV)
            gate = _dot(xb, wg3_ref[:, _G_GOG + h * GLA_DV:_G_GOG + (h + 1) * GLA_DV])
            o = og_ref[rs, sl]
            normed = o * lax.rsqrt(jnp.mean(o * o, axis=-1, keepdims=True) + EPS) * ggn_ref[:, sl]
            hg_ref[rs, sl] = (normed * (gate * jax.nn.sigmoid(gate))).astype(BF16)
        y_g = _dot(hg_ref[rs, :], wgo_ref[...])

        def branch_gate(idx):
            lo = _G_BR + idx * D_MODEL
            return jax.nn.sigmoid(_dot(xb, wg3_ref[:, lo:lo + D_MODEL]))

        mix = branch_gate(0) * y_r + branch_gate(1) * y_m + branch_gate(2) * y_g
        y = _dot(mix.astype(BF16), wout_ref[...])
        o_ref[rs, :] = _layer_norm(ALPHA * x + y, g_ref[...], b_ref[...])


def _out_proj_ln(x, o_r, o_m, o_g, lw, g, b):
    n = x.shape[0]
    tm = min(OUT_TM, n)
    weights = [lw["wg3"], lw["w_ret_o"], lw["w_mla_o"], lw["w_gla_o"], lw["w_out"], lw["ret_gn"], lw["gla_gn"], g, b]
    return pl.pallas_call(
        _out_proj_kernel,
        grid=(n // tm,),
        in_specs=[_row_spec(tm, D_MODEL), _row_spec(tm, RET_HEADS * RET_DV), _row_spec(tm, MLA_HEADS * MLA_DV),
                  _row_spec(tm, GLA_HEADS * GLA_DV)] + [_full_spec(w.shape) for w in weights],
        out_specs=_row_spec(tm, D_MODEL),
        out_shape=jax.ShapeDtypeStruct((n, D_MODEL), F32),
        scratch_shapes=[pltpu.VMEM((tm, RET_HEADS * RET_DV), BF16), pltpu.VMEM((tm, GLA_HEADS * GLA_DV), BF16)],
        compiler_params=_params("parallel"),
        name="out_proj_ln",
    )(x, o_r, o_m, o_g, *weights)


def _prep_layer(w, l):
    offs = np.cumsum((0,) + IN_SPLITS)
    w_in = w["w_in"][l]
    (r_q, r_k, r_v, r_g, m_cq, m_ckv, m_kr, g_q, g_k, g_v, g_lr, g_og, br) = [
        w_in[:, offs[i]:offs[i + 1]] for i in range(len(IN_SPLITS))]

    def pad_cols(a, n):
        return jnp.pad(a, ((0, 0), (0, n - a.shape[1])))

    w1 = jnp.concatenate([r_q, r_k, r_v, m_cq, pad_cols(m_kr, LANES), m_ckv, g_q, g_k, g_v,
                          pad_cols(g_lr, LANES)], axis=1).astype(BF16)
    dq = MLA_NOPE + MLA_ROPE
    wq = jnp.pad(w["mla_w_q_up"][l].reshape(MLA_Q_LORA, MLA_HEADS, dq),
                 ((0, 0), (0, 0), (0, MLA_HEAD_PAD - dq))).reshape(MLA_Q_LORA, MLA_HEADS * MLA_HEAD_PAD).astype(BF16)
    kv = w["mla_w_kv_up"][l].reshape(MLA_KV_LORA, MLA_HEADS, MLA_NOPE + MLA_DV)
    wk = jnp.pad(kv[:, :, :MLA_NOPE], ((0, 0), (0, 0), (0, MLA_HEAD_PAD - MLA_NOPE))).reshape(
        MLA_KV_LORA, MLA_HEADS * MLA_HEAD_PAD).astype(BF16)
    wv = jnp.pad(kv[:, :, MLA_NOPE:], ((0, 0), (0, 0), (0, MLA_HEAD_PAD - MLA_DV))).reshape(
        MLA_KV_LORA, MLA_HEADS * MLA_HEAD_PAD).astype(BF16)
    place = np.zeros((LANES, MLA_HEADS * MLA_HEAD_PAD), np.float32)
    v_ones = np.zeros((1, MLA_HEADS * MLA_HEAD_PAD), np.float32)
    for h in range(MLA_HEADS):
        place[np.arange(MLA_ROPE), h * MLA_HEAD_PAD + MLA_NOPE + np.arange(MLA_ROPE)] = 1.0
        v_ones[0, h * MLA_HEAD_PAD + MLA_DV] = 1.0
    up1, up2 = w["ffn1_up"][l], w["ffn2_up"][l]
    return {
        "w1": w1, "wq": wq, "wk": wk, "wv": wv, "e": jnp.asarray(place, BF16), "v_ones": jnp.asarray(v_ones),
        "wgate": jnp.pad(w["gla_w_gate_up"][l], ((0, LANES - GLA_GATE_RANK), (0, 0))).astype(BF16),
        "bgate": w["gla_b_gate"][l][None, :],
        "qn": w["mla_q_norm_g"][l][None, :], "kvn": w["mla_kv_norm_g"][l][None, :],
        "wg3": jnp.concatenate([r_g, g_og, br], axis=1).astype(BF16),
        "w_ret_o": w["w_ret_o"][l].astype(BF16), "w_mla_o": w["w_mla_o"][l].astype(BF16),
        "w_gla_o": w["w_gla_o"][l].astype(BF16), "w_out": w["w_out"][l].astype(BF16),
        "ret_gn": w["ret_gn_g"][l][None, :], "gla_gn": w["gla_gn_g"][l][None, :],
        "f1g": up1[:, :D_FF].astype(BF16), "f1u": up1[:, D_FF:].astype(BF16), "f1d": w["ffn1_down"][l].astype(BF16),
        "f2g": up2[:, :D_FF].astype(BF16), "f2u": up2[:, D_FF:].astype(BF16), "f2d": w["ffn2_down"][l].astype(BF16),
        "ln_g": w["ln_g"][l], "ln_b": w["ln_b"][l],
    }


def _rope_tables(pos, tm):
    def cos_sin(half):
        inv = ROPE_THETA ** (-jnp.arange(half, dtype=F32) / half)
        ang = pos.astype(F32)[:, None] * inv[None, :]
        return jnp.cos(ang), jnp.sin(ang)

    t = pos.shape[0]
    c32, s32 = cos_sin(RET_DK // 2)
    c16, s16 = cos_sin(MLA_ROPE // 2)
    cr_h = jnp.tile(jnp.concatenate([c32, c32], axis=1), (1, RET_HEADS))
    sr_h = jnp.tile(jnp.concatenate([-s32, s32], axis=1), (1, RET_HEADS))
    k_scale = RET_DK ** -0.5
    q_scale = (MLA_NOPE + MLA_ROPE) ** -0.5 * float(np.log2(np.e))
    zeros = lambda n: jnp.zeros((t, n), F32)
    tabs = {
        "cr": jnp.concatenate([cr_h, cr_h * k_scale], axis=1),
        "sr": jnp.concatenate([sr_h, sr_h * k_scale], axis=1),
        "cq": jnp.concatenate([jnp.ones((t, MLA_NOPE), F32), c16, c16, zeros(LANES - MLA_NOPE - MLA_ROPE)], axis=1) * q_scale,
        "sq": jnp.concatenate([zeros(MLA_NOPE), -s16, s16, zeros(LANES - MLA_NOPE - MLA_ROPE)], axis=1) * q_scale,
        "ck": jnp.concatenate([c16, c16, zeros(LANES - MLA_ROPE)], axis=1),
        "sk": jnp.concatenate([-s16, s16, zeros(LANES - MLA_ROPE)], axis=1),
    }
    if t < tm:
        tabs = {k: jnp.tile(v, (tm // t, 1)) for k, v in tabs.items()}
    return tabs


def _group_layer(x, b, t, lw, tabs, past, q_chunk0):
    n = b * t
    x = _ffn_ln(x, lw["f1g"], lw["f1u"], lw["f1d"], lw["ln_g"][0:1], lw["ln_b"][0:1])
    rqk, rv, qm, ckv, krp, gqk, gv, la = _in_proj(x, lw, tabs)

    log_gamma = jnp.log(1.0 - 2.0 ** (-5.0 - jnp.arange(RET_HEADS, dtype=F32)))
    ret_la = jnp.repeat(log_gamma, RET_DK)[None, :]
    s_ret0 = None if past is None else jnp.swapaxes(past[2], -1, -2)
    s_gla0 = None if past is None else jnp.swapaxes(past[3], -1, -2)
    o_r, s_retT = _scan(rqk.reshape(b, t, -1), rv.reshape(b, t, -1), ret_la, s_ret0,
                        heads=RET_HEADS, dk=RET_DK, dv=RET_DV)
    o_g, s_glaT = _scan(gqk.reshape(b, t, -1), gv.reshape(b, t, -1), la.reshape(b, t, -1), s_gla0,
                        heads=GLA_HEADS, dk=GLA_DK, dv=GLA_DV)

    if past is None:
        ckv_all, kr_all, t_k = ckv, krp, t
        tq, tk = min(ATT_TQ, t), min(ATT_TK, t)
    else:
        t_past = past[0].shape[1]
        t_k = -(-(t_past + t) // LANES) * LANES
        pad = t_k - t_past - t
        ckv_all = jnp.concatenate([past[0], ckv.reshape(b, t, -1), jnp.zeros((b, pad, MLA_KV_LORA), F32)], axis=1)
        kr_past = jnp.pad(past[1], ((0, 0), (0, 0), (0, LANES - MLA_ROPE)))
        kr_all = jnp.concatenate([kr_past, krp.reshape(b, t, -1), jnp.zeros((b, pad, LANES), F32)], axis=1)
        ckv_all, kr_all = ckv_all.reshape(b * t_k, -1), kr_all.reshape(b * t_k, -1)
        tq, tk = t, t_k
    k_m, v_m = _kv_up(ckv_all, kr_all, lw)
    o_m = _attention(qm.reshape(b, t, -1), k_m.reshape(b, t_k, -1), v_m.reshape(b, t_k, -1),
                     q_chunk0=q_chunk0, tq=tq, tk=tk)

    x = _out_proj_ln(x, o_r.reshape(n, -1), o_m.reshape(n, -1), o_g.reshape(n, -1), lw,
                     lw["ln_g"][1:2], lw["ln_b"][1:2])
    x = _ffn_ln(x, lw["f2g"], lw["f2u"], lw["f2d"], lw["ln_g"][2:3], lw["ln_b"][2:3])
    new_state = (ckv.reshape(b, t, -1), krp[:, :MLA_ROPE].reshape(b, t, -1),
                 jnp.swapaxes(s_retT, -1, -2), jnp.swapaxes(s_glaT, -1, -2))
    return x, new_state


def kernel(x_prompt, x_sample, cache_mla_ckv, cache_mla_krope, state_ret, state_gla, w_in, ret_gn_g, mla_q_norm_g, mla_w_q_up, mla_kv_norm_g, mla_w_kv_up, gla_w_gate_up, gla_b_gate, gla_gn_g, w_ret_o, w_mla_o, w_gla_o, w_out, ffn1_up, ffn1_down, ffn2_up, ffn2_down, ln_g, ln_b):
    w = dict(w_in=w_in, ret_gn_g=ret_gn_g, mla_q_norm_g=mla_q_norm_g, mla_w_q_up=mla_w_q_up,
             mla_kv_norm_g=mla_kv_norm_g, mla_w_kv_up=mla_w_kv_up, gla_w_gate_up=gla_w_gate_up,
             gla_b_gate=gla_b_gate, gla_gn_g=gla_gn_g, w_ret_o=w_ret_o, w_mla_o=w_mla_o, w_gla_o=w_gla_o,
             w_out=w_out, ffn1_up=ffn1_up, ffn1_down=ffn1_down, ffn2_up=ffn2_up, ffn2_down=ffn2_down,
             ln_g=ln_g, ln_b=ln_b)
    bp, tp, _ = x_prompt.shape
    bs, ts, _ = x_sample.shape
    t_past = cache_mla_ckv.shape[2]
    tabs_p = _rope_tables(jnp.arange(tp), min(INP_TM, bp * tp))
    tabs_s = _rope_tables(t_past + jnp.arange(ts), min(INP_TM, bs * ts))
    xp = x_prompt.reshape(bp * tp, D_MODEL)
    xs = x_sample.reshape(bs * ts, D_MODEL)
    st_p, st_s = [], []
    for l in range(DEPTH):
        lw = _prep_layer(w, l)
        xp, st = _group_layer(xp, bp, tp, lw, tabs_p, None, 0)
        st_p.append(st)
        past = (cache_mla_ckv[l], cache_mla_krope[l], state_ret[l], state_gla[l])
        xs, st = _group_layer(xs, bs, ts, lw, tabs_s, past, t_past // CHUNK)
        st_s.append(st)
    stack = lambda sts, i: jnp.stack([s[i] for s in sts])
    return (xp.reshape(bp, tp, D_MODEL), xs.reshape(bs, ts, D_MODEL),
            stack(st_p, 0), stack(st_p, 1), stack(st_p, 2), stack(st_p, 3),
            stack(st_s, 0), stack(st_s, 1), stack(st_s, 2), stack(st_s, 3))
```

```python
import functools

import numpy as np
import jax
import jax.numpy as jnp
from jax import lax
from jax.experimental import pallas as pl
from jax.experimental.pallas import tpu as pltpu

F32 = jnp.float32
BF16 = jnp.bfloat16

D_MODEL = 1024
DEPTH = 2
CHUNK = 64
CHUNK_SHIFT = 6
ALPHA = (2 * DEPTH) ** 0.25
EPS = 1e-5
ROPE_THETA = 10000.0
RET_HEADS, RET_DK, RET_DV = 4, 64, 128
MLA_HEADS, MLA_Q_LORA, MLA_KV_LORA, MLA_NOPE, MLA_ROPE, MLA_DV = 8, 384, 256, 64, 32, 64
GLA_HEADS, GLA_DK, GLA_DV, GLA_GATE_RANK, GLA_TAU = 4, 128, 256, 16, 16.0
D_FF = 2816
N_BRANCH = 3
IN_SPLITS = (RET_HEADS * RET_DK, RET_HEADS * RET_DK, RET_HEADS * RET_DV, RET_HEADS * RET_DV,
             MLA_Q_LORA, MLA_KV_LORA, MLA_ROPE,
             GLA_HEADS * GLA_DK, GLA_HEADS * GLA_DK, GLA_HEADS * GLA_DV, GLA_GATE_RANK, GLA_HEADS * GLA_DV,
             N_BRANCH * D_MODEL)

LANES = 128
MLA_HEAD_PAD = LANES
VMEM_LIMIT = 56 * 1024 * 1024

_C_RQK, _C_RV, _C_CQ, _C_KR, _C_CKV, _C_GQK, _C_GV, _C_LR, _C_END = (
    0, 512, 1024, 1408, 1536, 1792, 2816, 3840, 3968)
_G_GOG, _G_BR = 0, 1024

_NT = (((1,), (1,)), ((), ()))
_TN = (((0,), (0,)), ((), ()))


def _params(*sem):
    return pltpu.CompilerParams(dimension_semantics=sem, vmem_limit_bytes=VMEM_LIMIT)


def _dot(a, b):
    return jnp.dot(a, b, preferred_element_type=F32)


def _layer_norm(z, g, b):
    mu = jnp.mean(z, axis=-1, keepdims=True)
    zc = z - mu
    var = jnp.mean(zc * zc, axis=-1, keepdims=True)
    return zc * lax.rsqrt(var + EPS) * g + b


def _row_spec(tm, cols):
    return pl.BlockSpec((tm, cols), lambda i: (i, 0))


def _full_spec(shape):
    return pl.BlockSpec(shape, lambda *_: (0,) * len(shape), pipeline_mode=pl.Buffered(1))


FFN_TM = 512
FFN_FC = 256


def _ffn_ln_kernel(x_ref, wup_ref, wd_ref, g_ref, b_ref, o_ref, h_ref):
    x = x_ref[...]
    xb = x.astype(BF16)
    for c in range(D_FF // FFN_FC):
        sl = slice(c * FFN_FC, (c + 1) * FFN_FC)
        gate = _dot(xb, wup_ref[:, sl])
        up = _dot(xb, wup_ref[:, D_FF + c * FFN_FC:D_FF + (c + 1) * FFN_FC])
        h_ref[:, sl] = (gate * jax.nn.sigmoid(gate) * up).astype(BF16)
    y = _dot(h_ref[...], wd_ref[...])
    o_ref[...] = _layer_norm(ALPHA * x + 0.5 * y, g_ref[...], b_ref[...])


def _ffn_ln(x, wup, wd, g, b):
    n = x.shape[0]
    tm = min(FFN_TM, n)
    return pl.pallas_call(
        _ffn_ln_kernel,
        grid=(n // tm,),
        in_specs=[_row_spec(tm, D_MODEL), _full_spec(wup.shape), _full_spec(wd.shape),
                  _full_spec(g.shape), _full_spec(b.shape)],
        out_specs=_row_spec(tm, D_MODEL),
        out_shape=jax.ShapeDtypeStruct((n, D_MODEL), F32),
        scratch_shapes=[pltpu.VMEM((tm, D_FF), BF16)],
        compiler_params=_params("parallel"),
        name="ffn_ln",
    )(x, wup, wd, g, b)


INP_TM = 512


def _swap_halves(x, first_mask, half):
    return jnp.where(first_mask, pltpu.roll(x, LANES - half, 1), pltpu.roll(x, half, 1))


def _in_proj_kernel(x_ref, w_ref, wq_ref, wgate_ref, bgate_ref, qn_ref, kvn_ref,
                    cr_ref, sr_ref, cq_ref, sq_ref, ck_ref, sk_ref, *rest):
    rqk_ref, rv_ref, qm_ref, ckv_ref, kr_ref, kr_out_ref, gqk_ref, gv_ref, la_ref = rest[-9:]
    xb = x_ref[...].astype(BF16)
    tm = xb.shape[0]
    lane = lax.broadcasted_iota(jnp.int32, (tm, LANES), 1)

    ret_first = (lane & (RET_DK - 1)) < RET_DK // 2
    h_rqk = _dot(xb, w_ref[:, _C_RQK:_C_RV])
    for c in range(2 * RET_HEADS * RET_DK // LANES):
        sl = slice(c * LANES, (c + 1) * LANES)
        h = h_rqk[:, sl]
        rqk_ref[:, sl] = h * cr_ref[:, sl] + _swap_halves(h, ret_first, RET_DK // 2) * sr_ref[:, sl]
    rv_ref[...] = _dot(xb, w_ref[:, _C_RV:_C_CQ]).astype(BF16)

    h_cq_kr = _dot(xb, w_ref[:, _C_CQ:_C_CKV])
    hq = h_cq_kr[:, :MLA_Q_LORA]
    cq = hq * lax.rsqrt(jnp.mean(hq * hq, axis=-1, keepdims=True) + EPS) * qn_ref[...]
    q_first = lane < MLA_NOPE + MLA_ROPE // 2
    q_up = _dot(cq.astype(BF16), wq_ref[...])
    for h_i in range(MLA_HEADS):
        sl = slice(h_i * MLA_HEAD_PAD, (h_i + 1) * MLA_HEAD_PAD)
        qh = q_up[:, sl]
        qm_ref[:, sl] = (qh * cq_ref[...] + _swap_halves(qh, q_first, MLA_ROPE // 2) * sq_ref[...]).astype(BF16)

    hkr = h_cq_kr[:, MLA_Q_LORA:]
    kr = hkr * ck_ref[...] + _swap_halves(hkr, lane < MLA_ROPE // 2, MLA_ROPE // 2) * sk_ref[...]
    kr_ref[...] = kr
    kr_out_ref[...] = kr[:, :MLA_ROPE]
    hkv = _dot(xb, w_ref[:, _C_CKV:_C_GQK])
    ckv_ref[...] = hkv * lax.rsqrt(jnp.mean(hkv * hkv, axis=-1, keepdims=True) + EPS) * kvn_ref[...]

    ghd = GLA_HEADS * GLA_DK
    gla_q = _dot(xb, w_ref[:, _C_GQK:_C_GQK + ghd])
    gqk_ref[:, :ghd] = gla_q * (GLA_DK ** -0.5)
    gqk_ref[:, ghd:] = _dot(xb, w_ref[:, _C_GQK + ghd:_C_GV])
    gv_ref[...] = _dot(xb, w_ref[:, _C_GV:_C_LR]).astype(BF16)
    lr = _dot(xb, w_ref[:, _C_LR:_C_END]).astype(BF16)
    logit = _dot(lr, wgate_ref[...]) + bgate_ref[...]
    log_sig = jnp.minimum(logit, 0.0) - jnp.log1p(jnp.exp(-jnp.abs(logit)))
    la_ref[...] = log_sig / GLA_TAU


def _in_proj(x, lw, tabs, layer, carried):
    n = x.shape[0]
    tm = min(INP_TM, n)
    period = tabs["cr"].shape[0] // tm

    def tab_spec(cols):
        return pl.BlockSpec((tm, cols), lambda i: (i % period, 0))

    def layer_spec(cols):
        return pl.BlockSpec((None, tm, cols), lambda i: (layer, i, 0))

    rows = lambda cols, dtype: (_row_spec(tm, cols), jax.ShapeDtypeStruct((n, cols), dtype))
    stacked = lambda cols: (layer_spec(cols), jax.ShapeDtypeStruct((DEPTH, n, cols), F32))
    outs = [rows(2 * RET_HEADS * RET_DK, F32), rows(RET_HEADS * RET_DV, BF16), rows(MLA_HEADS * MLA_HEAD_PAD, BF16),
            stacked(MLA_KV_LORA), rows(LANES, F32), stacked(MLA_ROPE), rows(2 * GLA_HEADS * GLA_DK, F32),
            rows(GLA_HEADS * GLA_DV, BF16), rows(GLA_HEADS * GLA_DK, F32)]
    weights = [lw["w1"], lw["wq"], lw["wgate"], lw["bgate"], lw["qn"], lw["kvn"]]
    tables = [tabs["cr"], tabs["sr"], tabs["cq"], tabs["sq"], tabs["ck"], tabs["sk"]]
    in_specs = ([_row_spec(tm, D_MODEL)] + [_full_spec(w.shape) for w in weights]
                + [tab_spec(t.shape[1]) for t in tables])
    args = [x, *weights, *tables]
    aliases = {}
    if carried is not None:
        aliases = {len(args): 3, len(args) + 1: 5}
        in_specs += [pl.BlockSpec(memory_space=pl.ANY)] * 2
        args += list(carried)
    return pl.pallas_call(
        _in_proj_kernel,
        grid=(n // tm,),
        in_specs=in_specs,
        out_specs=[o[0] for o in outs],
        out_shape=[o[1] for o in outs],
        input_output_aliases=aliases,
        compiler_params=_params("parallel"),
        name="in_proj",
    )(*args)


KVUP_TM = 512


def _kv_up_kernel(ckv_ref, kr_ref, wk_ref, e_ref, wv_ref, ones_ref, k_ref, v_ref):
    cb = ckv_ref[...].astype(BF16)
    k_ref[...] = (_dot(cb, wk_ref[...]) + _dot(kr_ref[...].astype(BF16), e_ref[...])).astype(BF16)
    v_ref[...] = (_dot(cb, wv_ref[...]) + ones_ref[...]).astype(BF16)


def _kv_up(ckv, kr, lw, *, n_rows, ckv_row0=0, kr_row0=0):
    tm = KVUP_TM if n_rows % KVUP_TM == 0 else n_rows
    kr_cols = kr.shape[1]
    weights = [lw["wk"], lw["e"][:kr_cols], lw["wv"], lw["v_ones"]]
    width = MLA_HEADS * MLA_HEAD_PAD

    def in_spec(cols, row0):
        return pl.BlockSpec((tm, cols), lambda i: (i + row0 // tm, 0))

    return pl.pallas_call(
        _kv_up_kernel,
        grid=(n_rows // tm,),
        in_specs=[in_spec(MLA_KV_LORA, ckv_row0), in_spec(kr_cols, kr_row0)] + [_full_spec(w.shape) for w in weights],
        out_specs=[_row_spec(tm, width), _row_spec(tm, width)],
        out_shape=[jax.ShapeDtypeStruct((n_rows, width), BF16), jax.ShapeDtypeStruct((n_rows, width), BF16)],
        compiler_params=_params("parallel"),
        name="kv_up",
    )(ckv, kr, *weights)


SCAN_CHUNKS_PER_STEP = 4
SCAN_STREAMS_PER_STEP = 4


def _cumsum_rows(a):
    rows = lax.broadcasted_iota(jnp.int32, a.shape, 0)
    s = 1
    while s < a.shape[0]:
        a = a + jnp.where(rows >= s, pltpu.roll(a, s, 0), 0.0)
        s *= 2
    return a


def _scan_kernel(*refs, heads, dk, dv, n_chunks, nb, has_la, has_s0):
    it = iter(refs)
    qk_ref, v_ref = next(it), next(it)
    la_ref = next(it)
    s0_ref = next(it) if has_s0 else None
    o_ref, sT_ref, st_ref = next(it), next(it), next(it)
    hd = heads * dk
    step = pl.program_id(1)

    @pl.when(step == 0)
    def _():
        if has_s0:
            st_ref[...] = s0_ref[...]
        else:
            st_ref[...] = jnp.zeros_like(st_ref)

    row = lax.broadcasted_iota(jnp.int32, (CHUNK, CHUNK), 0)
    col = lax.broadcasted_iota(jnp.int32, (CHUNK, CHUNK), 1)
    causal = row >= col
    ksl = lambda h: slice(h * dk, (h + 1) * dk)
    vsl = lambda h: slice(h * dv, (h + 1) * dv)
    chains = [(bi, h) for h in range(heads) for bi in range(nb)]

    def chunk(c, carry):
        rows = pl.ds(pl.multiple_of(c * CHUNK, CHUNK), CHUNK)
        qe, ke, kd, el = [], [], [], []
        for bi in range(nb):
            if has_la:
                bc = _cumsum_rows(la_ref[bi, rows, :])
            else:
                steps = lax.broadcasted_iota(jnp.int32, (CHUNK, hd), 0) + 1
                bc = steps.astype(F32) * la_ref[...]
            bl = bc[CHUNK - 1:CHUNK, :]
            q = qk_ref[bi, rows, :hd]
            k = qk_ref[bi, rows, hd:]
            qe.append((q * jnp.exp(bc)).astype(BF16))
            ke.append((k * jnp.exp(-bc)).astype(BF16))
            kd.append((k * jnp.exp(bl - bc)).astype(BF16))
            el.append(jnp.exp(bl))
        att = [lax.dot_general(qe[bi][:, ksl(h)], ke[bi][:, ksl(h)], _NT, preferred_element_type=F32)
               for bi, h in chains]
        cross = [lax.dot_general(qe[bi][:, ksl(h)], st_ref[bi, h].astype(BF16), _NT, preferred_element_type=F32)
                 for bi, h in chains]
        upd = [lax.dot_general(v_ref[bi, rows, vsl(h)], kd[bi][:, ksl(h)], _TN, preferred_element_type=F32)
               for bi, h in chains]
        for n, (bi, h) in enumerate(chains):
            a = jnp.where(causal, att[n], 0.0).astype(BF16)
            o_ref[bi, rows, vsl(h)] = _dot(a, v_ref[bi, rows, vsl(h)]) + cross[n]
        for n, (bi, h) in enumerate(chains):
            st_ref[bi, h] = st_ref[bi, h] * el[bi][:, ksl(h)] + upd[n]
        return carry

    lax.fori_loop(0, n_chunks, chunk, 0)

    @pl.when(step == pl.num_programs(1) - 1)
    def _():
        sT_ref[...] = st_ref[...]


def _scan(qk, v, la, s0T, *, heads, dk, dv):
    b, t, _ = qk.shape
    has_la = la.ndim == 3
    has_s0 = s0T is not None
    nb = min(SCAN_STREAMS_PER_STEP, b)
    ncs = min(SCAN_CHUNKS_PER_STEP, t // CHUNK)
    rows = ncs * CHUNK
    hd, hv = heads * dk, heads * dv

    def seq_spec(cols):
        return pl.BlockSpec((nb, rows, cols), lambda bi, si: (bi, si, 0))

    st_spec = pl.BlockSpec((nb, heads, dv, dk), lambda bi, si: (bi, 0, 0, 0))
    in_specs = [seq_spec(2 * hd), seq_spec(hv), seq_spec(hd) if has_la else _full_spec(la.shape)]
    args = [qk, v, la]
    if has_s0:
        in_specs.append(st_spec)
        args.append(s0T)
    kern = functools.partial(_scan_kernel, heads=heads, dk=dk, dv=dv, n_chunks=ncs, nb=nb,
                             has_la=has_la, has_s0=has_s0)
    return pl.pallas_call(
        kern,
        grid=(b // nb, t // rows),
        in_specs=in_specs,
        out_specs=[seq_spec(hv), st_spec],
        out_shape=[jax.ShapeDtypeStruct((b, t, hv), F32), jax.ShapeDtypeStruct((b, heads, dv, dk), F32)],
        scratch_shapes=[pltpu.VMEM((nb, heads, dv, dk), F32)],
        compiler_params=_params("parallel", "arbitrary"),
        name="scan_h%d_dk%d" % (heads, dk),
    )(*args)


ATT_TQ = 512
ATT_TK = 512
ATT_LOOKAHEAD = 2


def _attn_kernel(i_ref, j_ref, q_ref, k_ref, v_ref, o_ref, m_ref, acc_ref, *, tq, tk, nk):
    i, j = i_ref[pl.program_id(1)], j_ref[pl.program_id(1)]
    q_lo = (i * tq) // CHUNK
    q_hi = (i * tq + tq - 1) // CHUNK
    k_lo = (j * tk) // CHUNK
    k_hi = (j * tk + tk - 1) // CHUNK
    j_last = jnp.minimum(nk - 1, ((q_hi + 1) * CHUNK - 1) // tk)

    @pl.when(j == 0)
    def _():
        m_ref[...] = jnp.full_like(m_ref, -jnp.inf)
        acc_ref[...] = jnp.zeros_like(acc_ref)

    def tile(masked):
        if masked:
            qc = (i * tq + lax.broadcasted_iota(jnp.int32, (tq, tk), 0)) >> CHUNK_SHIFT
            kc = (j * tk + lax.broadcasted_iota(jnp.int32, (tq, tk), 1)) >> CHUNK_SHIFT
            vis = kc <= qc

        def scores(h):
            hs = slice(h * MLA_HEAD_PAD, (h + 1) * MLA_HEAD_PAD)
            return lax.dot_general(q_ref[0, :, hs], k_ref[0, :, hs], _NT, preferred_element_type=F32)

        pending = [scores(h) for h in range(ATT_LOOKAHEAD)]
        for h in range(MLA_HEADS):
            hs = slice(h * MLA_HEAD_PAD, (h + 1) * MLA_HEAD_PAD)
            s = pending.pop(0)
            if h + ATT_LOOKAHEAD < MLA_HEADS:
                pending.append(scores(h + ATT_LOOKAHEAD))
            if masked:
                s = jnp.where(vis, s, -jnp.inf)
            m_prev = m_ref[h]
            m_new = jnp.maximum(m_prev, jnp.max(s, axis=-1, keepdims=True))
            alpha = jnp.exp2(m_prev - m_new)
            p = jnp.concatenate([jnp.exp2(s[:, c * LANES:(c + 1) * LANES] - m_new).astype(BF16)
                                 for c in range(tk // LANES)], axis=1)
            acc_ref[h] = alpha * acc_ref[h] + _dot(p, v_ref[0, :, hs])
            m_ref[h] = m_new

    @pl.when(k_hi <= q_lo)
    def _():
        tile(False)

    @pl.when(jnp.logical_and(k_hi > q_lo, k_lo <= q_hi))
    def _():
        tile(True)

    @pl.when(j == j_last)
    def _():
        for h in range(MLA_HEADS):
            a = acc_ref[h]
            o_ref[0, :, h * MLA_DV:(h + 1) * MLA_DV] = (a[:, :MLA_DV] / a[:, MLA_DV:MLA_DV + 1]).astype(o_ref.dtype)


def _attention(q, k, v, *, tq, tk):
    b, t_q, _ = q.shape
    t_k = k.shape[1]
    nq, nk = t_q // tq, t_k // tk
    width = MLA_HEADS * MLA_HEAD_PAD
    pairs = [(i, j) for i in range(nq)
             for j in range(min(nk - 1, (((i * tq + tq - 1) // CHUNK + 1) * CHUNK - 1) // tk) + 1)]
    i_tab = jnp.asarray([p[0] for p in pairs], jnp.int32)
    j_tab = jnp.asarray([p[1] for p in pairs], jnp.int32)

    kern = functools.partial(_attn_kernel, tq=tq, tk=tk, nk=nk)
    grid_spec = pltpu.PrefetchScalarGridSpec(
        num_scalar_prefetch=2,
        grid=(b, len(pairs)),
        in_specs=[pl.BlockSpec((1, tq, width), lambda bi, p, it, jt: (bi, it[p], 0)),
                  pl.BlockSpec((1, tk, width), lambda bi, p, it, jt: (bi, jt[p], 0)),
                  pl.BlockSpec((1, tk, width), lambda bi, p, it, jt: (bi, jt[p], 0))],
        out_specs=pl.BlockSpec((1, tq, MLA_HEADS * MLA_DV), lambda bi, p, it, jt: (bi, it[p], 0)),
        scratch_shapes=[pltpu.VMEM((MLA_HEADS, tq, LANES), F32), pltpu.VMEM((MLA_HEADS, tq, MLA_HEAD_PAD), F32)],
    )
    return pl.pallas_call(
        kern,
        grid_spec=grid_spec,
        out_shape=jax.ShapeDtypeStruct((b, t_q, MLA_HEADS * MLA_DV), BF16),
        compiler_params=_params("parallel", "arbitrary"),
        name="mla_attention",
    )(i_tab, j_tab, q, k, v)


def _attn_cached_kernel(q_ref, kp_ref, vp_ref, kn_ref, vn_ref, o_ref):
    t_new = q_ref.shape[1]
    row = lax.broadcasted_iota(jnp.int32, (t_new, t_new), 0) >> CHUNK_SHIFT
    col = lax.broadcasted_iota(jnp.int32, (t_new, t_new), 1) >> CHUNK_SHIFT
    vis_new = col <= row
    for h in range(MLA_HEADS):
        hs = slice(h * MLA_HEAD_PAD, (h + 1) * MLA_HEAD_PAD)
        q = q_ref[0, :, hs]
        s_past = lax.dot_general(q, kp_ref[0, :, hs], _NT, preferred_element_type=F32)
        s_new = lax.dot_general(q, kn_ref[0, :, hs], _NT, preferred_element_type=F32)
        s_new = jnp.where(vis_new, s_new, -jnp.inf)
        m = jnp.maximum(jnp.max(s_past, axis=-1, keepdims=True), jnp.max(s_new, axis=-1, keepdims=True))
        acc = (_dot(jnp.exp2(s_past - m).astype(BF16), vp_ref[0, :, hs])
               + _dot(jnp.exp2(s_new - m).astype(BF16), vn_ref[0, :, hs]))
        o_ref[0, :, h * MLA_DV:(h + 1) * MLA_DV] = (acc[:, :MLA_DV] / acc[:, MLA_DV:MLA_DV + 1]).astype(o_ref.dtype)


def _attention_cached(q, k_past, v_past, k_new, v_new):
    b, t, width = q.shape
    t_past = k_past.shape[1]

    def spec(rows):
        return pl.BlockSpec((1, rows, width), lambda bi: (bi, 0, 0))

    return pl.pallas_call(
        _attn_cached_kernel,
        grid=(b,),
        in_specs=[spec(t), spec(t_past), spec(t_past), spec(t), spec(t)],
        out_specs=pl.BlockSpec((1, t, MLA_HEADS * MLA_DV), lambda bi: (bi, 0, 0)),
        out_shape=jax.ShapeDtypeStruct((b, t, MLA_HEADS * MLA_DV), BF16),
        compiler_params=_params("parallel"),
        name="mla_attention_cached",
    )(q, k_past, v_past, k_new, v_new)


OUT_TM = 512
OUT_SUB = 256


def _out_proj_kernel(x_ref, or_ref, om_ref, og_ref, wrg_ref, wg3_ref, wro_ref, wmo_ref, wgo_ref, wout_ref,
                     rgn_ref, ggn_ref, g_ref, b_ref, o_ref, hr_ref, hg_ref):
    tm = x_ref.shape[0]
    sub = min(OUT_SUB, tm)
    for r in range(tm // sub):
        rs = slice(r * sub, (r + 1) * sub)
        x = x_ref[rs, :]
        xb = x.astype(BF16)

        ret_gate = _dot(xb, wrg_ref[...])
        for h in range(RET_HEADS):
            sl = slice(h * RET_DV, (h + 1) * RET_DV)
            gate = ret_gate[:, sl]
            o = or_ref[rs, sl]
            oc = o - jnp.mean(o, axis=-1, keepdims=True)
            normed = oc * lax.rsqrt(jnp.mean(oc * oc, axis=-1, keepdims=True) + EPS) * rgn_ref[:, sl]
            hr_ref[rs, sl] = (normed * (gate * jax.nn.sigmoid(gate))).astype(BF16)
        y_r = _dot(hr_ref[rs, :], wro_ref[...])

        y_m = _dot(om_ref[rs, :], wmo_ref[...])

        for h in range(GLA_HEADS):
            sl = slice(h * GLA_DV, (h + 1) * GLA_DV)
            gate = _dot(xb, wg3_ref[:, _G_GOG + h * GLA_DV:_G_GOG + (h + 1) * GLA_DV])
            o = og_ref[rs, sl]
            normed = o * lax.rsqrt(jnp.mean(o * o, axis=-1, keepdims=True) + EPS) * ggn_ref[:, sl]
            hg_ref[rs, sl] = (normed * (gate * jax.nn.sigmoid(gate))).astype(BF16)
        y_g = _dot(hg_ref[rs, :], wgo_ref[...])

        def branch_gate(idx):
            lo = _G_BR + idx * D_MODEL
            return jax.nn.sigmoid(_dot(xb, wg3_ref[:, lo:lo + D_MODEL]))

        mix = branch_gate(0) * y_r + branch_gate(1) * y_m + branch_gate(2) * y_g
        y = _dot(mix.astype(BF16), wout_ref[...])
        o_ref[rs, :] = _layer_norm(ALPHA * x + y, g_ref[...], b_ref[...])


def _out_proj_ln(x, o_r, o_m, o_g, lw, g, b):
    n = x.shape[0]
    tm = min(OUT_TM, n)
    weights = [lw["wrg"], lw["wg3"], lw["w_ret_o"], lw["w_mla_o"], lw["w_gla_o"], lw["w_out"],
               lw["ret_gn"], lw["gla_gn"], g, b]
    return pl.pallas_call(
        _out_proj_kernel,
        grid=(n // tm,),
        in_specs=[_row_spec(tm, D_MODEL), _row_spec(tm, RET_HEADS * RET_DV), _row_spec(tm, MLA_HEADS * MLA_DV),
                  _row_spec(tm, GLA_HEADS * GLA_DV)] + [_full_spec(w.shape) for w in weights],
        out_specs=_row_spec(tm, D_MODEL),
        out_shape=jax.ShapeDtypeStruct((n, D_MODEL), F32),
        scratch_shapes=[pltpu.VMEM((tm, RET_HEADS * RET_DV), BF16), pltpu.VMEM((tm, GLA_HEADS * GLA_DV), BF16)],
        compiler_params=_params("parallel"),
        name="out_proj_ln",
    )(x, o_r, o_m, o_g, *weights)


def _prep_layer(w, l):
    offs = np.cumsum((0,) + IN_SPLITS)
    w_in = w["w_in"][l]
    (r_q, r_k, r_v, r_g, m_cq, m_ckv, m_kr, g_q, g_k, g_v, g_lr, g_og, br) = [
        w_in[:, offs[i]:offs[i + 1]] for i in range(len(IN_SPLITS))]

    def pad_cols(a, n):
        return jnp.pad(a, ((0, 0), (0, n - a.shape[1])))

    w1 = jnp.concatenate([r_q, r_k, r_v, m_cq, pad_cols(m_kr, LANES), m_ckv, g_q, g_k, g_v,
                          pad_cols(g_lr, LANES)], axis=1).astype(BF16)
    dq = MLA_NOPE + MLA_ROPE
    wq = jnp.pad(w["mla_w_q_up"][l].reshape(MLA_Q_LORA, MLA_HEADS, dq),
                 ((0, 0), (0, 0), (0, MLA_HEAD_PAD - dq))).reshape(MLA_Q_LORA, MLA_HEADS * MLA_HEAD_PAD).astype(BF16)
    kv = w["mla_w_kv_up"][l].reshape(MLA_KV_LORA, MLA_HEADS, MLA_NOPE + MLA_DV)
    wk = jnp.pad(kv[:, :, :MLA_NOPE], ((0, 0), (0, 0), (0, MLA_HEAD_PAD - MLA_NOPE))).reshape(
        MLA_KV_LORA, MLA_HEADS * MLA_HEAD_PAD).astype(BF16)
    wv = jnp.pad(kv[:, :, MLA_NOPE:], ((0, 0), (0, 0), (0, MLA_HEAD_PAD - MLA_DV))).reshape(
        MLA_KV_LORA, MLA_HEADS * MLA_HEAD_PAD).astype(BF16)
    place = np.zeros((LANES, MLA_HEADS * MLA_HEAD_PAD), np.float32)
    v_ones = np.zeros((1, MLA_HEADS * MLA_HEAD_PAD), np.float32)
    for h in range(MLA_HEADS):
        place[np.arange(MLA_ROPE), h * MLA_HEAD_PAD + MLA_NOPE + np.arange(MLA_ROPE)] = 1.0
        v_ones[0, h * MLA_HEAD_PAD + MLA_DV] = 1.0
    return {
        "w1": w1, "wq": wq, "wk": wk, "wv": wv, "e": jnp.asarray(place, BF16), "v_ones": jnp.asarray(v_ones),
        "wgate": jnp.pad(w["gla_w_gate_up"][l], ((0, LANES - GLA_GATE_RANK), (0, 0))).astype(BF16),
        "bgate": w["gla_b_gate"][l][None, :],
        "qn": w["mla_q_norm_g"][l][None, :], "kvn": w["mla_kv_norm_g"][l][None, :],
        "wrg": r_g.astype(BF16), "wg3": w_in[:, offs[11]:offs[13]].astype(BF16),
        "w_ret_o": w["w_ret_o"][l].astype(BF16), "w_mla_o": w["w_mla_o"][l].astype(BF16),
        "w_gla_o": w["w_gla_o"][l].astype(BF16), "w_out": w["w_out"][l].astype(BF16),
        "ret_gn": w["ret_gn_g"][l][None, :], "gla_gn": w["gla_gn_g"][l][None, :],
        "f1u": w["ffn1_up"][l].astype(BF16), "f1d": w["ffn1_down"][l].astype(BF16),
        "f2u": w["ffn2_up"][l].astype(BF16), "f2d": w["ffn2_down"][l].astype(BF16),
        "ln_g": w["ln_g"][l], "ln_b": w["ln_b"][l],
    }


def _rope_tables(pos, tm):
    def cos_sin(half):
        inv = ROPE_THETA ** (-jnp.arange(half, dtype=F32) / half)
        ang = pos.astype(F32)[:, None] * inv[None, :]
        return jnp.cos(ang), jnp.sin(ang)

    t = pos.shape[0]
    c32, s32 = cos_sin(RET_DK // 2)
    c16, s16 = cos_sin(MLA_ROPE // 2)
    cr_h = jnp.tile(jnp.concatenate([c32, c32], axis=1), (1, RET_HEADS))
    sr_h = jnp.tile(jnp.concatenate([-s32, s32], axis=1), (1, RET_HEADS))
    k_scale = RET_DK ** -0.5
    q_scale = (MLA_NOPE + MLA_ROPE) ** -0.5 * float(np.log2(np.e))
    zeros = lambda n: jnp.zeros((t, n), F32)
    tabs = {
        "cr": jnp.concatenate([cr_h, cr_h * k_scale], axis=1),
        "sr": jnp.concatenate([sr_h, sr_h * k_scale], axis=1),
        "cq": jnp.concatenate([jnp.ones((t, MLA_NOPE), F32), c16, c16, zeros(LANES - MLA_NOPE - MLA_ROPE)], axis=1) * q_scale,
        "sq": jnp.concatenate([zeros(MLA_NOPE), -s16, s16, zeros(LANES - MLA_NOPE - MLA_ROPE)], axis=1) * q_scale,
        "ck": jnp.concatenate([c16, c16, zeros(LANES - MLA_ROPE)], axis=1),
        "sk": jnp.concatenate([-s16, s16, zeros(LANES - MLA_ROPE)], axis=1),
    }
    if t < tm:
        tabs = {k: jnp.tile(v, (tm // t, 1)) for k, v in tabs.items()}
    return tabs


def _group_layer(x, b, t, lw, tabs, past, layer, carried):
    n = b * t
    x = _ffn_ln(x, lw["f1u"], lw["f1d"], lw["ln_g"][0:1], lw["ln_b"][0:1])
    rqk, rv, qm, ckv_all, krp, kr_all, gqk, gv, la = _in_proj(x, lw, tabs, layer, carried)

    log_gamma = jnp.log(1.0 - 2.0 ** (-5.0 - jnp.arange(RET_HEADS, dtype=F32)))
    ret_la = jnp.repeat(log_gamma, RET_DK)[None, :]
    s_ret0 = None if past is None else jnp.swapaxes(past[2], -1, -2)
    s_gla0 = None if past is None else jnp.swapaxes(past[3], -1, -2)
    o_r, s_retT = _scan(rqk.reshape(b, t, -1), rv.reshape(b, t, -1), ret_la, s_ret0,
                        heads=RET_HEADS, dk=RET_DK, dv=RET_DV)
    o_g, s_glaT = _scan(gqk.reshape(b, t, -1), gv.reshape(b, t, -1), la.reshape(b, t, -1), s_gla0,
                        heads=GLA_HEADS, dk=GLA_DK, dv=GLA_DV)

    k_m, v_m = _kv_up(ckv_all.reshape(DEPTH * n, -1), krp, lw, n_rows=n, ckv_row0=layer * n)
    qm, k_m, v_m = qm.reshape(b, t, -1), k_m.reshape(b, t, -1), v_m.reshape(b, t, -1)
    if past is None:
        o_m = _attention(qm, k_m, v_m, tq=min(ATT_TQ, t), tk=min(ATT_TK, t))
    else:
        cache_ckv, cache_kr = past[0], past[1]
        t_past = cache_ckv.shape[2]
        n_past = b * t_past
        k_p, v_p = _kv_up(cache_ckv.reshape(-1, MLA_KV_LORA), cache_kr.reshape(-1, MLA_ROPE), lw,
                          n_rows=n_past, ckv_row0=layer * n_past, kr_row0=layer * n_past)
        o_m = _attention_cached(qm, k_p.reshape(b, t_past, -1), v_p.reshape(b, t_past, -1), k_m, v_m)

    x = _out_proj_ln(x, o_r.reshape(n, -1), o_m.reshape(n, -1), o_g.reshape(n, -1), lw,
                     lw["ln_g"][1:2], lw["ln_b"][1:2])
    x = _ffn_ln(x, lw["f2u"], lw["f2d"], lw["ln_g"][2:3], lw["ln_b"][2:3])
    return x, (ckv_all, kr_all), (jnp.swapaxes(s_retT, -1, -2), jnp.swapaxes(s_glaT, -1, -2))


def kernel(x_prompt, x_sample, cache_mla_ckv, cache_mla_krope, state_ret, state_gla, w_in, ret_gn_g, mla_q_norm_g, mla_w_q_up, mla_kv_norm_g, mla_w_kv_up, gla_w_gate_up, gla_b_gate, gla_gn_g, w_ret_o, w_mla_o, w_gla_o, w_out, ffn1_up, ffn1_down, ffn2_up, ffn2_down, ln_g, ln_b):
    w = dict(w_in=w_in, ret_gn_g=ret_gn_g, mla_q_norm_g=mla_q_norm_g, mla_w_q_up=mla_w_q_up,
             mla_kv_norm_g=mla_kv_norm_g, mla_w_kv_up=mla_w_kv_up, gla_w_gate_up=gla_w_gate_up,
             gla_b_gate=gla_b_gate, gla_gn_g=gla_gn_g, w_ret_o=w_ret_o, w_mla_o=w_mla_o, w_gla_o=w_gla_o,
             w_out=w_out, ffn1_up=ffn1_up, ffn1_down=ffn1_down, ffn2_up=ffn2_up, ffn2_down=ffn2_down,
             ln_g=ln_g, ln_b=ln_b)
    bp, tp, _ = x_prompt.shape
    bs, ts, _ = x_sample.shape
    t_past = cache_mla_ckv.shape[2]
    assert t_past % CHUNK == 0 and tp % CHUNK == 0 and ts % CHUNK == 0
    tabs_p = _rope_tables(jnp.arange(tp), min(INP_TM, bp * tp))
    tabs_s = _rope_tables(t_past + jnp.arange(ts), min(INP_TM, bs * ts))
    xp = x_prompt.reshape(bp * tp, D_MODEL)
    xs = x_sample.reshape(bs * ts, D_MODEL)
    carried_p = carried_s = None
    st_p, st_s = [], []
    for l in range(DEPTH):
        lw = _prep_layer(w, l)
        xp, carried_p, st = _group_layer(xp, bp, tp, lw, tabs_p, None, l, carried_p)
        st_p.append(st)
        past = (cache_mla_ckv, cache_mla_krope, state_ret[l], state_gla[l])
        xs, carried_s, st = _group_layer(xs, bs, ts, lw, tabs_s, past, l, carried_s)
        st_s.append(st)
    stack = lambda sts, i: jnp.stack([s[i] for s in sts])
    return (xp.reshape(bp, tp, D_MODEL), xs.reshape(bs, ts, D_MODEL),
            carried_p[0].reshape(DEPTH, bp, tp, -1), carried_p[1].reshape(DEPTH, bp, tp, -1),
            stack(st_p, 0), stack(st_p, 1),
            carried_s[0].reshape(DEPTH, bs, ts, -1), carried_s[1].reshape(DEPTH, bs, ts, -1),
            stack(st_s, 0), stack(st_s, 1))
```

```python
import functools

import numpy as np
import jax
import jax.numpy as jnp
from jax import lax
from jax.experimental import pallas as pl
from jax.experimental.pallas import tpu as pltpu

F32 = jnp.float32
BF16 = jnp.bfloat16

D_MODEL = 1024
DEPTH = 2
CHUNK = 64
CHUNK_SHIFT = 6
ALPHA = (2 * DEPTH) ** 0.25
EPS = 1e-5
ROPE_THETA = 10000.0
RET_HEADS, RET_DK, RET_DV = 4, 64, 128
MLA_HEADS, MLA_Q_LORA, MLA_KV_LORA, MLA_NOPE, MLA_ROPE, MLA_DV = 8, 384, 256, 64, 32, 64
GLA_HEADS, GLA_DK, GLA_DV, GLA_GATE_RANK, GLA_TAU = 4, 128, 256, 16, 16.0
D_FF = 2816
N_BRANCH = 3
IN_SPLITS = (RET_HEADS * RET_DK, RET_HEADS * RET_DK, RET_HEADS * RET_DV, RET_HEADS * RET_DV,
             MLA_Q_LORA, MLA_KV_LORA, MLA_ROPE,
             GLA_HEADS * GLA_DK, GLA_HEADS * GLA_DK, GLA_HEADS * GLA_DV, GLA_GATE_RANK, GLA_HEADS * GLA_DV,
             N_BRANCH * D_MODEL)

LANES = 128
MLA_HEAD_PAD = LANES
VMEM_LIMIT = 56 * 1024 * 1024

_C_RQK, _C_RV, _C_CQ, _C_KR, _C_CKV, _C_GQK, _C_GV, _C_LR, _C_END = (
    0, 512, 1024, 1408, 1536, 1792, 2816, 3840, 3968)
_G_GOG, _G_BR = 0, 1024

_NT = (((1,), (1,)), ((), ()))
_TN = (((0,), (0,)), ((), ()))


def _params(*sem):
    return pltpu.CompilerParams(dimension_semantics=sem, vmem_limit_bytes=VMEM_LIMIT)


def _dot(a, b):
    return jnp.dot(a, b, preferred_element_type=F32)


def _layer_norm(z, g, b):
    mu = jnp.mean(z, axis=-1, keepdims=True)
    zc = z - mu
    var = jnp.mean(zc * zc, axis=-1, keepdims=True)
    return zc * lax.rsqrt(var + EPS) * g + b


def _row_spec(tm, cols):
    return pl.BlockSpec((tm, cols), lambda i: (i, 0))


def _full_spec(shape):
    return pl.BlockSpec(shape, lambda *_: (0,) * len(shape), pipeline_mode=pl.Buffered(1))


FFN_TM = 512
FFN_FC = 256


def _ffn_ln_kernel(x_ref, wup_ref, wd_ref, g_ref, b_ref, o_ref, h_ref):
    x = x_ref[...]
    xb = x.astype(BF16)
    for c in range(D_FF // FFN_FC):
        sl = slice(c * FFN_FC, (c + 1) * FFN_FC)
        gate = _dot(xb, wup_ref[:, sl])
        up = _dot(xb, wup_ref[:, D_FF + c * FFN_FC:D_FF + (c + 1) * FFN_FC])
        h_ref[:, sl] = (gate * jax.nn.sigmoid(gate) * up).astype(BF16)
    y = _dot(h_ref[...], wd_ref[...])
    o_ref[...] = _layer_norm(ALPHA * x + 0.5 * y, g_ref[...], b_ref[...])


def _ffn_ln(x, wup, wd, g, b):
    n = x.shape[0]
    tm = min(FFN_TM, n)
    return pl.pallas_call(
        _ffn_ln_kernel,
        grid=(n // tm,),
        in_specs=[_row_spec(tm, D_MODEL), _full_spec(wup.shape), _full_spec(wd.shape),
                  _full_spec(g.shape), _full_spec(b.shape)],
        out_specs=_row_spec(tm, D_MODEL),
        out_shape=jax.ShapeDtypeStruct((n, D_MODEL), F32),
        scratch_shapes=[pltpu.VMEM((tm, D_FF), BF16)],
        compiler_params=_params("parallel"),
        name="ffn_ln",
    )(x, wup, wd, g, b)


INP_TM = 512


def _swap_halves(x, first_mask, half):
    return jnp.where(first_mask, pltpu.roll(x, LANES - half, 1), pltpu.roll(x, half, 1))


def _in_proj_kernel(x_ref, w_ref, wq_ref, wgate_ref, bgate_ref, qn_ref, kvn_ref,
                    cr_ref, sr_ref, cq_ref, sq_ref, ck_ref, sk_ref, *rest, n_kv_out):
    rqk_ref, rv_ref, qm_ref, ckv_ref, kr_ref, kr_out_ref, gqk_ref, gv_ref, la_ref = rest[-9 - n_kv_out:][:9]
    xb = x_ref[...].astype(BF16)
    tm = xb.shape[0]
    lane = lax.broadcasted_iota(jnp.int32, (tm, LANES), 1)

    ret_first = (lane & (RET_DK - 1)) < RET_DK // 2
    h_rqk = _dot(xb, w_ref[:, _C_RQK:_C_RV])
    for c in range(2 * RET_HEADS * RET_DK // LANES):
        sl = slice(c * LANES, (c + 1) * LANES)
        h = h_rqk[:, sl]
        rqk_ref[:, sl] = h * cr_ref[:, sl] + _swap_halves(h, ret_first, RET_DK // 2) * sr_ref[:, sl]
    rv_ref[...] = _dot(xb, w_ref[:, _C_RV:_C_CQ]).astype(BF16)

    h_cq_kr = _dot(xb, w_ref[:, _C_CQ:_C_CKV])
    hq = h_cq_kr[:, :MLA_Q_LORA]
    cq = hq * lax.rsqrt(jnp.mean(hq * hq, axis=-1, keepdims=True) + EPS) * qn_ref[...]
    q_first = lane < MLA_NOPE + MLA_ROPE // 2
    q_up = _dot(cq.astype(BF16), wq_ref[...])
    for h_i in range(MLA_HEADS):
        sl = slice(h_i * MLA_HEAD_PAD, (h_i + 1) * MLA_HEAD_PAD)
        qh = q_up[:, sl]
        qm_ref[:, sl] = (qh * cq_ref[...] + _swap_halves(qh, q_first, MLA_ROPE // 2) * sq_ref[...]).astype(BF16)

    hkr = h_cq_kr[:, MLA_Q_LORA:]
    kr = hkr * ck_ref[...] + _swap_halves(hkr, lane < MLA_ROPE // 2, MLA_ROPE // 2) * sk_ref[...]
    kr_ref[...] = kr
    kr_out_ref[...] = kr[:, :MLA_ROPE]
    hkv = _dot(xb, w_ref[:, _C_CKV:_C_GQK])
    ckv = hkv * lax.rsqrt(jnp.mean(hkv * hkv, axis=-1, keepdims=True) + EPS) * kvn_ref[...]
    ckv_ref[...] = ckv
    if n_kv_out:
        wk_ref, e_ref, wv_ref, ones_ref = rest[:4]
        k_ref, v_ref = rest[-2:]
        cb = ckv.astype(BF16)
        k_ref[...] = (_dot(cb, wk_ref[...]) + _dot(kr.astype(BF16), e_ref[...])).astype(BF16)
        v_ref[...] = (_dot(cb, wv_ref[...]) + ones_ref[...]).astype(BF16)

    ghd = GLA_HEADS * GLA_DK
    gla_q = _dot(xb, w_ref[:, _C_GQK:_C_GQK + ghd])
    gqk_ref[:, :ghd] = gla_q * (GLA_DK ** -0.5)
    gqk_ref[:, ghd:] = _dot(xb, w_ref[:, _C_GQK + ghd:_C_GV])
    gv_ref[...] = _dot(xb, w_ref[:, _C_GV:_C_LR]).astype(BF16)
    lr = _dot(xb, w_ref[:, _C_LR:_C_END]).astype(BF16)
    logit = _dot(lr, wgate_ref[...]) + bgate_ref[...]
    log_sig = jnp.minimum(logit, 0.0) - jnp.log1p(jnp.exp(-jnp.abs(logit)))
    la_ref[...] = log_sig / GLA_TAU


def _in_proj(x, lw, tabs, layer, carried, emit_kv):
    n = x.shape[0]
    tm = min(INP_TM, n)
    period = tabs["cr"].shape[0] // tm

    def tab_spec(cols):
        return pl.BlockSpec((tm, cols), lambda i: (i % period, 0))

    def layer_spec(cols):
        return pl.BlockSpec((None, tm, cols), lambda i: (layer, i, 0))

    rows = lambda cols, dtype: (_row_spec(tm, cols), jax.ShapeDtypeStruct((n, cols), dtype))
    stacked = lambda cols: (layer_spec(cols), jax.ShapeDtypeStruct((DEPTH, n, cols), F32))
    outs = [rows(2 * RET_HEADS * RET_DK, F32), rows(RET_HEADS * RET_DV, BF16), rows(MLA_HEADS * MLA_HEAD_PAD, BF16),
            stacked(MLA_KV_LORA), rows(LANES, F32), stacked(MLA_ROPE), rows(2 * GLA_HEADS * GLA_DK, F32),
            rows(GLA_HEADS * GLA_DV, BF16), rows(GLA_HEADS * GLA_DK, F32)]
    weights = [lw["w1"], lw["wq"], lw["wgate"], lw["bgate"], lw["qn"], lw["kvn"]]
    tables = [tabs["cr"], tabs["sr"], tabs["cq"], tabs["sq"], tabs["ck"], tabs["sk"]]
    in_specs = ([_row_spec(tm, D_MODEL)] + [_full_spec(w.shape) for w in weights]
                + [tab_spec(t.shape[1]) for t in tables])
    args = [x, *weights, *tables]
    if emit_kv:
        kv_weights = [lw["wk"], lw["e"], lw["wv"], lw["v_ones"]]
        in_specs += [_full_spec(w.shape) for w in kv_weights]
        args += kv_weights
        outs += [rows(MLA_HEADS * MLA_HEAD_PAD, BF16)] * 2
    aliases = {}
    if carried is not None:
        aliases = {len(args): 3, len(args) + 1: 5}
        in_specs += [pl.BlockSpec(memory_space=pl.ANY)] * 2
        args += list(carried)
    return pl.pallas_call(
        functools.partial(_in_proj_kernel, n_kv_out=2 if emit_kv else 0),
        grid=(n // tm,),
        in_specs=in_specs,
        out_specs=[o[0] for o in outs],
        out_shape=[o[1] for o in outs],
        input_output_aliases=aliases,
        compiler_params=_params("parallel"),
        name="in_proj",
    )(*args)


SCAN_CHUNKS_PER_STEP = 4
SCAN_STREAMS_PER_STEP = 4


def _cumsum_rows(a):
    rows = lax.broadcasted_iota(jnp.int32, a.shape, 0)
    s = 1
    while s < a.shape[0]:
        a = a + jnp.where(rows >= s, pltpu.roll(a, s, 0), 0.0)
        s *= 2
    return a


def _scan_kernel(*refs, heads, dk, dv, n_chunks, nb, has_la, has_s0):
    it = iter(refs)
    qk_ref, v_ref = next(it), next(it)
    la_ref = next(it)
    s0_ref = next(it) if has_s0 else None
    o_ref, sT_ref, st_ref = next(it), next(it), next(it)
    hd = heads * dk
    step = pl.program_id(1)

    @pl.when(step == 0)
    def _():
        if has_s0:
            st_ref[...] = s0_ref[...]
        else:
            st_ref[...] = jnp.zeros_like(st_ref)

    row = lax.broadcasted_iota(jnp.int32, (CHUNK, CHUNK), 0)
    col = lax.broadcasted_iota(jnp.int32, (CHUNK, CHUNK), 1)
    causal = row >= col
    ksl = lambda h: slice(h * dk, (h + 1) * dk)
    vsl = lambda h: slice(h * dv, (h + 1) * dv)
    chains = [(bi, h) for h in range(heads) for bi in range(nb)]

    def chunk(c, carry):
        rows = pl.ds(pl.multiple_of(c * CHUNK, CHUNK), CHUNK)
        qe, ke, kd, el = [], [], [], []
        for bi in range(nb):
            if has_la:
                bc = _cumsum_rows(la_ref[bi, rows, :])
            else:
                steps = lax.broadcasted_iota(jnp.int32, (CHUNK, hd), 0) + 1
                bc = steps.astype(F32) * la_ref[...]
            bl = bc[CHUNK - 1:CHUNK, :]
            q = qk_ref[bi, rows, :hd]
            k = qk_ref[bi, rows, hd:]
            qe.append((q * jnp.exp(bc)).astype(BF16))
            ke.append((k * jnp.exp(-bc)).astype(BF16))
            kd.append((k * jnp.exp(bl - bc)).astype(BF16))
            el.append(jnp.exp(bl))
        att = [lax.dot_general(qe[bi][:, ksl(h)], ke[bi][:, ksl(h)], _NT, preferred_element_type=F32)
               for bi, h in chains]
        cross = [lax.dot_general(qe[bi][:, ksl(h)], st_ref[bi, h].astype(BF16), _NT, preferred_element_type=F32)
                 for bi, h in chains]
        upd = [lax.dot_general(v_ref[bi, rows, vsl(h)], kd[bi][:, ksl(h)], _TN, preferred_element_type=F32)
               for bi, h in chains]
        for n, (bi, h) in enumerate(chains):
            a = jnp.where(causal, att[n], 0.0).astype(BF16)
            o_ref[bi, rows, vsl(h)] = _dot(a, v_ref[bi, rows, vsl(h)]) + cross[n]
        for n, (bi, h) in enumerate(chains):
            st_ref[bi, h] = st_ref[bi, h] * el[bi][:, ksl(h)] + upd[n]
        return carry

    lax.fori_loop(0, n_chunks, chunk, 0)

    @pl.when(step == pl.num_programs(1) - 1)
    def _():
        sT_ref[...] = st_ref[...]


def _scan(qk, v, la, s0T, *, heads, dk, dv):
    b, t, _ = qk.shape
    has_la = la.ndim == 3
    has_s0 = s0T is not None
    nb = min(SCAN_STREAMS_PER_STEP, b)
    ncs = min(SCAN_CHUNKS_PER_STEP, t // CHUNK)
    rows = ncs * CHUNK
    hd, hv = heads * dk, heads * dv

    def seq_spec(cols):
        return pl.BlockSpec((nb, rows, cols), lambda bi, si: (bi, si, 0))

    st_spec = pl.BlockSpec((nb, heads, dv, dk), lambda bi, si: (bi, 0, 0, 0))
    in_specs = [seq_spec(2 * hd), seq_spec(hv), seq_spec(hd) if has_la else _full_spec(la.shape)]
    args = [qk, v, la]
    if has_s0:
        in_specs.append(st_spec)
        args.append(s0T)
    kern = functools.partial(_scan_kernel, heads=heads, dk=dk, dv=dv, n_chunks=ncs, nb=nb,
                             has_la=has_la, has_s0=has_s0)
    return pl.pallas_call(
        kern,
        grid=(b // nb, t // rows),
        in_specs=in_specs,
        out_specs=[seq_spec(hv), st_spec],
        out_shape=[jax.ShapeDtypeStruct((b, t, hv), F32), jax.ShapeDtypeStruct((b, heads, dv, dk), F32)],
        scratch_shapes=[pltpu.VMEM((nb, heads, dv, dk), F32)],
        compiler_params=_params("parallel", "arbitrary"),
        name="scan_h%d_dk%d" % (heads, dk),
    )(*args)


ATT_TQ = 512
ATT_TK = 512
ATT_LOOKAHEAD = 2


def _attn_kernel(i_ref, j_ref, q_ref, k_ref, v_ref, o_ref, m_ref, acc_ref, *, tq, tk, nk):
    i, j = i_ref[pl.program_id(1)], j_ref[pl.program_id(1)]
    q_lo = (i * tq) // CHUNK
    q_hi = (i * tq + tq - 1) // CHUNK
    k_lo = (j * tk) // CHUNK
    k_hi = (j * tk + tk - 1) // CHUNK
    j_last = jnp.minimum(nk - 1, ((q_hi + 1) * CHUNK - 1) // tk)

    @pl.when(j == 0)
    def _():
        m_ref[...] = jnp.full_like(m_ref, -jnp.inf)
        acc_ref[...] = jnp.zeros_like(acc_ref)

    def tile(masked):
        if masked:
            qc = (i * tq + lax.broadcasted_iota(jnp.int32, (tq, tk), 0)) >> CHUNK_SHIFT
            kc = (j * tk + lax.broadcasted_iota(jnp.int32, (tq, tk), 1)) >> CHUNK_SHIFT
            vis = kc <= qc

        def scores(h):
            hs = slice(h * MLA_HEAD_PAD, (h + 1) * MLA_HEAD_PAD)
            return lax.dot_general(q_ref[0, :, hs], k_ref[0, :, hs], _NT, preferred_element_type=F32)

        pending = [scores(h) for h in range(ATT_LOOKAHEAD)]
        for h in range(MLA_HEADS):
            hs = slice(h * MLA_HEAD_PAD, (h + 1) * MLA_HEAD_PAD)
            s = pending.pop(0)
            if h + ATT_LOOKAHEAD < MLA_HEADS:
                pending.append(scores(h + ATT_LOOKAHEAD))
            if masked:
                s = jnp.where(vis, s, -jnp.inf)
            m_prev = m_ref[h]
            m_new = jnp.maximum(m_prev, jnp.max(s, axis=-1, keepdims=True))
            alpha = jnp.exp2(m_prev - m_new)
            p = jnp.concatenate([jnp.exp2(s[:, c * LANES:(c + 1) * LANES] - m_new).astype(BF16)
                                 for c in range(tk // LANES)], axis=1)
            acc_ref[h] = alpha * acc_ref[h] + _dot(p, v_ref[0, :, hs])
            m_ref[h] = m_new

    @pl.when(k_hi <= q_lo)
    def _():
        tile(False)

    @pl.when(jnp.logical_and(k_hi > q_lo, k_lo <= q_hi))
    def _():
        tile(True)

    @pl.when(j == j_last)
    def _():
        for h in range(MLA_HEADS):
            a = acc_ref[h]
            o_ref[0, :, h * MLA_DV:(h + 1) * MLA_DV] = (a[:, :MLA_DV] / a[:, MLA_DV:MLA_DV + 1]).astype(o_ref.dtype)


def _attention(q, k, v, *, tq, tk):
    b, t_q, _ = q.shape
    t_k = k.shape[1]
    nq, nk = t_q // tq, t_k // tk
    width = MLA_HEADS * MLA_HEAD_PAD
    pairs = [(i, j) for i in range(nq)
             for j in range(min(nk - 1, (((i * tq + tq - 1) // CHUNK + 1) * CHUNK - 1) // tk) + 1)]
    i_tab = jnp.asarray([p[0] for p in pairs], jnp.int32)
    j_tab = jnp.asarray([p[1] for p in pairs], jnp.int32)

    kern = functools.partial(_attn_kernel, tq=tq, tk=tk, nk=nk)
    grid_spec = pltpu.PrefetchScalarGridSpec(
        num_scalar_prefetch=2,
        grid=(b, len(pairs)),
        in_specs=[pl.BlockSpec((1, tq, width), lambda bi, p, it, jt: (bi, it[p], 0)),
                  pl.BlockSpec((1, tk, width), lambda bi, p, it, jt: (bi, jt[p], 0)),
                  pl.BlockSpec((1, tk, width), lambda bi, p, it, jt: (bi, jt[p], 0))],
        out_specs=pl.BlockSpec((1, tq, MLA_HEADS * MLA_DV), lambda bi, p, it, jt: (bi, it[p], 0)),
        scratch_shapes=[pltpu.VMEM((MLA_HEADS, tq, LANES), F32), pltpu.VMEM((MLA_HEADS, tq, MLA_HEAD_PAD), F32)],
    )
    return pl.pallas_call(
        kern,
        grid_spec=grid_spec,
        out_shape=jax.ShapeDtypeStruct((b, t_q, MLA_HEADS * MLA_DV), BF16),
        compiler_params=_params("parallel", "arbitrary"),
        name="mla_attention",
    )(i_tab, j_tab, q, k, v)


def _attn_cached_kernel(q_ref, cp_ref, krp_ref, cn_ref, krn_ref, wk_ref, e_ref, wv_ref, o_ref):
    t_new = q_ref.shape[1]
    hsl = lambda h: slice(h * MLA_HEAD_PAD, (h + 1) * MLA_HEAD_PAD)
    ckv_p = cp_ref[...].astype(BF16)
    ckv_n = cn_ref[...].astype(BF16)
    kr_p = _dot(krp_ref[...].astype(BF16), e_ref[:MLA_ROPE, :]).astype(BF16)
    kr_n = _dot(krn_ref[...].astype(BF16), e_ref[...]).astype(BF16)
    q_all = jnp.concatenate([q_ref[0, :, hsl(h)] for h in range(MLA_HEADS)], axis=0)
    q_lat = jnp.concatenate(
        [lax.dot_general(q_ref[0, :, hsl(h)], wk_ref[:, hsl(h)], _NT, preferred_element_type=F32)
         for h in range(MLA_HEADS)], axis=0).astype(BF16)
    s_past = (lax.dot_general(q_lat, ckv_p, _NT, preferred_element_type=F32)
              + lax.dot_general(q_all, kr_p, _NT, preferred_element_type=F32))
    s_new = (lax.dot_general(q_lat, ckv_n, _NT, preferred_element_type=F32)
             + lax.dot_general(q_all, kr_n, _NT, preferred_element_type=F32))
    q_chunk = jnp.concatenate([lax.broadcasted_iota(jnp.int32, (t_new, t_new), 0) >> CHUNK_SHIFT] * MLA_HEADS, axis=0)
    k_chunk = lax.broadcasted_iota(jnp.int32, (MLA_HEADS * t_new, t_new), 1) >> CHUNK_SHIFT
    s_new = jnp.where(k_chunk <= q_chunk, s_new, -jnp.inf)
    m = jnp.maximum(jnp.max(s_past, axis=-1, keepdims=True), jnp.max(s_new, axis=-1, keepdims=True))
    p_past = jnp.exp2(s_past - m)
    p_new = jnp.exp2(s_new - m)
    denom = jnp.sum(p_past, axis=-1, keepdims=True) + jnp.sum(p_new, axis=-1, keepdims=True)
    o_lat = ((_dot(p_past.astype(BF16), ckv_p) + _dot(p_new.astype(BF16), ckv_n)) / denom).astype(BF16)
    for h in range(MLA_HEADS):
        o = _dot(o_lat[h * t_new:(h + 1) * t_new], wv_ref[:, hsl(h)])
        o_ref[0, :, h * MLA_DV:(h + 1) * MLA_DV] = o[:, :MLA_DV].astype(o_ref.dtype)


def _attention_cached(q, cache_ckv, cache_kr, ckv_all, krp, lw, layer):
    b, t, width = q.shape
    t_past = cache_ckv.shape[2]
    weights = [lw["wk"], lw["e"][:, :MLA_HEAD_PAD], lw["wv"]]
    return pl.pallas_call(
        _attn_cached_kernel,
        grid=(b,),
        in_specs=[pl.BlockSpec((1, t, width), lambda bi: (bi, 0, 0)),
                  pl.BlockSpec((t_past, MLA_KV_LORA), lambda bi: (layer * b + bi, 0)),
                  pl.BlockSpec((t_past, MLA_ROPE), lambda bi: (layer * b + bi, 0)),
                  pl.BlockSpec((t, MLA_KV_LORA), lambda bi: (layer * b + bi, 0)),
                  pl.BlockSpec((t, LANES), lambda bi: (bi, 0))] + [_full_spec(w.shape) for w in weights],
        out_specs=pl.BlockSpec((1, t, MLA_HEADS * MLA_DV), lambda bi: (bi, 0, 0)),
        out_shape=jax.ShapeDtypeStruct((b, t, MLA_HEADS * MLA_DV), BF16),
        compiler_params=_params("parallel"),
        name="mla_attention_cached",
    )(q, cache_ckv.reshape(-1, MLA_KV_LORA), cache_kr.reshape(-1, MLA_ROPE), ckv_all.reshape(-1, MLA_KV_LORA),
      krp, *weights)


OUT_TM = 512
OUT_SUB = 256


def _out_proj_kernel(x_ref, or_ref, om_ref, og_ref, wrg_ref, wg3_ref, wro_ref, wmo_ref, wgo_ref, wout_ref,
                     rgn_ref, ggn_ref, g_ref, b_ref, o_ref, hr_ref, hg_ref):
    tm = x_ref.shape[0]
    sub = min(OUT_SUB, tm)
    for r in range(tm // sub):
        rs = slice(r * sub, (r + 1) * sub)
        x = x_ref[rs, :]
        xb = x.astype(BF16)

        ret_gate = _dot(xb, wrg_ref[...])
        for h in range(RET_HEADS):
            sl = slice(h * RET_DV, (h + 1) * RET_DV)
            gate = ret_gate[:, sl]
            o = or_ref[rs, sl]
            oc = o - jnp.mean(o, axis=-1, keepdims=True)
            normed = oc * lax.rsqrt(jnp.mean(oc * oc, axis=-1, keepdims=True) + EPS) * rgn_ref[:, sl]
            hr_ref[rs, sl] = (normed * (gate * jax.nn.sigmoid(gate))).astype(BF16)
        y_r = _dot(hr_ref[rs, :], wro_ref[...])

        y_m = _dot(om_ref[rs, :], wmo_ref[...])

        for h in range(GLA_HEADS):
            sl = slice(h * GLA_DV, (h + 1) * GLA_DV)
            gate = _dot(xb, wg3_ref[:, _G_GOG + h * GLA_DV:_G_GOG + (h + 1) * GLA_DV])
            o = og_ref[rs, sl]
            normed = o * lax.rsqrt(jnp.mean(o * o, axis=-1, keepdims=True) + EPS) * ggn_ref[:, sl]
            hg_ref[rs, sl] = (normed * (gate * jax.nn.sigmoid(gate))).astype(BF16)
        y_g = _dot(hg_ref[rs, :], wgo_ref[...])

        def branch_gate(idx):
            lo = _G_BR + idx * D_MODEL
            return jax.nn.sigmoid(_dot(xb, wg3_ref[:, lo:lo + D_MODEL]))

        mix = branch_gate(0) * y_r + branch_gate(1) * y_m + branch_gate(2) * y_g
        y = _dot(mix.astype(BF16), wout_ref[...])
        o_ref[rs, :] = _layer_norm(ALPHA * x + y, g_ref[...], b_ref[...])


def _out_proj_ln(x, o_r, o_m, o_g, lw, g, b):
    n = x.shape[0]
    tm = min(OUT_TM, n)
    weights = [lw["wrg"], lw["wg3"], lw["w_ret_o"], lw["w_mla_o"], lw["w_gla_o"], lw["w_out"],
               lw["ret_gn"], lw["gla_gn"], g, b]
    return pl.pallas_call(
        _out_proj_kernel,
        grid=(n // tm,),
        in_specs=[_row_spec(tm, D_MODEL), _row_spec(tm, RET_HEADS * RET_DV), _row_spec(tm, MLA_HEADS * MLA_DV),
                  _row_spec(tm, GLA_HEADS * GLA_DV)] + [_full_spec(w.shape) for w in weights],
        out_specs=_row_spec(tm, D_MODEL),
        out_shape=jax.ShapeDtypeStruct((n, D_MODEL), F32),
        scratch_shapes=[pltpu.VMEM((tm, RET_HEADS * RET_DV), BF16), pltpu.VMEM((tm, GLA_HEADS * GLA_DV), BF16)],
        compiler_params=_params("parallel"),
        name="out_proj_ln",
    )(x, o_r, o_m, o_g, *weights)


def _prep_layer(w, l):
    offs = np.cumsum((0,) + IN_SPLITS)
    w_in = w["w_in"][l]
    (r_q, r_k, r_v, r_g, m_cq, m_ckv, m_kr, g_q, g_k, g_v, g_lr, g_og, br) = [
        w_in[:, offs[i]:offs[i + 1]] for i in range(len(IN_SPLITS))]

    def pad_cols(a, n):
        return jnp.pad(a, ((0, 0), (0, n - a.shape[1])))

    w1 = jnp.concatenate([r_q, r_k, r_v, m_cq, pad_cols(m_kr, LANES), m_ckv, g_q, g_k, g_v,
                          pad_cols(g_lr, LANES)], axis=1).astype(BF16)
    dq = MLA_NOPE + MLA_ROPE
    wq = jnp.pad(w["mla_w_q_up"][l].reshape(MLA_Q_LORA, MLA_HEADS, dq),
                 ((0, 0), (0, 0), (0, MLA_HEAD_PAD - dq))).reshape(MLA_Q_LORA, MLA_HEADS * MLA_HEAD_PAD).astype(BF16)
    kv = w["mla_w_kv_up"][l].reshape(MLA_KV_LORA, MLA_HEADS, MLA_NOPE + MLA_DV)
    wk = jnp.pad(kv[:, :, :MLA_NOPE], ((0, 0), (0, 0), (0, MLA_HEAD_PAD - MLA_NOPE))).reshape(
        MLA_KV_LORA, MLA_HEADS * MLA_HEAD_PAD).astype(BF16)
    wv = jnp.pad(kv[:, :, MLA_NOPE:], ((0, 0), (0, 0), (0, MLA_HEAD_PAD - MLA_DV))).reshape(
        MLA_KV_LORA, MLA_HEADS * MLA_HEAD_PAD).astype(BF16)
    place = np.zeros((LANES, MLA_HEADS * MLA_HEAD_PAD), np.float32)
    v_ones = np.zeros((1, MLA_HEADS * MLA_HEAD_PAD), np.float32)
    for h in range(MLA_HEADS):
        place[np.arange(MLA_ROPE), h * MLA_HEAD_PAD + MLA_NOPE + np.arange(MLA_ROPE)] = 1.0
        v_ones[0, h * MLA_HEAD_PAD + MLA_DV] = 1.0
    return {
        "w1": w1, "wq": wq, "wk": wk, "wv": wv, "e": jnp.asarray(place, BF16), "v_ones": jnp.asarray(v_ones),
        "wgate": jnp.pad(w["gla_w_gate_up"][l], ((0, LANES - GLA_GATE_RANK), (0, 0))).astype(BF16),
        "bgate": w["gla_b_gate"][l][None, :],
        "qn": w["mla_q_norm_g"][l][None, :], "kvn": w["mla_kv_norm_g"][l][None, :],
        "wrg": r_g.astype(BF16), "wg3": w_in[:, offs[11]:offs[13]].astype(BF16),
        "w_ret_o": w["w_ret_o"][l].astype(BF16), "w_mla_o": w["w_mla_o"][l].astype(BF16),
        "w_gla_o": w["w_gla_o"][l].astype(BF16), "w_out": w["w_out"][l].astype(BF16),
        "ret_gn": w["ret_gn_g"][l][None, :], "gla_gn": w["gla_gn_g"][l][None, :],
        "f1u": w["ffn1_up"][l].astype(BF16), "f1d": w["ffn1_down"][l].astype(BF16),
        "f2u": w["ffn2_up"][l].astype(BF16), "f2d": w["ffn2_down"][l].astype(BF16),
        "ln_g": w["ln_g"][l], "ln_b": w["ln_b"][l],
    }


def _rope_tables(pos, tm):
    def cos_sin(half):
        inv = ROPE_THETA ** (-jnp.arange(half, dtype=F32) / half)
        ang = pos.astype(F32)[:, None] * inv[None, :]
        return jnp.cos(ang), jnp.sin(ang)

    t = pos.shape[0]
    c32, s32 = cos_sin(RET_DK // 2)
    c16, s16 = cos_sin(MLA_ROPE // 2)
    cr_h = jnp.tile(jnp.concatenate([c32, c32], axis=1), (1, RET_HEADS))
    sr_h = jnp.tile(jnp.concatenate([-s32, s32], axis=1), (1, RET_HEADS))
    k_scale = RET_DK ** -0.5
    q_scale = (MLA_NOPE + MLA_ROPE) ** -0.5 * float(np.log2(np.e))
    zeros = lambda n: jnp.zeros((t, n), F32)
    tabs = {
        "cr": jnp.concatenate([cr_h, cr_h * k_scale], axis=1),
        "sr": jnp.concatenate([sr_h, sr_h * k_scale], axis=1),
        "cq": jnp.concatenate([jnp.ones((t, MLA_NOPE), F32), c16, c16, zeros(LANES - MLA_NOPE - MLA_ROPE)], axis=1) * q_scale,
        "sq": jnp.concatenate([zeros(MLA_NOPE), -s16, s16, zeros(LANES - MLA_NOPE - MLA_ROPE)], axis=1) * q_scale,
        "ck": jnp.concatenate([c16, c16, zeros(LANES - MLA_ROPE)], axis=1),
        "sk": jnp.concatenate([-s16, s16, zeros(LANES - MLA_ROPE)], axis=1),
    }
    if t < tm:
        tabs = {k: jnp.tile(v, (tm // t, 1)) for k, v in tabs.items()}
    return tabs


def _group_layer(x, b, t, lw, tabs, past, layer, carried):
    n = b * t
    x = _ffn_ln(x, lw["f1u"], lw["f1d"], lw["ln_g"][0:1], lw["ln_b"][0:1])
    rqk, rv, qm, ckv_all, krp, kr_all, gqk, gv, la, *kv = _in_proj(x, lw, tabs, layer, carried, past is None)

    log_gamma = jnp.log(1.0 - 2.0 ** (-5.0 - jnp.arange(RET_HEADS, dtype=F32)))
    ret_la = jnp.repeat(log_gamma, RET_DK)[None, :]
    s_ret0 = None if past is None else jnp.swapaxes(past[2], -1, -2)
    s_gla0 = None if past is None else jnp.swapaxes(past[3], -1, -2)
    o_r, s_retT = _scan(rqk.reshape(b, t, -1), rv.reshape(b, t, -1), ret_la, s_ret0,
                        heads=RET_HEADS, dk=RET_DK, dv=RET_DV)
    o_g, s_glaT = _scan(gqk.reshape(b, t, -1), gv.reshape(b, t, -1), la.reshape(b, t, -1), s_gla0,
                        heads=GLA_HEADS, dk=GLA_DK, dv=GLA_DV)

    qm = qm.reshape(b, t, -1)
    if past is None:
        o_m = _attention(qm, kv[0].reshape(b, t, -1), kv[1].reshape(b, t, -1), tq=min(ATT_TQ, t), tk=min(ATT_TK, t))
    else:
        o_m = _attention_cached(qm, past[0], past[1], ckv_all, krp, lw, layer)

    x = _out_proj_ln(x, o_r.reshape(n, -1), o_m.reshape(n, -1), o_g.reshape(n, -1), lw,
                     lw["ln_g"][1:2], lw["ln_b"][1:2])
    x = _ffn_ln(x, lw["f2u"], lw["f2d"], lw["ln_g"][2:3], lw["ln_b"][2:3])
    return x, (ckv_all, kr_all), (jnp.swapaxes(s_retT, -1, -2), jnp.swapaxes(s_glaT, -1, -2))


def kernel(x_prompt, x_sample, cache_mla_ckv, cache_mla_krope, state_ret, state_gla, w_in, ret_gn_g, mla_q_norm_g, mla_w_q_up, mla_kv_norm_g, mla_w_kv_up, gla_w_gate_up, gla_b_gate, gla_gn_g, w_ret_o, w_mla_o, w_gla_o, w_out, ffn1_up, ffn1_down, ffn2_up, ffn2_down, ln_g, ln_b):
    w = dict(w_in=w_in, ret_gn_g=ret_gn_g, mla_q_norm_g=mla_q_norm_g, mla_w_q_up=mla_w_q_up,
             mla_kv_norm_g=mla_kv_norm_g, mla_w_kv_up=mla_w_kv_up, gla_w_gate_up=gla_w_gate_up,
             gla_b_gate=gla_b_gate, gla_gn_g=gla_gn_g, w_ret_o=w_ret_o, w_mla_o=w_mla_o, w_gla_o=w_gla_o,
             w_out=w_out, ffn1_up=ffn1_up, ffn1_down=ffn1_down, ffn2_up=ffn2_up, ffn2_down=ffn2_down,
             ln_g=ln_g, ln_b=ln_b)
    bp, tp, _ = x_prompt.shape
    bs, ts, _ = x_sample.shape
    t_past = cache_mla_ckv.shape[2]
    assert t_past % CHUNK == 0 and tp % CHUNK == 0 and ts % CHUNK == 0
    tabs_p = _rope_tables(jnp.arange(tp), min(INP_TM, bp * tp))
    tabs_s = _rope_tables(t_past + jnp.arange(ts), min(INP_TM, bs * ts))
    xp = x_prompt.reshape(bp * tp, D_MODEL)
    xs = x_sample.reshape(bs * ts, D_MODEL)
    carried_p = carried_s = None
    st_p, st_s = [], []
    for l in range(DEPTH):
        lw = _prep_layer(w, l)
        xp, carried_p, st = _group_layer(xp, bp, tp, lw, tabs_p, None, l, carried_p)
        st_p.append(st)
        past = (cache_mla_ckv, cache_mla_krope, state_ret[l], state_gla[l])
        xs, carried_s, st = _group_layer(xs, bs, ts, lw, tabs_s, past, l, carried_s)
        st_s.append(st)
    stack = lambda sts, i: jnp.stack([s[i] for s in sts])
    return (xp.reshape(bp, tp, D_MODEL), xs.reshape(bs, ts, D_MODEL),
            carried_p[0].reshape(DEPTH, bp, tp, -1), carried_p[1].reshape(DEPTH, bp, tp, -1),
            stack(st_p, 0), stack(st_p, 1),
            carried_s[0].reshape(DEPTH, bs, ts, -1), carried_s[1].reshape(DEPTH, bs, ts, -1),
            stack(st_s, 0), stack(st_s, 1))
```

```python
import functools

import numpy as np
import jax
import jax.numpy as jnp
from jax import lax
from jax.experimental import pallas as pl
from jax.experimental.pallas import tpu as pltpu

F32 = jnp.float32
BF16 = jnp.bfloat16

D_MODEL = 1024
DEPTH = 2
CHUNK = 64
CHUNK_SHIFT = 6
ALPHA = (2 * DEPTH) ** 0.25
EPS = 1e-5
ROPE_THETA = 10000.0
RET_HEADS, RET_DK, RET_DV = 4, 64, 128
MLA_HEADS, MLA_Q_LORA, MLA_KV_LORA, MLA_NOPE, MLA_ROPE, MLA_DV = 8, 384, 256, 64, 32, 64
GLA_HEADS, GLA_DK, GLA_DV, GLA_GATE_RANK, GLA_TAU = 4, 128, 256, 16, 16.0
D_FF = 2816
N_BRANCH = 3
IN_SPLITS = (RET_HEADS * RET_DK, RET_HEADS * RET_DK, RET_HEADS * RET_DV, RET_HEADS * RET_DV,
             MLA_Q_LORA, MLA_KV_LORA, MLA_ROPE,
             GLA_HEADS * GLA_DK, GLA_HEADS * GLA_DK, GLA_HEADS * GLA_DV, GLA_GATE_RANK, GLA_HEADS * GLA_DV,
             N_BRANCH * D_MODEL)

LANES = 128
MLA_HEAD_PAD = LANES
VMEM_LIMIT = 56 * 1024 * 1024

_C_RQK, _C_RV, _C_CQ, _C_KR, _C_CKV, _C_GQK, _C_GV, _C_LR, _C_END = (
    0, 512, 1024, 1408, 1536, 1792, 2816, 3840, 3968)
_G_GOG, _G_BR = 0, 1024

_NT = (((1,), (1,)), ((), ()))
_TN = (((0,), (0,)), ((), ()))


def _params(*sem):
    return pltpu.CompilerParams(dimension_semantics=sem, vmem_limit_bytes=VMEM_LIMIT)


def _dot(a, b):
    return jnp.dot(a, b, preferred_element_type=F32)


def _layer_norm(z, g, b):
    mu = jnp.mean(z, axis=-1, keepdims=True)
    zc = z - mu
    var = jnp.mean(zc * zc, axis=-1, keepdims=True)
    return zc * lax.rsqrt(var + EPS) * g + b


def _row_spec(tm, cols):
    return pl.BlockSpec((tm, cols), lambda i: (i, 0))


def _full_spec(shape):
    return pl.BlockSpec(shape, lambda *_: (0,) * len(shape), pipeline_mode=pl.Buffered(1))


FFN_TM = 512
FFN_FC = 256


def _ffn_ln_kernel(x_ref, wup_ref, wd_ref, g_ref, b_ref, o_ref, h_ref):
    x = x_ref[...]
    xb = x.astype(BF16)
    for c in range(D_FF // FFN_FC):
        sl = slice(c * FFN_FC, (c + 1) * FFN_FC)
        gate = _dot(xb, wup_ref[:, sl])
        up = _dot(xb, wup_ref[:, D_FF + c * FFN_FC:D_FF + (c + 1) * FFN_FC])
        h_ref[:, sl] = (gate * jax.nn.sigmoid(gate) * up).astype(BF16)
    y = _dot(h_ref[...], wd_ref[...])
    o_ref[...] = _layer_norm(ALPHA * x + 0.5 * y, g_ref[...], b_ref[...])


def _ffn_ln(x, wup, wd, g, b):
    n = x.shape[0]
    tm = min(FFN_TM, n)
    return pl.pallas_call(
        _ffn_ln_kernel,
        grid=(n // tm,),
        in_specs=[_row_spec(tm, D_MODEL), _full_spec(wup.shape), _full_spec(wd.shape),
                  _full_spec(g.shape), _full_spec(b.shape)],
        out_specs=_row_spec(tm, D_MODEL),
        out_shape=jax.ShapeDtypeStruct((n, D_MODEL), F32),
        scratch_shapes=[pltpu.VMEM((tm, D_FF), BF16)],
        compiler_params=_params("parallel"),
        name="ffn_ln",
    )(x, wup, wd, g, b)


INP_TM = 512


def _swap_halves(x, first_mask, half):
    return jnp.where(first_mask, pltpu.roll(x, LANES - half, 1), pltpu.roll(x, half, 1))


def _in_proj_kernel(x_ref, w_ref, wq_ref, wgate_ref, bgate_ref, qn_ref, kvn_ref,
                    cr_ref, sr_ref, cq_ref, sq_ref, ck_ref, sk_ref, *rest, n_kv_out):
    rqk_ref, rv_ref, qm_ref, ckv_ref, kr_ref, kr_out_ref, gqk_ref, gv_ref, la_ref = rest[-9 - n_kv_out:][:9]
    xb = x_ref[...].astype(BF16)
    tm = xb.shape[0]
    lane = lax.broadcasted_iota(jnp.int32, (tm, LANES), 1)
    ghd = GLA_HEADS * GLA_DK

    h_rqk = _dot(xb, w_ref[:, _C_RQK:_C_RV])
    h_cq_kr = _dot(xb, w_ref[:, _C_CQ:_C_CKV])

    ret_first = (lane & (RET_DK - 1)) < RET_DK // 2
    for c in range(2 * RET_HEADS * RET_DK // LANES):
        sl = slice(c * LANES, (c + 1) * LANES)
        h = h_rqk[:, sl]
        rqk_ref[:, sl] = h * cr_ref[:, sl] + _swap_halves(h, ret_first, RET_DK // 2) * sr_ref[:, sl]
    rv_ref[...] = _dot(xb, w_ref[:, _C_RV:_C_CQ]).astype(BF16)

    hq = h_cq_kr[:, :MLA_Q_LORA]
    cq = hq * lax.rsqrt(jnp.mean(hq * hq, axis=-1, keepdims=True) + EPS) * qn_ref[...]
    q_up = _dot(cq.astype(BF16), wq_ref[...])
    hkv = _dot(xb, w_ref[:, _C_CKV:_C_GQK])
    gv_ref[...] = _dot(xb, w_ref[:, _C_GV:_C_LR]).astype(BF16)
    q_first = lane < MLA_NOPE + MLA_ROPE // 2
    for h_i in range(MLA_HEADS):
        sl = slice(h_i * MLA_HEAD_PAD, (h_i + 1) * MLA_HEAD_PAD)
        qh = q_up[:, sl]
        qm_ref[:, sl] = (qh * cq_ref[...] + _swap_halves(qh, q_first, MLA_ROPE // 2) * sq_ref[...]).astype(BF16)

    hkr = h_cq_kr[:, MLA_Q_LORA:]
    kr = hkr * ck_ref[...] + _swap_halves(hkr, lane < MLA_ROPE // 2, MLA_ROPE // 2) * sk_ref[...]
    kr_ref[...] = kr
    kr_out_ref[...] = kr[:, :MLA_ROPE]
    ckv = hkv * lax.rsqrt(jnp.mean(hkv * hkv, axis=-1, keepdims=True) + EPS) * kvn_ref[...]
    ckv_ref[...] = ckv
    lr = _dot(xb, w_ref[:, _C_LR:_C_END]).astype(BF16)
    gla_q = _dot(xb, w_ref[:, _C_GQK:_C_GQK + ghd])
    if n_kv_out:
        wk_ref, e_ref, wv_ref, ones_ref = rest[:4]
        k_ref, v_ref = rest[-2:]
        cb = ckv.astype(BF16)
        k_ref[...] = (_dot(cb, wk_ref[...]) + _dot(kr.astype(BF16), e_ref[...])).astype(BF16)
        v_ref[...] = (_dot(cb, wv_ref[...]) + ones_ref[...]).astype(BF16)

    gqk_ref[:, :ghd] = gla_q * (GLA_DK ** -0.5)
    logit = _dot(lr, wgate_ref[...]) + bgate_ref[...]
    gqk_ref[:, ghd:] = _dot(xb, w_ref[:, _C_GQK + ghd:_C_GV])
    log_sig = jnp.minimum(logit, 0.0) - jnp.log1p(jnp.exp(-jnp.abs(logit)))
    la_ref[...] = log_sig / GLA_TAU


def _in_proj(x, lw, tabs, layer, carried, emit_kv):
    n = x.shape[0]
    tm = min(INP_TM, n)
    period = tabs["cr"].shape[0] // tm

    def tab_spec(cols):
        return pl.BlockSpec((tm, cols), lambda i: (i % period, 0))

    def layer_spec(cols):
        return pl.BlockSpec((None, tm, cols), lambda i: (layer, i, 0))

    rows = lambda cols, dtype: (_row_spec(tm, cols), jax.ShapeDtypeStruct((n, cols), dtype))
    stacked = lambda cols: (layer_spec(cols), jax.ShapeDtypeStruct((DEPTH, n, cols), F32))
    outs = [rows(2 * RET_HEADS * RET_DK, F32), rows(RET_HEADS * RET_DV, BF16), rows(MLA_HEADS * MLA_HEAD_PAD, BF16),
            stacked(MLA_KV_LORA), rows(LANES, F32), stacked(MLA_ROPE), rows(2 * GLA_HEADS * GLA_DK, F32),
            rows(GLA_HEADS * GLA_DV, BF16), rows(GLA_HEADS * GLA_DK, F32)]
    weights = [lw["w1"], lw["wq"], lw["wgate"], lw["bgate"], lw["qn"], lw["kvn"]]
    tables = [tabs["cr"], tabs["sr"], tabs["cq"], tabs["sq"], tabs["ck"], tabs["sk"]]
    in_specs = ([_row_spec(tm, D_MODEL)] + [_full_spec(w.shape) for w in weights]
                + [tab_spec(t.shape[1]) for t in tables])
    args = [x, *weights, *tables]
    if emit_kv:
        kv_weights = [lw["wk"], lw["e"], lw["wv"], lw["v_ones"]]
        in_specs += [_full_spec(w.shape) for w in kv_weights]
        args += kv_weights
        outs += [rows(MLA_HEADS * MLA_HEAD_PAD, BF16)] * 2
    aliases = {}
    if carried is not None:
        aliases = {len(args): 3, len(args) + 1: 5}
        in_specs += [pl.BlockSpec(memory_space=pl.ANY)] * 2
        args += list(carried)
    return pl.pallas_call(
        functools.partial(_in_proj_kernel, n_kv_out=2 if emit_kv else 0),
        grid=(n // tm,),
        in_specs=in_specs,
        out_specs=[o[0] for o in outs],
        out_shape=[o[1] for o in outs],
        input_output_aliases=aliases,
        compiler_params=_params("parallel"),
        name="in_proj",
    )(*args)


SCAN_CHUNKS_PER_STEP = 4
SCAN_STREAMS_PER_STEP = 4


def _cumsum_rows(a):
    rows = lax.broadcasted_iota(jnp.int32, a.shape, 0)
    s = 1
    while s < a.shape[0]:
        a = a + jnp.where(rows >= s, pltpu.roll(a, s, 0), 0.0)
        s *= 2
    return a


def _scan_kernel(*refs, heads, dk, dv, n_chunks, nb, has_la, has_s0):
    it = iter(refs)
    qk_ref, v_ref = next(it), next(it)
    la_ref = next(it)
    s0_ref = next(it) if has_s0 else None
    o_ref, sT_ref, st_ref = next(it), next(it), next(it)
    hd = heads * dk
    step = pl.program_id(1)

    @pl.when(step == 0)
    def _():
        if has_s0:
            st_ref[...] = s0_ref[...]
        else:
            st_ref[...] = jnp.zeros_like(st_ref)

    row = lax.broadcasted_iota(jnp.int32, (CHUNK, CHUNK), 0)
    col = lax.broadcasted_iota(jnp.int32, (CHUNK, CHUNK), 1)
    causal = row >= col
    ksl = lambda h: slice(h * dk, (h + 1) * dk)
    vsl = lambda h: slice(h * dv, (h + 1) * dv)
    chains = [(bi, h) for h in range(heads) for bi in range(nb)]

    def chunk(c, carry):
        rows = pl.ds(pl.multiple_of(c * CHUNK, CHUNK), CHUNK)
        qe, ke, kd, el = [], [], [], []
        for bi in range(nb):
            if has_la:
                bc = _cumsum_rows(la_ref[bi, rows, :])
            else:
                steps = lax.broadcasted_iota(jnp.int32, (CHUNK, hd), 0) + 1
                bc = steps.astype(F32) * la_ref[...]
            bl = bc[CHUNK - 1:CHUNK, :]
            q = qk_ref[bi, rows, :hd]
            k = qk_ref[bi, rows, hd:]
            qe.append((q * jnp.exp(bc)).astype(BF16))
            ke.append((k * jnp.exp(-bc)).astype(BF16))
            kd.append((k * jnp.exp(bl - bc)).astype(BF16))
            el.append(jnp.exp(bl))
        att = [lax.dot_general(qe[bi][:, ksl(h)], ke[bi][:, ksl(h)], _NT, preferred_element_type=F32)
               for bi, h in chains]
        cross = [lax.dot_general(qe[bi][:, ksl(h)], st_ref[bi, h].astype(BF16), _NT, preferred_element_type=F32)
                 for bi, h in chains]
        upd = [lax.dot_general(v_ref[bi, rows, vsl(h)], kd[bi][:, ksl(h)], _TN, preferred_element_type=F32)
               for bi, h in chains]
        for n, (bi, h) in enumerate(chains):
            a = jnp.where(causal, att[n], 0.0).astype(BF16)
            o_ref[bi, rows, vsl(h)] = _dot(a, v_ref[bi, rows, vsl(h)]) + cross[n]
        for n, (bi, h) in enumerate(chains):
            st_ref[bi, h] = st_ref[bi, h] * el[bi][:, ksl(h)] + upd[n]
        return carry

    lax.fori_loop(0, n_chunks, chunk, 0)

    @pl.when(step == pl.num_programs(1) - 1)
    def _():
        sT_ref[...] = st_ref[...]


def _scan(qk, v, la, s0T, *, heads, dk, dv):
    b, t, _ = qk.shape
    has_la = la.ndim == 3
    has_s0 = s0T is not None
    nb = min(SCAN_STREAMS_PER_STEP, b)
    ncs = min(SCAN_CHUNKS_PER_STEP, t // CHUNK)
    rows = ncs * CHUNK
    hd, hv = heads * dk, heads * dv

    def seq_spec(cols):
        return pl.BlockSpec((nb, rows, cols), lambda bi, si: (bi, si, 0))

    st_spec = pl.BlockSpec((nb, heads, dv, dk), lambda bi, si: (bi, 0, 0, 0))
    in_specs = [seq_spec(2 * hd), seq_spec(hv), seq_spec(hd) if has_la else _full_spec(la.shape)]
    args = [qk, v, la]
    if has_s0:
        in_specs.append(st_spec)
        args.append(s0T)
    kern = functools.partial(_scan_kernel, heads=heads, dk=dk, dv=dv, n_chunks=ncs, nb=nb,
                             has_la=has_la, has_s0=has_s0)
    return pl.pallas_call(
        kern,
        grid=(b // nb, t // rows),
        in_specs=in_specs,
        out_specs=[seq_spec(hv), st_spec],
        out_shape=[jax.ShapeDtypeStruct((b, t, hv), F32), jax.ShapeDtypeStruct((b, heads, dv, dk), F32)],
        scratch_shapes=[pltpu.VMEM((nb, heads, dv, dk), F32)],
        compiler_params=_params("parallel", "arbitrary"),
        name="scan_h%d_dk%d" % (heads, dk),
    )(*args)


ATT_TQ = 512
ATT_TK = 512
ATT_LOOKAHEAD = 2


def _attn_kernel(i_ref, j_ref, q_ref, k_ref, v_ref, o_ref, m_ref, acc_ref, *, tq, tk, nk):
    i, j = i_ref[pl.program_id(1)], j_ref[pl.program_id(1)]
    q_lo = (i * tq) // CHUNK
    q_hi = (i * tq + tq - 1) // CHUNK
    k_lo = (j * tk) // CHUNK
    k_hi = (j * tk + tk - 1) // CHUNK
    j_last = jnp.minimum(nk - 1, ((q_hi + 1) * CHUNK - 1) // tk)

    @pl.when(j == 0)
    def _():
        m_ref[...] = jnp.full_like(m_ref, -jnp.inf)
        acc_ref[...] = jnp.zeros_like(acc_ref)

    def tile(masked):
        if masked:
            qc = (i * tq + lax.broadcasted_iota(jnp.int32, (tq, tk), 0)) >> CHUNK_SHIFT
            kc = (j * tk + lax.broadcasted_iota(jnp.int32, (tq, tk), 1)) >> CHUNK_SHIFT
            vis = kc <= qc

        def scores(h):
            hs = slice(h * MLA_HEAD_PAD, (h + 1) * MLA_HEAD_PAD)
            return lax.dot_general(q_ref[0, :, hs], k_ref[0, :, hs], _NT, preferred_element_type=F32)

        pending = [scores(h) for h in range(ATT_LOOKAHEAD)]
        for h in range(MLA_HEADS):
            hs = slice(h * MLA_HEAD_PAD, (h + 1) * MLA_HEAD_PAD)
            s = pending.pop(0)
            if h + ATT_LOOKAHEAD < MLA_HEADS:
                pending.append(scores(h + ATT_LOOKAHEAD))
            if masked:
                s = jnp.where(vis, s, -jnp.inf)
            m_prev = m_ref[h]
            m_new = jnp.maximum(m_prev, jnp.max(s, axis=-1, keepdims=True))
            alpha = jnp.exp2(m_prev - m_new)
            p = jnp.concatenate([jnp.exp2(s[:, c * LANES:(c + 1) * LANES] - m_new).astype(BF16)
                                 for c in range(tk // LANES)], axis=1)
            acc_ref[h] = alpha * acc_ref[h] + _dot(p, v_ref[0, :, hs])
            m_ref[h] = m_new

    @pl.when(k_hi <= q_lo)
    def _():
        tile(False)

    @pl.when(jnp.logical_and(k_hi > q_lo, k_lo <= q_hi))
    def _():
        tile(True)

    @pl.when(j == j_last)
    def _():
        for h in range(MLA_HEADS):
            a = acc_ref[h]
            o_ref[0, :, h * MLA_DV:(h + 1) * MLA_DV] = (a[:, :MLA_DV] / a[:, MLA_DV:MLA_DV + 1]).astype(o_ref.dtype)


def _attention(q, k, v, *, tq, tk):
    b, t_q, _ = q.shape
    t_k = k.shape[1]
    nq, nk = t_q // tq, t_k // tk
    width = MLA_HEADS * MLA_HEAD_PAD
    pairs = [(i, j) for i in range(nq)
             for j in range(min(nk - 1, (((i * tq + tq - 1) // CHUNK + 1) * CHUNK - 1) // tk) + 1)]
    i_tab = jnp.asarray([p[0] for p in pairs], jnp.int32)
    j_tab = jnp.asarray([p[1] for p in pairs], jnp.int32)

    kern = functools.partial(_attn_kernel, tq=tq, tk=tk, nk=nk)
    grid_spec = pltpu.PrefetchScalarGridSpec(
        num_scalar_prefetch=2,
        grid=(b, len(pairs)),
        in_specs=[pl.BlockSpec((1, tq, width), lambda bi, p, it, jt: (bi, it[p], 0)),
                  pl.BlockSpec((1, tk, width), lambda bi, p, it, jt: (bi, jt[p], 0)),
                  pl.BlockSpec((1, tk, width), lambda bi, p, it, jt: (bi, jt[p], 0))],
        out_specs=pl.BlockSpec((1, tq, MLA_HEADS * MLA_DV), lambda bi, p, it, jt: (bi, it[p], 0)),
        scratch_shapes=[pltpu.VMEM((MLA_HEADS, tq, LANES), F32), pltpu.VMEM((MLA_HEADS, tq, MLA_HEAD_PAD), F32)],
    )
    return pl.pallas_call(
        kern,
        grid_spec=grid_spec,
        out_shape=jax.ShapeDtypeStruct((b, t_q, MLA_HEADS * MLA_DV), BF16),
        compiler_params=_params("parallel", "arbitrary"),
        name="mla_attention",
    )(i_tab, j_tab, q, k, v)


def _attn_cached_kernel(q_ref, cp_ref, krp_ref, cn_ref, krn_ref, wk_ref, e_ref, wv_ref, o_ref):
    t_new = q_ref.shape[1]
    hsl = lambda h: slice(h * MLA_HEAD_PAD, (h + 1) * MLA_HEAD_PAD)
    ckv_p = cp_ref[...].astype(BF16)
    ckv_n = cn_ref[...].astype(BF16)
    kr_p = _dot(krp_ref[...].astype(BF16), e_ref[:MLA_ROPE, :]).astype(BF16)
    kr_n = _dot(krn_ref[...].astype(BF16), e_ref[...]).astype(BF16)
    q_all = jnp.concatenate([q_ref[0, :, hsl(h)] for h in range(MLA_HEADS)], axis=0)
    q_lat = jnp.concatenate(
        [lax.dot_general(q_ref[0, :, hsl(h)], wk_ref[:, hsl(h)], _NT, preferred_element_type=F32)
         for h in range(MLA_HEADS)], axis=0).astype(BF16)
    s_past = (lax.dot_general(q_lat, ckv_p, _NT, preferred_element_type=F32)
              + lax.dot_general(q_all, kr_p, _NT, preferred_element_type=F32))
    s_new = (lax.dot_general(q_lat, ckv_n, _NT, preferred_element_type=F32)
             + lax.dot_general(q_all, kr_n, _NT, preferred_element_type=F32))
    q_chunk = jnp.concatenate([lax.broadcasted_iota(jnp.int32, (t_new, t_new), 0) >> CHUNK_SHIFT] * MLA_HEADS, axis=0)
    k_chunk = lax.broadcasted_iota(jnp.int32, (MLA_HEADS * t_new, t_new), 1) >> CHUNK_SHIFT
    s_new = jnp.where(k_chunk <= q_chunk, s_new, -jnp.inf)
    m = jnp.maximum(jnp.max(s_past, axis=-1, keepdims=True), jnp.max(s_new, axis=-1, keepdims=True))
    p_past = jnp.exp2(s_past - m)
    p_new = jnp.exp2(s_new - m)
    denom = jnp.sum(p_past, axis=-1, keepdims=True) + jnp.sum(p_new, axis=-1, keepdims=True)
    o_lat = ((_dot(p_past.astype(BF16), ckv_p) + _dot(p_new.astype(BF16), ckv_n)) / denom).astype(BF16)
    for h in range(MLA_HEADS):
        o = _dot(o_lat[h * t_new:(h + 1) * t_new], wv_ref[:, hsl(h)])
        o_ref[0, :, h * MLA_DV:(h + 1) * MLA_DV] = o[:, :MLA_DV].astype(o_ref.dtype)


def _attention_cached(q, cache_ckv, cache_kr, ckv_all, krp, lw, layer):
    b, t, width = q.shape
    t_past = cache_ckv.shape[2]
    weights = [lw["wk"], lw["e"][:, :MLA_HEAD_PAD], lw["wv"]]
    return pl.pallas_call(
        _attn_cached_kernel,
        grid=(b,),
        in_specs=[pl.BlockSpec((1, t, width), lambda bi: (bi, 0, 0)),
                  pl.BlockSpec((t_past, MLA_KV_LORA), lambda bi: (layer * b + bi, 0)),
                  pl.BlockSpec((t_past, MLA_ROPE), lambda bi: (layer * b + bi, 0)),
                  pl.BlockSpec((t, MLA_KV_LORA), lambda bi: (layer * b + bi, 0)),
                  pl.BlockSpec((t, LANES), lambda bi: (bi, 0))] + [_full_spec(w.shape) for w in weights],
        out_specs=pl.BlockSpec((1, t, MLA_HEADS * MLA_DV), lambda bi: (bi, 0, 0)),
        out_shape=jax.ShapeDtypeStruct((b, t, MLA_HEADS * MLA_DV), BF16),
        compiler_params=_params("parallel"),
        name="mla_attention_cached",
    )(q, cache_ckv.reshape(-1, MLA_KV_LORA), cache_kr.reshape(-1, MLA_ROPE), ckv_all.reshape(-1, MLA_KV_LORA),
      krp, *weights)


OUT_TM = 512
OUT_SUB = 256


def _out_proj_kernel(x_ref, or_ref, om_ref, og_ref, wrg_ref, wg3_ref, wro_ref, wmo_ref, wgo_ref, wout_ref,
                     rgn_ref, ggn_ref, g_ref, b_ref, o_ref, hr_ref, hg_ref):
    tm = x_ref.shape[0]
    sub = min(OUT_SUB, tm)
    for r in range(tm // sub):
        rs = slice(r * sub, (r + 1) * sub)
        x = x_ref[rs, :]
        xb = x.astype(BF16)

        ret_gate = _dot(xb, wrg_ref[...])
        gla_gate = [_dot(xb, wg3_ref[:, _G_GOG + h * GLA_DV:_G_GOG + (h + 1) * GLA_DV]) for h in range(GLA_HEADS)]

        for h in range(RET_HEADS):
            sl = slice(h * RET_DV, (h + 1) * RET_DV)
            gate = ret_gate[:, sl]
            o = or_ref[rs, sl]
            oc = o - jnp.mean(o, axis=-1, keepdims=True)
            normed = oc * lax.rsqrt(jnp.mean(oc * oc, axis=-1, keepdims=True) + EPS) * rgn_ref[:, sl]
            hr_ref[rs, sl] = (normed * (gate * jax.nn.sigmoid(gate))).astype(BF16)
        y_m = _dot(om_ref[rs, :], wmo_ref[...])
        gate_m = _dot(xb, wg3_ref[:, _G_BR + D_MODEL:_G_BR + 2 * D_MODEL])

        for h in range(GLA_HEADS):
            sl = slice(h * GLA_DV, (h + 1) * GLA_DV)
            gate = gla_gate[h]
            o = og_ref[rs, sl]
            normed = o * lax.rsqrt(jnp.mean(o * o, axis=-1, keepdims=True) + EPS) * ggn_ref[:, sl]
            hg_ref[rs, sl] = (normed * (gate * jax.nn.sigmoid(gate))).astype(BF16)
        y_r = _dot(hr_ref[rs, :], wro_ref[...])
        gate_r = _dot(xb, wg3_ref[:, _G_BR:_G_BR + D_MODEL])
        mix = jax.nn.sigmoid(gate_m) * y_m + jax.nn.sigmoid(gate_r) * y_r
        y_g = _dot(hg_ref[rs, :], wgo_ref[...])
        gate_g = _dot(xb, wg3_ref[:, _G_BR + 2 * D_MODEL:_G_BR + 3 * D_MODEL])
        mix = mix + jax.nn.sigmoid(gate_g) * y_g
        y = _dot(mix.astype(BF16), wout_ref[...])
        o_ref[rs, :] = _layer_norm(ALPHA * x + y, g_ref[...], b_ref[...])


def _out_proj_ln(x, o_r, o_m, o_g, lw, g, b):
    n = x.shape[0]
    tm = min(OUT_TM, n)
    weights = [lw["wrg"], lw["wg3"], lw["w_ret_o"], lw["w_mla_o"], lw["w_gla_o"], lw["w_out"],
               lw["ret_gn"], lw["gla_gn"], g, b]
    return pl.pallas_call(
        _out_proj_kernel,
        grid=(n // tm,),
        in_specs=[_row_spec(tm, D_MODEL), _row_spec(tm, RET_HEADS * RET_DV), _row_spec(tm, MLA_HEADS * MLA_DV),
                  _row_spec(tm, GLA_HEADS * GLA_DV)] + [_full_spec(w.shape) for w in weights],
        out_specs=_row_spec(tm, D_MODEL),
        out_shape=jax.ShapeDtypeStruct((n, D_MODEL), F32),
        scratch_shapes=[pltpu.VMEM((tm, RET_HEADS * RET_DV), BF16), pltpu.VMEM((tm, GLA_HEADS * GLA_DV), BF16)],
        compiler_params=_params("parallel"),
        name="out_proj_ln",
    )(x, o_r, o_m, o_g, *weights)


def _prep_layer(w, l):
    offs = np.cumsum((0,) + IN_SPLITS)
    w_in = w["w_in"][l]
    (r_q, r_k, r_v, r_g, m_cq, m_ckv, m_kr, g_q, g_k, g_v, g_lr, g_og, br) = [
        w_in[:, offs[i]:offs[i + 1]] for i in range(len(IN_SPLITS))]

    def pad_cols(a, n):
        return jnp.pad(a, ((0, 0), (0, n - a.shape[1])))

    w1 = jnp.concatenate([r_q, r_k, r_v, m_cq, pad_cols(m_kr, LANES), m_ckv, g_q, g_k, g_v,
                          pad_cols(g_lr, LANES)], axis=1).astype(BF16)
    dq = MLA_NOPE + MLA_ROPE
    wq = jnp.pad(w["mla_w_q_up"][l].reshape(MLA_Q_LORA, MLA_HEADS, dq),
                 ((0, 0), (0, 0), (0, MLA_HEAD_PAD - dq))).reshape(MLA_Q_LORA, MLA_HEADS * MLA_HEAD_PAD).astype(BF16)
    kv = w["mla_w_kv_up"][l].reshape(MLA_KV_LORA, MLA_HEADS, MLA_NOPE + MLA_DV)
    wk = jnp.pad(kv[:, :, :MLA_NOPE], ((0, 0), (0, 0), (0, MLA_HEAD_PAD - MLA_NOPE))).reshape(
        MLA_KV_LORA, MLA_HEADS * MLA_HEAD_PAD).astype(BF16)
    wv = jnp.pad(kv[:, :, MLA_NOPE:], ((0, 0), (0, 0), (0, MLA_HEAD_PAD - MLA_DV))).reshape(
        MLA_KV_LORA, MLA_HEADS * MLA_HEAD_PAD).astype(BF16)
    place = np.zeros((LANES, MLA_HEADS * MLA_HEAD_PAD), np.float32)
    v_ones = np.zeros((1, MLA_HEADS * MLA_HEAD_PAD), np.float32)
    for h in range(MLA_HEADS):
        place[np.arange(MLA_ROPE), h * MLA_HEAD_PAD + MLA_NOPE + np.arange(MLA_ROPE)] = 1.0
        v_ones[0, h * MLA_HEAD_PAD + MLA_DV] = 1.0
    return {
        "w1": w1, "wq": wq, "wk": wk, "wv": wv, "e": jnp.asarray(place, BF16), "v_ones": jnp.asarray(v_ones),
        "wgate": jnp.pad(w["gla_w_gate_up"][l], ((0, LANES - GLA_GATE_RANK), (0, 0))).astype(BF16),
        "bgate": w["gla_b_gate"][l][None, :],
        "qn": w["mla_q_norm_g"][l][None, :], "kvn": w["mla_kv_norm_g"][l][None, :],
        "wrg": r_g.astype(BF16), "wg3": w_in[:, offs[11]:offs[13]].astype(BF16),
        "w_ret_o": w["w_ret_o"][l].astype(BF16), "w_mla_o": w["w_mla_o"][l].astype(BF16),
        "w_gla_o": w["w_gla_o"][l].astype(BF16), "w_out": w["w_out"][l].astype(BF16),
        "ret_gn": w["ret_gn_g"][l][None, :], "gla_gn": w["gla_gn_g"][l][None, :],
        "f1u": w["ffn1_up"][l].astype(BF16), "f1d": w["ffn1_down"][l].astype(BF16),
        "f2u": w["ffn2_up"][l].astype(BF16), "f2d": w["ffn2_down"][l].astype(BF16),
        "ln_g": w["ln_g"][l], "ln_b": w["ln_b"][l],
    }


def _rope_tables(pos, tm):
    def cos_sin(half):
        inv = ROPE_THETA ** (-jnp.arange(half, dtype=F32) / half)
        ang = pos.astype(F32)[:, None] * inv[None, :]
        return jnp.cos(ang), jnp.sin(ang)

    t = pos.shape[0]
    c32, s32 = cos_sin(RET_DK // 2)
    c16, s16 = cos_sin(MLA_ROPE // 2)
    cr_h = jnp.tile(jnp.concatenate([c32, c32], axis=1), (1, RET_HEADS))
    sr_h = jnp.tile(jnp.concatenate([-s32, s32], axis=1), (1, RET_HEADS))
    k_scale = RET_DK ** -0.5
    q_scale = (MLA_NOPE + MLA_ROPE) ** -0.5 * float(np.log2(np.e))
    zeros = lambda n: jnp.zeros((t, n), F32)
    tabs = {
        "cr": jnp.concatenate([cr_h, cr_h * k_scale], axis=1),
        "sr": jnp.concatenate([sr_h, sr_h * k_scale], axis=1),
        "cq": jnp.concatenate([jnp.ones((t, MLA_NOPE), F32), c16, c16, zeros(LANES - MLA_NOPE - MLA_ROPE)], axis=1) * q_scale,
        "sq": jnp.concatenate([zeros(MLA_NOPE), -s16, s16, zeros(LANES - MLA_NOPE - MLA_ROPE)], axis=1) * q_scale,
        "ck": jnp.concatenate([c16, c16, zeros(LANES - MLA_ROPE)], axis=1),
        "sk": jnp.concatenate([-s16, s16, zeros(LANES - MLA_ROPE)], axis=1),
    }
    if t < tm:
        tabs = {k: jnp.tile(v, (tm // t, 1)) for k, v in tabs.items()}
    return tabs


def _group_layer(x, b, t, lw, tabs, past, layer, carried):
    n = b * t
    x = _ffn_ln(x, lw["f1u"], lw["f1d"], lw["ln_g"][0:1], lw["ln_b"][0:1])
    rqk, rv, qm, ckv_all, krp, kr_all, gqk, gv, la, *kv = _in_proj(x, lw, tabs, layer, carried, past is None)

    log_gamma = jnp.log(1.0 - 2.0 ** (-5.0 - jnp.arange(RET_HEADS, dtype=F32)))
    ret_la = jnp.repeat(log_gamma, RET_DK)[None, :]
    s_ret0 = None if past is None else jnp.swapaxes(past[2], -1, -2)
    s_gla0 = None if past is None else jnp.swapaxes(past[3], -1, -2)
    o_r, s_retT = _scan(rqk.reshape(b, t, -1), rv.reshape(b, t, -1), ret_la, s_ret0,
                        heads=RET_HEADS, dk=RET_DK, dv=RET_DV)
    o_g, s_glaT = _scan(gqk.reshape(b, t, -1), gv.reshape(b, t, -1), la.reshape(b, t, -1), s_gla0,
                        heads=GLA_HEADS, dk=GLA_DK, dv=GLA_DV)

    qm = qm.reshape(b, t, -1)
    if past is None:
        o_m = _attention(qm, kv[0].reshape(b, t, -1), kv[1].reshape(b, t, -1), tq=min(ATT_TQ, t), tk=min(ATT_TK, t))
    else:
        o_m = _attention_cached(qm, past[0], past[1], ckv_all, krp, lw, layer)

    x = _out_proj_ln(x, o_r.reshape(n, -1), o_m.reshape(n, -1), o_g.reshape(n, -1), lw,
                     lw["ln_g"][1:2], lw["ln_b"][1:2])
    x = _ffn_ln(x, lw["f2u"], lw["f2d"], lw["ln_g"][2:3], lw["ln_b"][2:3])
    return x, (ckv_all, kr_all), (jnp.swapaxes(s_retT, -1, -2), jnp.swapaxes(s_glaT, -1, -2))


def kernel(x_prompt, x_sample, cache_mla_ckv, cache_mla_krope, state_ret, state_gla, w_in, ret_gn_g, mla_q_norm_g, mla_w_q_up, mla_kv_norm_g, mla_w_kv_up, gla_w_gate_up, gla_b_gate, gla_gn_g, w_ret_o, w_mla_o, w_gla_o, w_out, ffn1_up, ffn1_down, ffn2_up, ffn2_down, ln_g, ln_b):
    w = dict(w_in=w_in, ret_gn_g=ret_gn_g, mla_q_norm_g=mla_q_norm_g, mla_w_q_up=mla_w_q_up,
             mla_kv_norm_g=mla_kv_norm_g, mla_w_kv_up=mla_w_kv_up, gla_w_gate_up=gla_w_gate_up,
             gla_b_gate=gla_b_gate, gla_gn_g=gla_gn_g, w_ret_o=w_ret_o, w_mla_o=w_mla_o, w_gla_o=w_gla_o,
             w_out=w_out, ffn1_up=ffn1_up, ffn1_down=ffn1_down, ffn2_up=ffn2_up, ffn2_down=ffn2_down,
             ln_g=ln_g, ln_b=ln_b)
    bp, tp, _ = x_prompt.shape
    bs, ts, _ = x_sample.shape
    t_past = cache_mla_ckv.shape[2]
    assert t_past % CHUNK == 0 and tp % CHUNK == 0 and ts % CHUNK == 0
    tabs_p = _rope_tables(jnp.arange(tp), min(INP_TM, bp * tp))
    tabs_s = _rope_tables(t_past + jnp.arange(ts), min(INP_TM, bs * ts))
    xp = x_prompt.reshape(bp * tp, D_MODEL)
    xs = x_sample.reshape(bs * ts, D_MODEL)
    carried_p = carried_s = None
    st_p, st_s = [], []
    for l in range(DEPTH):
        lw = _prep_layer(w, l)
        xp, carried_p, st = _group_layer(xp, bp, tp, lw, tabs_p, None, l, carried_p)
        st_p.append(st)
        past = (cache_mla_ckv, cache_mla_krope, state_ret[l], state_gla[l])
        xs, carried_s, st = _group_layer(xs, bs, ts, lw, tabs_s, past, l, carried_s)
        st_s.append(st)
    stack = lambda sts, i: jnp.stack([s[i] for s in sts])
    return (xp.reshape(bp, tp, D_MODEL), xs.reshape(bs, ts, D_MODEL),
            carried_p[0].reshape(DEPTH, bp, tp, -1), carried_p[1].reshape(DEPTH, bp, tp, -1),
            stack(st_p, 0), stack(st_p, 1),
            carried_s[0].reshape(DEPTH, bs, ts, -1), carried_s[1].reshape(DEPTH, bs, ts, -1),
            stack(st_s, 0), stack(st_s, 1))
```

```python
import functools

import numpy as np
import jax
import jax.numpy as jnp
from jax import lax
from jax.experimental import pallas as pl
from jax.experimental.pallas import tpu as pltpu

F32 = jnp.float32
BF16 = jnp.bfloat16

D_MODEL = 1024
DEPTH = 2
CHUNK = 64
CHUNK_SHIFT = 6
ALPHA = (2 * DEPTH) ** 0.25
EPS = 1e-5
ROPE_THETA = 10000.0
RET_HEADS, RET_DK, RET_DV = 4, 64, 128
MLA_HEADS, MLA_Q_LORA, MLA_KV_LORA, MLA_NOPE, MLA_ROPE, MLA_DV = 8, 384, 256, 64, 32, 64
GLA_HEADS, GLA_DK, GLA_DV, GLA_GATE_RANK, GLA_TAU = 4, 128, 256, 16, 16.0
D_FF = 2816
N_BRANCH = 3
IN_SPLITS = (RET_HEADS * RET_DK, RET_HEADS * RET_DK, RET_HEADS * RET_DV, RET_HEADS * RET_DV,
             MLA_Q_LORA, MLA_KV_LORA, MLA_ROPE,
             GLA_HEADS * GLA_DK, GLA_HEADS * GLA_DK, GLA_HEADS * GLA_DV, GLA_GATE_RANK, GLA_HEADS * GLA_DV,
             N_BRANCH * D_MODEL)

LANES = 128
MLA_HEAD_PAD = LANES
VMEM_LIMIT = 56 * 1024 * 1024

_C_RQK, _C_RV, _C_CQ, _C_KR, _C_CKV, _C_GQK, _C_GV, _C_LR, _C_END = (
    0, 512, 1024, 1408, 1536, 1792, 2816, 3840, 3968)
_G_GOG, _G_BR = 0, 1024

_NT = (((1,), (1,)), ((), ()))
_TN = (((0,), (0,)), ((), ()))


def _params(*sem):
    return pltpu.CompilerParams(dimension_semantics=sem, vmem_limit_bytes=VMEM_LIMIT)


def _dot(a, b):
    return jnp.dot(a, b, preferred_element_type=F32)


def _layer_norm(z, g, b):
    mu = jnp.mean(z, axis=-1, keepdims=True)
    zc = z - mu
    var = jnp.mean(zc * zc, axis=-1, keepdims=True)
    return zc * lax.rsqrt(var + EPS) * g + b


def _row_spec(tm, cols):
    return pl.BlockSpec((tm, cols), lambda i: (i, 0))


def _full_spec(shape):
    return pl.BlockSpec(shape, lambda *_: (0,) * len(shape), pipeline_mode=pl.Buffered(1))


FFN_TM = 512
FFN_FC = 256


def _ffn_ln_kernel(x_ref, wup_ref, wd_ref, g_ref, b_ref, o_ref, h_ref):
    x = x_ref[...]
    xb = x.astype(BF16)
    for c in range(D_FF // FFN_FC):
        sl = slice(c * FFN_FC, (c + 1) * FFN_FC)
        gate = _dot(xb, wup_ref[:, sl])
        up = _dot(xb, wup_ref[:, D_FF + c * FFN_FC:D_FF + (c + 1) * FFN_FC])
        h_ref[:, sl] = (gate * jax.nn.sigmoid(gate) * up).astype(BF16)
    y = _dot(h_ref[...], wd_ref[...])
    o_ref[...] = _layer_norm(ALPHA * x + 0.5 * y, g_ref[...], b_ref[...])


def _ffn_ln(x, wup, wd, g, b):
    n = x.shape[0]
    tm = min(FFN_TM, n)
    return pl.pallas_call(
        _ffn_ln_kernel,
        grid=(n // tm,),
        in_specs=[_row_spec(tm, D_MODEL), _full_spec(wup.shape), _full_spec(wd.shape),
                  _full_spec(g.shape), _full_spec(b.shape)],
        out_specs=_row_spec(tm, D_MODEL),
        out_shape=jax.ShapeDtypeStruct((n, D_MODEL), F32),
        scratch_shapes=[pltpu.VMEM((tm, D_FF), BF16)],
        compiler_params=_params("parallel"),
        name="ffn_ln",
    )(x, wup, wd, g, b)


INP_TM = 512


def _swap_halves(x, first_mask, half):
    return jnp.where(first_mask, pltpu.roll(x, LANES - half, 1), pltpu.roll(x, half, 1))


def _in_proj_kernel(x_ref, w_ref, wq_ref, wgate_ref, bgate_ref, qn_ref, kvn_ref,
                    cr_ref, sr_ref, cq_ref, sq_ref, ck_ref, sk_ref, *rest, n_kv_out):
    rqk_ref, rv_ref, qm_ref, ckv_ref, kr_ref, kr_out_ref, gqk_ref, gv_ref, la_ref = rest[-9 - n_kv_out:][:9]
    xb = x_ref[...].astype(BF16)
    tm = xb.shape[0]
    lane = lax.broadcasted_iota(jnp.int32, (tm, LANES), 1)
    ghd = GLA_HEADS * GLA_DK

    h_rqk = _dot(xb, w_ref[:, _C_RQK:_C_RV])
    h_cq_kr = _dot(xb, w_ref[:, _C_CQ:_C_CKV])

    ret_first = (lane & (RET_DK - 1)) < RET_DK // 2
    for c in range(2 * RET_HEADS * RET_DK // LANES):
        sl = slice(c * LANES, (c + 1) * LANES)
        h = h_rqk[:, sl]
        rqk_ref[:, sl] = h * cr_ref[:, sl] + _swap_halves(h, ret_first, RET_DK // 2) * sr_ref[:, sl]
    rv_ref[...] = _dot(xb, w_ref[:, _C_RV:_C_CQ]).astype(BF16)

    hq = h_cq_kr[:, :MLA_Q_LORA]
    cq = hq * lax.rsqrt(jnp.mean(hq * hq, axis=-1, keepdims=True) + EPS) * qn_ref[...]
    q_up = _dot(cq.astype(BF16), wq_ref[...])
    hkv = _dot(xb, w_ref[:, _C_CKV:_C_GQK])
    gv_ref[...] = _dot(xb, w_ref[:, _C_GV:_C_LR]).astype(BF16)
    q_first = lane < MLA_NOPE + MLA_ROPE // 2
    for h_i in range(MLA_HEADS):
        sl = slice(h_i * MLA_HEAD_PAD, (h_i + 1) * MLA_HEAD_PAD)
        qh = q_up[:, sl]
        qm_ref[:, sl] = (qh * cq_ref[...] + _swap_halves(qh, q_first, MLA_ROPE // 2) * sq_ref[...]).astype(BF16)

    hkr = h_cq_kr[:, MLA_Q_LORA:]
    kr = hkr * ck_ref[...] + _swap_halves(hkr, lane < MLA_ROPE // 2, MLA_ROPE // 2) * sk_ref[...]
    kr_ref[...] = kr
    kr_out_ref[...] = kr[:, :MLA_ROPE]
    ckv = hkv * lax.rsqrt(jnp.mean(hkv * hkv, axis=-1, keepdims=True) + EPS) * kvn_ref[...]
    ckv_ref[...] = ckv
    lr = _dot(xb, w_ref[:, _C_LR:_C_END]).astype(BF16)
    gla_q = _dot(xb, w_ref[:, _C_GQK:_C_GQK + ghd])
    if n_kv_out:
        wk_ref, e_ref, wv_ref, ones_ref = rest[:4]
        k_ref, v_ref = rest[-2:]
        cb = ckv.astype(BF16)
        k_ref[...] = (_dot(cb, wk_ref[...]) + _dot(kr.astype(BF16), e_ref[...])).astype(BF16)
        v_ref[...] = (_dot(cb, wv_ref[...]) + ones_ref[...]).astype(BF16)

    gqk_ref[:, :ghd] = gla_q * (GLA_DK ** -0.5)
    logit = _dot(lr, wgate_ref[...]) + bgate_ref[...]
    gqk_ref[:, ghd:] = _dot(xb, w_ref[:, _C_GQK + ghd:_C_GV])
    log_sig = jnp.minimum(logit, 0.0) - jnp.log1p(jnp.exp(-jnp.abs(logit)))
    la_ref[...] = log_sig / GLA_TAU


def _in_proj(x, lw, tabs, layer, carried, emit_kv):
    n = x.shape[0]
    tm = min(INP_TM, n)
    period = tabs["cr"].shape[0] // tm

    def tab_spec(cols):
        return pl.BlockSpec((tm, cols), lambda i: (i % period, 0))

    def layer_spec(cols):
        return pl.BlockSpec((None, tm, cols), lambda i: (layer, i, 0))

    rows = lambda cols, dtype: (_row_spec(tm, cols), jax.ShapeDtypeStruct((n, cols), dtype))
    stacked = lambda cols: (layer_spec(cols), jax.ShapeDtypeStruct((DEPTH, n, cols), F32))
    outs = [rows(2 * RET_HEADS * RET_DK, F32), rows(RET_HEADS * RET_DV, BF16), rows(MLA_HEADS * MLA_HEAD_PAD, BF16),
            stacked(MLA_KV_LORA), rows(LANES, F32), stacked(MLA_ROPE), rows(2 * GLA_HEADS * GLA_DK, F32),
            rows(GLA_HEADS * GLA_DV, BF16), rows(GLA_HEADS * GLA_DK, F32)]
    weights = [lw["w1"], lw["wq"], lw["wgate"], lw["bgate"], lw["qn"], lw["kvn"]]
    tables = [tabs["cr"], tabs["sr"], tabs["cq"], tabs["sq"], tabs["ck"], tabs["sk"]]
    in_specs = ([_row_spec(tm, D_MODEL)] + [_full_spec(w.shape) for w in weights]
                + [tab_spec(t.shape[1]) for t in tables])
    args = [x, *weights, *tables]
    if emit_kv:
        kv_weights = [lw["wk"], lw["e"], lw["wv"], lw["v_ones"]]
        in_specs += [_full_spec(w.shape) for w in kv_weights]
        args += kv_weights
        outs += [rows(MLA_HEADS * MLA_HEAD_PAD, BF16)] * 2
    aliases = {}
    if carried is not None:
        aliases = {len(args): 3, len(args) + 1: 5}
        in_specs += [pl.BlockSpec(memory_space=pl.ANY)] * 2
        args += list(carried)
    return pl.pallas_call(
        functools.partial(_in_proj_kernel, n_kv_out=2 if emit_kv else 0),
        grid=(n // tm,),
        in_specs=in_specs,
        out_specs=[o[0] for o in outs],
        out_shape=[o[1] for o in outs],
        input_output_aliases=aliases,
        compiler_params=_params("parallel"),
        name="in_proj",
    )(*args)


SCAN_CHUNKS_PER_STEP = 4
SCAN_STREAMS_PER_STEP = 4


def _cumsum_rows(a):
    rows = lax.broadcasted_iota(jnp.int32, a.shape, 0)
    s = 1
    while s < a.shape[0]:
        a = a + jnp.where(rows >= s, pltpu.roll(a, s, 0), 0.0)
        s *= 2
    return a


def _scan_kernel(*refs, heads, dk, dv, n_chunks, nb, has_la, has_s0):
    it = iter(refs)
    qk_ref, v_ref = next(it), next(it)
    la_ref = next(it)
    s0_ref = next(it) if has_s0 else None
    o_ref, sT_ref, st_ref = next(it), next(it), next(it)
    hd = heads * dk
    step = pl.program_id(1)

    @pl.when(step == 0)
    def _():
        if has_s0:
            st_ref[...] = s0_ref[...]
        else:
            st_ref[...] = jnp.zeros_like(st_ref)

    row = lax.broadcasted_iota(jnp.int32, (CHUNK, CHUNK), 0)
    col = lax.broadcasted_iota(jnp.int32, (CHUNK, CHUNK), 1)
    causal = row >= col
    ksl = lambda h: slice(h * dk, (h + 1) * dk)
    vsl = lambda h: slice(h * dv, (h + 1) * dv)
    chains = [(bi, h) for h in range(heads) for bi in range(nb)]

    def chunk(c, carry):
        rows = pl.ds(pl.multiple_of(c * CHUNK, CHUNK), CHUNK)
        qe, ke, kd, el = [], [], [], []
        for bi in range(nb):
            if has_la:
                bc = _cumsum_rows(la_ref[bi, rows, :])
            else:
                steps = lax.broadcasted_iota(jnp.int32, (CHUNK, hd), 0) + 1
                bc = steps.astype(F32) * la_ref[...]
            bl = bc[CHUNK - 1:CHUNK, :]
            q = qk_ref[bi, rows, :hd]
            k = qk_ref[bi, rows, hd:]
            qe.append((q * jnp.exp(bc)).astype(BF16))
            ke.append((k * jnp.exp(-bc)).astype(BF16))
            kd.append((k * jnp.exp(bl - bc)).astype(BF16))
            el.append(jnp.exp(bl))
        att = [lax.dot_general(qe[bi][:, ksl(h)], ke[bi][:, ksl(h)], _NT, preferred_element_type=F32)
               for bi, h in chains]
        cross = [lax.dot_general(qe[bi][:, ksl(h)], st_ref[bi, h].astype(BF16), _NT, preferred_element_type=F32)
                 for bi, h in chains]
        upd = [lax.dot_general(v_ref[bi, rows, vsl(h)], kd[bi][:, ksl(h)], _TN, preferred_element_type=F32)
               for bi, h in chains]
        for n, (bi, h) in enumerate(chains):
            a = jnp.where(causal, att[n], 0.0).astype(BF16)
            o_ref[bi, rows, vsl(h)] = _dot(a, v_ref[bi, rows, vsl(h)]) + cross[n]
        for n, (bi, h) in enumerate(chains):
            st_ref[bi, h] = st_ref[bi, h] * el[bi][:, ksl(h)] + upd[n]
        return carry

    lax.fori_loop(0, n_chunks, chunk, 0)

    @pl.when(step == pl.num_programs(1) - 1)
    def _():
        sT_ref[...] = st_ref[...]


def _scan(qk, v, la, s0T, *, heads, dk, dv):
    b, t, _ = qk.shape
    has_la = la.ndim == 3
    has_s0 = s0T is not None
    nb = min(SCAN_STREAMS_PER_STEP, b)
    ncs = min(SCAN_CHUNKS_PER_STEP, t // CHUNK)
    rows = ncs * CHUNK
    hd, hv = heads * dk, heads * dv

    def seq_spec(cols):
        return pl.BlockSpec((nb, rows, cols), lambda bi, si: (bi, si, 0))

    st_spec = pl.BlockSpec((nb, heads, dv, dk), lambda bi, si: (bi, 0, 0, 0))
    in_specs = [seq_spec(2 * hd), seq_spec(hv), seq_spec(hd) if has_la else _full_spec(la.shape)]
    args = [qk, v, la]
    if has_s0:
        in_specs.append(st_spec)
        args.append(s0T)
    kern = functools.partial(_scan_kernel, heads=heads, dk=dk, dv=dv, n_chunks=ncs, nb=nb,
                             has_la=has_la, has_s0=has_s0)
    return pl.pallas_call(
        kern,
        grid=(b // nb, t // rows),
        in_specs=in_specs,
        out_specs=[seq_spec(hv), st_spec],
        out_shape=[jax.ShapeDtypeStruct((b, t, hv), F32), jax.ShapeDtypeStruct((b, heads, dv, dk), F32)],
        scratch_shapes=[pltpu.VMEM((nb, heads, dv, dk), F32)],
        compiler_params=_params("parallel", "arbitrary"),
        name="scan_h%d_dk%d" % (heads, dk),
    )(*args)


ATT_TQ = 1024
ATT_TK = 512
ATT_SUB = 512
ATT_LOOKAHEAD = 2


def _attn_kernel(i_ref, j_ref, q_ref, k_ref, v_ref, o_ref, m_ref, acc_ref, *, tq, tk, nk):
    i, j = i_ref[pl.program_id(1)], j_ref[pl.program_id(1)]
    n_sub = tq // ATT_SUB
    q_lo = [(i * tq + r * ATT_SUB) // CHUNK for r in range(n_sub)]
    q_hi = [(i * tq + (r + 1) * ATT_SUB - 1) // CHUNK for r in range(n_sub)]
    k_lo = (j * tk) // CHUNK
    k_hi = (j * tk + tk - 1) // CHUNK
    j_last = jnp.minimum(nk - 1, ((q_hi[-1] + 1) * CHUNK - 1) // tk)

    @pl.when(j == 0)
    def _():
        m_ref[...] = jnp.full_like(m_ref, -jnp.inf)
        acc_ref[...] = jnp.zeros_like(acc_ref)

    def tile(modes):
        vis = {}
        for r, mode in enumerate(modes):
            if mode == "masked":
                row0 = i * tq + r * ATT_SUB
                qc = (row0 + lax.broadcasted_iota(jnp.int32, (ATT_SUB, tk), 0)) >> CHUNK_SHIFT
                kc = (j * tk + lax.broadcasted_iota(jnp.int32, (ATT_SUB, tk), 1)) >> CHUNK_SHIFT
                vis[r] = kc <= qc
        work = [(h, r) for h in range(MLA_HEADS) for r, mode in enumerate(modes) if mode != "skip"]

        def scores(h, r):
            hs = slice(h * MLA_HEAD_PAD, (h + 1) * MLA_HEAD_PAD)
            return lax.dot_general(q_ref[0, r * ATT_SUB:(r + 1) * ATT_SUB, hs], k_ref[0, :, hs], _NT,
                                   preferred_element_type=F32)

        pending = [scores(*w) for w in work[:ATT_LOOKAHEAD]]
        for n, (h, r) in enumerate(work):
            hs = slice(h * MLA_HEAD_PAD, (h + 1) * MLA_HEAD_PAD)
            rs = slice(r * ATT_SUB, (r + 1) * ATT_SUB)
            s = pending.pop(0)
            if n + ATT_LOOKAHEAD < len(work):
                pending.append(scores(*work[n + ATT_LOOKAHEAD]))
            if modes[r] == "masked":
                s = jnp.where(vis[r], s, -jnp.inf)
            m_prev = m_ref[h, rs, :]
            m_new = jnp.maximum(m_prev, jnp.max(s, axis=-1, keepdims=True))
            alpha = jnp.exp2(m_prev - m_new)
            p = jnp.concatenate([jnp.exp2(s[:, c * LANES:(c + 1) * LANES] - m_new).astype(BF16)
                                 for c in range(tk // LANES)], axis=1)
            acc_ref[h, rs, :] = alpha * acc_ref[h, rs, :] + _dot(p, v_ref[0, :, hs])
            m_ref[h, rs, :] = m_new

    def cond(r, mode):
        if mode == "full":
            return k_hi <= q_lo[r]
        if mode == "masked":
            return jnp.logical_and(k_hi > q_lo[r], k_lo <= q_hi[r])
        return k_lo > q_hi[r]

    combos = [("full",) * n_sub] + [("skip",) * r + ("masked",) + ("full",) * (n_sub - r - 1) for r in range(n_sub)]
    for modes in combos:
        pred = functools.reduce(jnp.logical_and, [cond(r, mode) for r, mode in enumerate(modes)])
        pl.when(pred)(functools.partial(tile, modes))

    @pl.when(j == j_last)
    def _():
        for h in range(MLA_HEADS):
            a = acc_ref[h]
            o_ref[0, :, h * MLA_DV:(h + 1) * MLA_DV] = (a[:, :MLA_DV] / a[:, MLA_DV:MLA_DV + 1]).astype(o_ref.dtype)


def _attention(q, k, v, *, tq, tk):
    b, t_q, _ = q.shape
    t_k = k.shape[1]
    assert tq % ATT_SUB == 0 and t_q % tq == 0 and t_k % tk == 0
    nq, nk = t_q // tq, t_k // tk
    width = MLA_HEADS * MLA_HEAD_PAD
    pairs = [(i, j) for i in range(nq)
             for j in range(min(nk - 1, (((i * tq + tq - 1) // CHUNK + 1) * CHUNK - 1) // tk) + 1)]
    i_tab = jnp.asarray([p[0] for p in pairs], jnp.int32)
    j_tab = jnp.asarray([p[1] for p in pairs], jnp.int32)

    kern = functools.partial(_attn_kernel, tq=tq, tk=tk, nk=nk)
    grid_spec = pltpu.PrefetchScalarGridSpec(
        num_scalar_prefetch=2,
        grid=(b, len(pairs)),
        in_specs=[pl.BlockSpec((1, tq, width), lambda bi, p, it, jt: (bi, it[p], 0)),
                  pl.BlockSpec((1, tk, width), lambda bi, p, it, jt: (bi, jt[p], 0)),
                  pl.BlockSpec((1, tk, width), lambda bi, p, it, jt: (bi, jt[p], 0))],
        out_specs=pl.BlockSpec((1, tq, MLA_HEADS * MLA_DV), lambda bi, p, it, jt: (bi, it[p], 0)),
        scratch_shapes=[pltpu.VMEM((MLA_HEADS, tq, LANES), F32), pltpu.VMEM((MLA_HEADS, tq, MLA_HEAD_PAD), F32)],
    )
    return pl.pallas_call(
        kern,
        grid_spec=grid_spec,
        out_shape=jax.ShapeDtypeStruct((b, t_q, MLA_HEADS * MLA_DV), BF16),
        compiler_params=_params("parallel", "arbitrary"),
        name="mla_attention",
    )(i_tab, j_tab, q, k, v)


def _attn_cached_kernel(q_ref, cp_ref, krp_ref, cn_ref, krn_ref, wk_ref, e_ref, wv_ref, o_ref):
    t_new = q_ref.shape[1]
    hsl = lambda h: slice(h * MLA_HEAD_PAD, (h + 1) * MLA_HEAD_PAD)
    ckv_p = cp_ref[...].astype(BF16)
    ckv_n = cn_ref[...].astype(BF16)
    kr_p = _dot(krp_ref[...].astype(BF16), e_ref[:MLA_ROPE, :]).astype(BF16)
    kr_n = _dot(krn_ref[...].astype(BF16), e_ref[...]).astype(BF16)
    q_all = jnp.concatenate([q_ref[0, :, hsl(h)] for h in range(MLA_HEADS)], axis=0)
    q_lat = jnp.concatenate(
        [lax.dot_general(q_ref[0, :, hsl(h)], wk_ref[:, hsl(h)], _NT, preferred_element_type=F32)
         for h in range(MLA_HEADS)], axis=0).astype(BF16)
    s_past = (lax.dot_general(q_lat, ckv_p, _NT, preferred_element_type=F32)
              + lax.dot_general(q_all, kr_p, _NT, preferred_element_type=F32))
    s_new = (lax.dot_general(q_lat, ckv_n, _NT, preferred_element_type=F32)
             + lax.dot_general(q_all, kr_n, _NT, preferred_element_type=F32))
    q_chunk = jnp.concatenate([lax.broadcasted_iota(jnp.int32, (t_new, t_new), 0) >> CHUNK_SHIFT] * MLA_HEADS, axis=0)
    k_chunk = lax.broadcasted_iota(jnp.int32, (MLA_HEADS * t_new, t_new), 1) >> CHUNK_SHIFT
    s_new = jnp.where(k_chunk <= q_chunk, s_new, -jnp.inf)
    m = jnp.maximum(jnp.max(s_past, axis=-1, keepdims=True), jnp.max(s_new, axis=-1, keepdims=True))
    p_past = jnp.exp2(s_past - m)
    p_new = jnp.exp2(s_new - m)
    denom = jnp.sum(p_past, axis=-1, keepdims=True) + jnp.sum(p_new, axis=-1, keepdims=True)
    o_lat = ((_dot(p_past.astype(BF16), ckv_p) + _dot(p_new.astype(BF16), ckv_n)) / denom).astype(BF16)
    for h in range(MLA_HEADS):
        o = _dot(o_lat[h * t_new:(h + 1) * t_new], wv_ref[:, hsl(h)])
        o_ref[0, :, h * MLA_DV:(h + 1) * MLA_DV] = o[:, :MLA_DV].astype(o_ref.dtype)


def _attention_cached(q, cache_ckv, cache_kr, ckv_all, krp, lw, layer):
    b, t, width = q.shape
    t_past = cache_ckv.shape[2]
    weights = [lw["wk"], lw["e"][:, :MLA_HEAD_PAD], lw["wv"]]
    return pl.pallas_call(
        _attn_cached_kernel,
        grid=(b,),
        in_specs=[pl.BlockSpec((1, t, width), lambda bi: (bi, 0, 0)),
                  pl.BlockSpec((t_past, MLA_KV_LORA), lambda bi: (layer * b + bi, 0)),
                  pl.BlockSpec((t_past, MLA_ROPE), lambda bi: (layer * b + bi, 0)),
                  pl.BlockSpec((t, MLA_KV_LORA), lambda bi: (layer * b + bi, 0)),
                  pl.BlockSpec((t, LANES), lambda bi: (bi, 0))] + [_full_spec(w.shape) for w in weights],
        out_specs=pl.BlockSpec((1, t, MLA_HEADS * MLA_DV), lambda bi: (bi, 0, 0)),
        out_shape=jax.ShapeDtypeStruct((b, t, MLA_HEADS * MLA_DV), BF16),
        compiler_params=_params("parallel"),
        name="mla_attention_cached",
    )(q, cache_ckv.reshape(-1, MLA_KV_LORA), cache_kr.reshape(-1, MLA_ROPE), ckv_all.reshape(-1, MLA_KV_LORA),
      krp, *weights)


OUT_TM = 512
OUT_SUB = 256


def _out_proj_kernel(x_ref, or_ref, om_ref, og_ref, wrg_ref, wg3_ref, wro_ref, wmo_ref, wgo_ref, wout_ref,
                     rgn_ref, ggn_ref, g_ref, b_ref, o_ref, hr_ref, hg_ref):
    tm = x_ref.shape[0]
    sub = min(OUT_SUB, tm)
    for r in range(tm // sub):
        rs = slice(r * sub, (r + 1) * sub)
        x = x_ref[rs, :]
        xb = x.astype(BF16)

        ret_gate = _dot(xb, wrg_ref[...])
        gla_gate = [_dot(xb, wg3_ref[:, _G_GOG + h * GLA_DV:_G_GOG + (h + 1) * GLA_DV]) for h in range(GLA_HEADS)]

        for h in range(RET_HEADS):
            sl = slice(h * RET_DV, (h + 1) * RET_DV)
            gate = ret_gate[:, sl]
            o = or_ref[rs, sl]
            oc = o - jnp.mean(o, axis=-1, keepdims=True)
            normed = oc * lax.rsqrt(jnp.mean(oc * oc, axis=-1, keepdims=True) + EPS) * rgn_ref[:, sl]
            hr_ref[rs, sl] = (normed * (gate * jax.nn.sigmoid(gate))).astype(BF16)
        y_m = _dot(om_ref[rs, :], wmo_ref[...])
        gate_m = _dot(xb, wg3_ref[:, _G_BR + D_MODEL:_G_BR + 2 * D_MODEL])

        for h in range(GLA_HEADS):
            sl = slice(h * GLA_DV, (h + 1) * GLA_DV)
            gate = gla_gate[h]
            o = og_ref[rs, sl]
            normed = o * lax.rsqrt(jnp.mean(o * o, axis=-1, keepdims=True) + EPS) * ggn_ref[:, sl]
            hg_ref[rs, sl] = (normed * (gate * jax.nn.sigmoid(gate))).astype(BF16)
        y_r = _dot(hr_ref[rs, :], wro_ref[...])
        gate_r = _dot(xb, wg3_ref[:, _G_BR:_G_BR + D_MODEL])
        mix = jax.nn.sigmoid(gate_m) * y_m + jax.nn.sigmoid(gate_r) * y_r
        y_g = _dot(hg_ref[rs, :], wgo_ref[...])
        gate_g = _dot(xb, wg3_ref[:, _G_BR + 2 * D_MODEL:_G_BR + 3 * D_MODEL])
        mix = mix + jax.nn.sigmoid(gate_g) * y_g
        y = _dot(mix.astype(BF16), wout_ref[...])
        o_ref[rs, :] = _layer_norm(ALPHA * x + y, g_ref[...], b_ref[...])


def _out_proj_ln(x, o_r, o_m, o_g, lw, g, b):
    n = x.shape[0]
    tm = min(OUT_TM, n)
    weights = [lw["wrg"], lw["wg3"], lw["w_ret_o"], lw["w_mla_o"], lw["w_gla_o"], lw["w_out"],
               lw["ret_gn"], lw["gla_gn"], g, b]
    return pl.pallas_call(
        _out_proj_kernel,
        grid=(n // tm,),
        in_specs=[_row_spec(tm, D_MODEL), _row_spec(tm, RET_HEADS * RET_DV), _row_spec(tm, MLA_HEADS * MLA_DV),
                  _row_spec(tm, GLA_HEADS * GLA_DV)] + [_full_spec(w.shape) for w in weights],
        out_specs=_row_spec(tm, D_MODEL),
        out_shape=jax.ShapeDtypeStruct((n, D_MODEL), F32),
        scratch_shapes=[pltpu.VMEM((tm, RET_HEADS * RET_DV), BF16), pltpu.VMEM((tm, GLA_HEADS * GLA_DV), BF16)],
        compiler_params=_params("parallel"),
        name="out_proj_ln",
    )(x, o_r, o_m, o_g, *weights)


def _prep_layer(w, l):
    offs = np.cumsum((0,) + IN_SPLITS)
    w_in = w["w_in"][l]
    (r_q, r_k, r_v, r_g, m_cq, m_ckv, m_kr, g_q, g_k, g_v, g_lr, g_og, br) = [
        w_in[:, offs[i]:offs[i + 1]] for i in range(len(IN_SPLITS))]

    def pad_cols(a, n):
        return jnp.pad(a, ((0, 0), (0, n - a.shape[1])))

    w1 = jnp.concatenate([r_q, r_k, r_v, m_cq, pad_cols(m_kr, LANES), m_ckv, g_q, g_k, g_v,
                          pad_cols(g_lr, LANES)], axis=1).astype(BF16)
    dq = MLA_NOPE + MLA_ROPE
    wq = jnp.pad(w["mla_w_q_up"][l].reshape(MLA_Q_LORA, MLA_HEADS, dq),
                 ((0, 0), (0, 0), (0, MLA_HEAD_PAD - dq))).reshape(MLA_Q_LORA, MLA_HEADS * MLA_HEAD_PAD).astype(BF16)
    kv = w["mla_w_kv_up"][l].reshape(MLA_KV_LORA, MLA_HEADS, MLA_NOPE + MLA_DV)
    wk = jnp.pad(kv[:, :, :MLA_NOPE], ((0, 0), (0, 0), (0, MLA_HEAD_PAD - MLA_NOPE))).reshape(
        MLA_KV_LORA, MLA_HEADS * MLA_HEAD_PAD).astype(BF16)
    wv = jnp.pad(kv[:, :, MLA_NOPE:], ((0, 0), (0, 0), (0, MLA_HEAD_PAD - MLA_DV))).reshape(
        MLA_KV_LORA, MLA_HEADS * MLA_HEAD_PAD).astype(BF16)
    place = np.zeros((LANES, MLA_HEADS * MLA_HEAD_PAD), np.float32)
    v_ones = np.zeros((1, MLA_HEADS * MLA_HEAD_PAD), np.float32)
    for h in range(MLA_HEADS):
        place[np.arange(MLA_ROPE), h * MLA_HEAD_PAD + MLA_NOPE + np.arange(MLA_ROPE)] = 1.0
        v_ones[0, h * MLA_HEAD_PAD + MLA_DV] = 1.0
    return {
        "w1": w1, "wq": wq, "wk": wk, "wv": wv, "e": jnp.asarray(place, BF16), "v_ones": jnp.asarray(v_ones),
        "wgate": jnp.pad(w["gla_w_gate_up"][l], ((0, LANES - GLA_GATE_RANK), (0, 0))).astype(BF16),
        "bgate": w["gla_b_gate"][l][None, :],
        "qn": w["mla_q_norm_g"][l][None, :], "kvn": w["mla_kv_norm_g"][l][None, :],
        "wrg": r_g.astype(BF16), "wg3": w_in[:, offs[11]:offs[13]].astype(BF16),
        "w_ret_o": w["w_ret_o"][l].astype(BF16), "w_mla_o": w["w_mla_o"][l].astype(BF16),
        "w_gla_o": w["w_gla_o"][l].astype(BF16), "w_out": w["w_out"][l].astype(BF16),
        "ret_gn": w["ret_gn_g"][l][None, :], "gla_gn": w["gla_gn_g"][l][None, :],
        "f1u": w["ffn1_up"][l].astype(BF16), "f1d": w["ffn1_down"][l].astype(BF16),
        "f2u": w["ffn2_up"][l].astype(BF16), "f2d": w["ffn2_down"][l].astype(BF16),
        "ln_g": w["ln_g"][l], "ln_b": w["ln_b"][l],
    }


def _rope_tables(pos, tm):
    def cos_sin(half):
        inv = ROPE_THETA ** (-jnp.arange(half, dtype=F32) / half)
        ang = pos.astype(F32)[:, None] * inv[None, :]
        return jnp.cos(ang), jnp.sin(ang)

    t = pos.shape[0]
    c32, s32 = cos_sin(RET_DK // 2)
    c16, s16 = cos_sin(MLA_ROPE // 2)
    cr_h = jnp.tile(jnp.concatenate([c32, c32], axis=1), (1, RET_HEADS))
    sr_h = jnp.tile(jnp.concatenate([-s32, s32], axis=1), (1, RET_HEADS))
    k_scale = RET_DK ** -0.5
    q_scale = (MLA_NOPE + MLA_ROPE) ** -0.5 * float(np.log2(np.e))
    zeros = lambda n: jnp.zeros((t, n), F32)
    tabs = {
        "cr": jnp.concatenate([cr_h, cr_h * k_scale], axis=1),
        "sr": jnp.concatenate([sr_h, sr_h * k_scale], axis=1),
        "cq": jnp.concatenate([jnp.ones((t, MLA_NOPE), F32), c16, c16, zeros(LANES - MLA_NOPE - MLA_ROPE)], axis=1) * q_scale,
        "sq": jnp.concatenate([zeros(MLA_NOPE), -s16, s16, zeros(LANES - MLA_NOPE - MLA_ROPE)], axis=1) * q_scale,
        "ck": jnp.concatenate([c16, c16, zeros(LANES - MLA_ROPE)], axis=1),
        "sk": jnp.concatenate([-s16, s16, zeros(LANES - MLA_ROPE)], axis=1),
    }
    if t < tm:
        tabs = {k: jnp.tile(v, (tm // t, 1)) for k, v in tabs.items()}
    return tabs


def _group_layer(x, b, t, lw, tabs, past, layer, carried):
    n = b * t
    x = _ffn_ln(x, lw["f1u"], lw["f1d"], lw["ln_g"][0:1], lw["ln_b"][0:1])
    rqk, rv, qm, ckv_all, krp, kr_all, gqk, gv, la, *kv = _in_proj(x, lw, tabs, layer, carried, past is None)

    log_gamma = jnp.log(1.0 - 2.0 ** (-5.0 - jnp.arange(RET_HEADS, dtype=F32)))
    ret_la = jnp.repeat(log_gamma, RET_DK)[None, :]
    s_ret0 = None if past is None else jnp.swapaxes(past[2], -1, -2)
    s_gla0 = None if past is None else jnp.swapaxes(past[3], -1, -2)
    o_r, s_retT = _scan(rqk.reshape(b, t, -1), rv.reshape(b, t, -1), ret_la, s_ret0,
                        heads=RET_HEADS, dk=RET_DK, dv=RET_DV)
    o_g, s_glaT = _scan(gqk.reshape(b, t, -1), gv.reshape(b, t, -1), la.reshape(b, t, -1), s_gla0,
                        heads=GLA_HEADS, dk=GLA_DK, dv=GLA_DV)

    qm = qm.reshape(b, t, -1)
    if past is None:
        o_m = _attention(qm, kv[0].reshape(b, t, -1), kv[1].reshape(b, t, -1), tq=min(ATT_TQ, t), tk=min(ATT_TK, t))
    else:
        o_m = _attention_cached(qm, past[0], past[1], ckv_all, krp, lw, layer)

    x = _out_proj_ln(x, o_r.reshape(n, -1), o_m.reshape(n, -1), o_g.reshape(n, -1), lw,
                     lw["ln_g"][1:2], lw["ln_b"][1:2])
    x = _ffn_ln(x, lw["f2u"], lw["f2d"], lw["ln_g"][2:3], lw["ln_b"][2:3])
    return x, (ckv_all, kr_all), (jnp.swapaxes(s_retT, -1, -2), jnp.swapaxes(s_glaT, -1, -2))


def kernel(x_prompt, x_sample, cache_mla_ckv, cache_mla_krope, state_ret, state_gla, w_in, ret_gn_g, mla_q_norm_g, mla_w_q_up, mla_kv_norm_g, mla_w_kv_up, gla_w_gate_up, gla_b_gate, gla_gn_g, w_ret_o, w_mla_o, w_gla_o, w_out, ffn1_up, ffn1_down, ffn2_up, ffn2_down, ln_g, ln_b):
    w = dict(w_in=w_in, ret_gn_g=ret_gn_g, mla_q_norm_g=mla_q_norm_g, mla_w_q_up=mla_w_q_up,
             mla_kv_norm_g=mla_kv_norm_g, mla_w_kv_up=mla_w_kv_up, gla_w_gate_up=gla_w_gate_up,
             gla_b_gate=gla_b_gate, gla_gn_g=gla_gn_g, w_ret_o=w_ret_o, w_mla_o=w_mla_o, w_gla_o=w_gla_o,
             w_out=w_out, ffn1_up=ffn1_up, ffn1_down=ffn1_down, ffn2_up=ffn2_up, ffn2_down=ffn2_down,
             ln_g=ln_g, ln_b=ln_b)
    bp, tp, _ = x_prompt.shape
    bs, ts, _ = x_sample.shape
    t_past = cache_mla_ckv.shape[2]
    assert t_past % CHUNK == 0 and tp % CHUNK == 0 and ts % CHUNK == 0
    tabs_p = _rope_tables(jnp.arange(tp), min(INP_TM, bp * tp))
    tabs_s = _rope_tables(t_past + jnp.arange(ts), min(INP_TM, bs * ts))
    xp = x_prompt.reshape(bp * tp, D_MODEL)
    xs = x_sample.reshape(bs * ts, D_MODEL)
    carried_p = carried_s = None
    st_p, st_s = [], []
    for l in range(DEPTH):
        lw = _prep_layer(w, l)
        xp, carried_p, st = _group_layer(xp, bp, tp, lw, tabs_p, None, l, carried_p)
        st_p.append(st)
        past = (cache_mla_ckv, cache_mla_krope, state_ret[l], state_gla[l])
        xs, carried_s, st = _group_layer(xs, bs, ts, lw, tabs_s, past, l, carried_s)
        st_s.append(st)
    stack = lambda sts, i: jnp.stack([s[i] for s in sts])
    return (xp.reshape(bp, tp, D_MODEL), xs.reshape(bs, ts, D_MODEL),
            carried_p[0].reshape(DEPTH, bp, tp, -1), carried_p[1].reshape(DEPTH, bp, tp, -1),
            stack(st_p, 0), stack(st_p, 1),
            carried_s[0].reshape(DEPTH, bs, ts, -1), carried_s[1].reshape(DEPTH, bs, ts, -1),
            stack(st_s, 0), stack(st_s, 1))
```

```python
import functools

import numpy as np
import jax
import jax.numpy as jnp
from jax import lax
from jax.experimental import pallas as pl
from jax.experimental.pallas import tpu as pltpu

F32 = jnp.float32
BF16 = jnp.bfloat16

D_MODEL = 1024
DEPTH = 2
CHUNK = 64
CHUNK_SHIFT = 6
ALPHA = (2 * DEPTH) ** 0.25
EPS = 1e-5
ROPE_THETA = 10000.0
RET_HEADS, RET_DK, RET_DV = 4, 64, 128
MLA_HEADS, MLA_Q_LORA, MLA_KV_LORA, MLA_NOPE, MLA_ROPE, MLA_DV = 8, 384, 256, 64, 32, 64
GLA_HEADS, GLA_DK, GLA_DV, GLA_GATE_RANK, GLA_TAU = 4, 128, 256, 16, 16.0
D_FF = 2816
N_BRANCH = 3
IN_SPLITS = (RET_HEADS * RET_DK, RET_HEADS * RET_DK, RET_HEADS * RET_DV, RET_HEADS * RET_DV,
             MLA_Q_LORA, MLA_KV_LORA, MLA_ROPE,
             GLA_HEADS * GLA_DK, GLA_HEADS * GLA_DK, GLA_HEADS * GLA_DV, GLA_GATE_RANK, GLA_HEADS * GLA_DV,
             N_BRANCH * D_MODEL)

LANES = 128
MLA_HEAD_PAD = LANES
VMEM_LIMIT = 56 * 1024 * 1024

_C_RQK, _C_RV, _C_CQ, _C_KR, _C_CKV, _C_GQK, _C_GV, _C_LR, _C_END = (
    0, 512, 1024, 1408, 1536, 1792, 2816, 3840, 3968)
_G_GOG, _G_BR = 0, 1024

_NT = (((1,), (1,)), ((), ()))
_TN = (((0,), (0,)), ((), ()))


def _params(*sem):
    return pltpu.CompilerParams(dimension_semantics=sem, vmem_limit_bytes=VMEM_LIMIT)


def _dot(a, b):
    return jnp.dot(a, b, preferred_element_type=F32)


def _layer_norm(z, g, b):
    mu = jnp.mean(z, axis=-1, keepdims=True)
    zc = z - mu
    var = jnp.mean(zc * zc, axis=-1, keepdims=True)
    return zc * lax.rsqrt(var + EPS) * g + b


def _row_spec(tm, cols):
    return pl.BlockSpec((tm, cols), lambda i: (i, 0))


def _full_spec(shape):
    return pl.BlockSpec(shape, lambda *_: (0,) * len(shape), pipeline_mode=pl.Buffered(1))


def _wspec(param):
    arr, idx = param
    return pl.BlockSpec((None,) + arr.shape[1:], lambda *_: (idx,) + (0,) * (arr.ndim - 1), pipeline_mode=pl.Buffered(1))


FFN_TM = 512
FFN_FC = 256


def _ffn_ln_kernel(x_ref, wup_ref, wd_ref, g_ref, b_ref, o_ref, h_ref):
    x = x_ref[...]
    xb = x.astype(BF16)
    for c in range(D_FF // FFN_FC):
        sl = slice(c * FFN_FC, (c + 1) * FFN_FC)
        gate = _dot(xb, wup_ref[:, sl])
        up = _dot(xb, wup_ref[:, D_FF + c * FFN_FC:D_FF + (c + 1) * FFN_FC])
        h_ref[:, sl] = (gate * jax.nn.sigmoid(gate) * up).astype(BF16)
    y = _dot(h_ref[...], wd_ref[...])
    o_ref[...] = _layer_norm(ALPHA * x + 0.5 * y, g_ref[...], b_ref[...])


def _ffn_ln(x, wup, wd, g, b):
    n = x.shape[0]
    tm = min(FFN_TM, n)
    params = [wup, wd, g, b]
    return pl.pallas_call(
        _ffn_ln_kernel,
        grid=(n // tm,),
        in_specs=[_row_spec(tm, D_MODEL)] + [_wspec(p) for p in params],
        out_specs=_row_spec(tm, D_MODEL),
        out_shape=jax.ShapeDtypeStruct((n, D_MODEL), F32),
        scratch_shapes=[pltpu.VMEM((tm, D_FF), BF16)],
        compiler_params=_params("parallel"),
        name="ffn_ln",
    )(x, *[p[0] for p in params])


INP_TM = 512


def _swap_halves(x, first_mask, half):
    return jnp.where(first_mask, pltpu.roll(x, LANES - half, 1), pltpu.roll(x, half, 1))


def _in_proj_kernel(x_ref, w_ref, wq_ref, wgate_ref, bgate_ref, qn_ref, kvn_ref,
                    cr_ref, sr_ref, cq_ref, sq_ref, ck_ref, sk_ref, *rest, n_kv_out):
    rqk_ref, rv_ref, qm_ref, ckv_ref, kr_ref, kr_out_ref, gqk_ref, gv_ref, la_ref = rest[-9 - n_kv_out:][:9]
    xb = x_ref[...].astype(BF16)
    tm = xb.shape[0]
    lane = lax.broadcasted_iota(jnp.int32, (tm, LANES), 1)
    ghd = GLA_HEADS * GLA_DK

    h_rqk = _dot(xb, w_ref[:, _C_RQK:_C_RV])
    h_cq_kr = _dot(xb, w_ref[:, _C_CQ:_C_CKV])

    ret_first = (lane & (RET_DK - 1)) < RET_DK // 2
    for c in range(2 * RET_HEADS * RET_DK // LANES):
        sl = slice(c * LANES, (c + 1) * LANES)
        h = h_rqk[:, sl]
        rqk_ref[:, sl] = h * cr_ref[:, sl] + _swap_halves(h, ret_first, RET_DK // 2) * sr_ref[:, sl]
    rv_ref[...] = _dot(xb, w_ref[:, _C_RV:_C_CQ]).astype(BF16)

    hq = h_cq_kr[:, :MLA_Q_LORA]
    cq = hq * lax.rsqrt(jnp.mean(hq * hq, axis=-1, keepdims=True) + EPS) * qn_ref[...]
    q_up = _dot(cq.astype(BF16), wq_ref[...])
    hkv = _dot(xb, w_ref[:, _C_CKV:_C_GQK])
    gv_ref[...] = _dot(xb, w_ref[:, _C_GV:_C_LR]).astype(BF16)
    q_first = lane < MLA_NOPE + MLA_ROPE // 2
    for h_i in range(MLA_HEADS):
        sl = slice(h_i * MLA_HEAD_PAD, (h_i + 1) * MLA_HEAD_PAD)
        qh = q_up[:, sl]
        qm_ref[:, sl] = (qh * cq_ref[...] + _swap_halves(qh, q_first, MLA_ROPE // 2) * sq_ref[...]).astype(BF16)

    hkr = h_cq_kr[:, MLA_Q_LORA:]
    kr = hkr * ck_ref[...] + _swap_halves(hkr, lane < MLA_ROPE // 2, MLA_ROPE // 2) * sk_ref[...]
    kr_ref[...] = kr
    kr_out_ref[...] = kr[:, :MLA_ROPE]
    ckv = hkv * lax.rsqrt(jnp.mean(hkv * hkv, axis=-1, keepdims=True) + EPS) * kvn_ref[...]
    ckv_ref[...] = ckv
    lr = _dot(xb, w_ref[:, _C_LR:_C_END]).astype(BF16)
    gla_q = _dot(xb, w_ref[:, _C_GQK:_C_GQK + ghd])
    if n_kv_out:
        wk_ref, e_ref, wv_ref, ones_ref = rest[:4]
        k_ref, v_ref = rest[-2:]
        cb = ckv.astype(BF16)
        k_ref[...] = (_dot(cb, wk_ref[...]) + _dot(kr.astype(BF16), e_ref[...])).astype(BF16)
        v_ref[...] = (_dot(cb, wv_ref[...]) + ones_ref[...]).astype(BF16)

    gqk_ref[:, :ghd] = gla_q * (GLA_DK ** -0.5)
    logit = _dot(lr, wgate_ref[...]) + bgate_ref[...]
    gqk_ref[:, ghd:] = _dot(xb, w_ref[:, _C_GQK + ghd:_C_GV])
    log_sig = jnp.minimum(logit, 0.0) - jnp.log1p(jnp.exp(-jnp.abs(logit)))
    la_ref[...] = log_sig / GLA_TAU


def _in_proj(x, lw, tabs, layer, carried, emit_kv):
    n = x.shape[0]
    tm = min(INP_TM, n)
    period = tabs["cr"].shape[0] // tm

    def tab_spec(cols):
        return pl.BlockSpec((tm, cols), lambda i: (i % period, 0))

    def layer_spec(cols):
        return pl.BlockSpec((None, tm, cols), lambda i: (layer, i, 0))

    rows = lambda cols, dtype: (_row_spec(tm, cols), jax.ShapeDtypeStruct((n, cols), dtype))
    stacked = lambda cols: (layer_spec(cols), jax.ShapeDtypeStruct((DEPTH, n, cols), F32))
    outs = [rows(2 * RET_HEADS * RET_DK, F32), rows(RET_HEADS * RET_DV, BF16), rows(MLA_HEADS * MLA_HEAD_PAD, BF16),
            stacked(MLA_KV_LORA), rows(LANES, F32), stacked(MLA_ROPE), rows(2 * GLA_HEADS * GLA_DK, F32),
            rows(GLA_HEADS * GLA_DV, BF16), rows(GLA_HEADS * GLA_DK, F32)]
    weights = [lw["w1"], lw["wq"], lw["wgate"], lw["bgate"], lw["qn"], lw["kvn"]]
    tables = [tabs["cr"], tabs["sr"], tabs["cq"], tabs["sq"], tabs["ck"], tabs["sk"]]
    in_specs = ([_row_spec(tm, D_MODEL)] + [_wspec(p) for p in weights]
                + [tab_spec(t.shape[1]) for t in tables])
    args = [x, *[p[0] for p in weights], *tables]
    if emit_kv:
        kv_weights = [lw["wk"], lw["e"], lw["wv"], lw["v_ones"]]
        in_specs += [_wspec(p) for p in kv_weights]
        args += [p[0] for p in kv_weights]
        outs += [rows(MLA_HEADS * MLA_HEAD_PAD, BF16)] * 2
    aliases = {}
    if carried is not None:
        aliases = {len(args): 3, len(args) + 1: 5}
        in_specs += [pl.BlockSpec(memory_space=pl.ANY)] * 2
        args += list(carried)
    return pl.pallas_call(
        functools.partial(_in_proj_kernel, n_kv_out=2 if emit_kv else 0),
        grid=(n // tm,),
        in_specs=in_specs,
        out_specs=[o[0] for o in outs],
        out_shape=[o[1] for o in outs],
        input_output_aliases=aliases,
        compiler_params=_params("parallel"),
        name="in_proj",
    )(*args)


SCAN_CHUNKS_PER_STEP = 4
SCAN_STREAMS_PER_STEP = 4


def _cumsum_rows(a):
    rows = lax.broadcasted_iota(jnp.int32, a.shape, 0)
    s = 1
    while s < a.shape[0]:
        a = a + jnp.where(rows >= s, pltpu.roll(a, s, 0), 0.0)
        s *= 2
    return a


def _scan_kernel(*refs, heads, dk, dv, n_chunks, nb, has_la, has_s0):
    it = iter(refs)
    qk_ref, v_ref = next(it), next(it)
    la_ref = next(it)
    s0_ref = next(it) if has_s0 else None
    o_ref, sT_ref, st_ref = next(it), next(it), next(it)
    hd = heads * dk
    step = pl.program_id(1)

    @pl.when(step == 0)
    def _():
        if has_s0:
            st_ref[...] = s0_ref[...]
        else:
            st_ref[...] = jnp.zeros_like(st_ref)

    row = lax.broadcasted_iota(jnp.int32, (CHUNK, CHUNK), 0)
    col = lax.broadcasted_iota(jnp.int32, (CHUNK, CHUNK), 1)
    causal = row >= col
    ksl = lambda h: slice(h * dk, (h + 1) * dk)
    vsl = lambda h: slice(h * dv, (h + 1) * dv)
    chains = [(bi, h) for h in range(heads) for bi in range(nb)]

    def chunk(c, carry):
        rows = pl.ds(pl.multiple_of(c * CHUNK, CHUNK), CHUNK)
        qe, ke, kd, el = [], [], [], []
        for bi in range(nb):
            if has_la:
                bc = _cumsum_rows(la_ref[bi, rows, :])
            else:
                steps = lax.broadcasted_iota(jnp.int32, (CHUNK, hd), 0) + 1
                bc = steps.astype(F32) * la_ref[...]
            bl = bc[CHUNK - 1:CHUNK, :]
            q = qk_ref[bi, rows, :hd]
            k = qk_ref[bi, rows, hd:]
            qe.append((q * jnp.exp(bc)).astype(BF16))
            ke.append((k * jnp.exp(-bc)).astype(BF16))
            kd.append((k * jnp.exp(bl - bc)).astype(BF16))
            el.append(jnp.exp(bl))
        att = [lax.dot_general(qe[bi][:, ksl(h)], ke[bi][:, ksl(h)], _NT, preferred_element_type=F32)
               for bi, h in chains]
        cross = [lax.dot_general(qe[bi][:, ksl(h)], st_ref[bi, h].astype(BF16), _NT, preferred_element_type=F32)
                 for bi, h in chains]
        upd = [lax.dot_general(v_ref[bi, rows, vsl(h)], kd[bi][:, ksl(h)], _TN, preferred_element_type=F32)
               for bi, h in chains]
        for n, (bi, h) in enumerate(chains):
            a = jnp.where(causal, att[n], 0.0).astype(BF16)
            o_ref[bi, rows, vsl(h)] = _dot(a, v_ref[bi, rows, vsl(h)]) + cross[n]
        for n, (bi, h) in enumerate(chains):
            st_ref[bi, h] = st_ref[bi, h] * el[bi][:, ksl(h)] + upd[n]
        return carry

    lax.fori_loop(0, n_chunks, chunk, 0)

    @pl.when(step == pl.num_programs(1) - 1)
    def _():
        sT_ref[...] = st_ref[...]


def _scan(qk, v, la, s0T, *, heads, dk, dv):
    b, t, _ = qk.shape
    has_la = la.ndim == 3
    has_s0 = s0T is not None
    nb = min(SCAN_STREAMS_PER_STEP, b)
    ncs = min(SCAN_CHUNKS_PER_STEP, t // CHUNK)
    rows = ncs * CHUNK
    hd, hv = heads * dk, heads * dv

    def seq_spec(cols):
        return pl.BlockSpec((nb, rows, cols), lambda bi, si: (bi, si, 0))

    st_spec = pl.BlockSpec((nb, heads, dv, dk), lambda bi, si: (bi, 0, 0, 0))
    in_specs = [seq_spec(2 * hd), seq_spec(hv), seq_spec(hd) if has_la else _full_spec(la.shape)]
    args = [qk, v, la]
    if has_s0:
        in_specs.append(st_spec)
        args.append(s0T)
    kern = functools.partial(_scan_kernel, heads=heads, dk=dk, dv=dv, n_chunks=ncs, nb=nb,
                             has_la=has_la, has_s0=has_s0)
    return pl.pallas_call(
        kern,
        grid=(b // nb, t // rows),
        in_specs=in_specs,
        out_specs=[seq_spec(hv), st_spec],
        out_shape=[jax.ShapeDtypeStruct((b, t, hv), F32), jax.ShapeDtypeStruct((b, heads, dv, dk), F32)],
        scratch_shapes=[pltpu.VMEM((nb, heads, dv, dk), F32)],
        compiler_params=_params("parallel", "arbitrary"),
        name="scan_h%d_dk%d" % (heads, dk),
    )(*args)


ATT_TQ = 1024
ATT_TK = 512
ATT_SUB = 512
ATT_LOOKAHEAD = 2


def _attn_kernel(i_ref, j_ref, q_ref, k_ref, v_ref, o_ref, m_ref, acc_ref, *, tq, tk, nk):
    i, j = i_ref[pl.program_id(1)], j_ref[pl.program_id(1)]
    n_sub = tq // ATT_SUB
    q_lo = [(i * tq + r * ATT_SUB) // CHUNK for r in range(n_sub)]
    q_hi = [(i * tq + (r + 1) * ATT_SUB - 1) // CHUNK for r in range(n_sub)]
    k_lo = (j * tk) // CHUNK
    k_hi = (j * tk + tk - 1) // CHUNK
    j_last = jnp.minimum(nk - 1, ((q_hi[-1] + 1) * CHUNK - 1) // tk)

    @pl.when(j == 0)
    def _():
        m_ref[...] = jnp.full_like(m_ref, -jnp.inf)
        acc_ref[...] = jnp.zeros_like(acc_ref)

    def tile(modes):
        vis = {}
        for r, mode in enumerate(modes):
            if mode == "masked":
                row0 = i * tq + r * ATT_SUB
                qc = (row0 + lax.broadcasted_iota(jnp.int32, (ATT_SUB, tk), 0)) >> CHUNK_SHIFT
                kc = (j * tk + lax.broadcasted_iota(jnp.int32, (ATT_SUB, tk), 1)) >> CHUNK_SHIFT
                vis[r] = kc <= qc
        work = [(h, r) for h in range(MLA_HEADS) for r, mode in enumerate(modes) if mode != "skip"]

        def scores(h, r):
            hs = slice(h * MLA_HEAD_PAD, (h + 1) * MLA_HEAD_PAD)
            return lax.dot_general(q_ref[0, r * ATT_SUB:(r + 1) * ATT_SUB, hs], k_ref[0, :, hs], _NT,
                                   preferred_element_type=F32)

        pending = [scores(*w) for w in work[:ATT_LOOKAHEAD]]
        for n, (h, r) in enumerate(work):
            hs = slice(h * MLA_HEAD_PAD, (h + 1) * MLA_HEAD_PAD)
            rs = slice(r * ATT_SUB, (r + 1) * ATT_SUB)
            s = pending.pop(0)
            if n + ATT_LOOKAHEAD < len(work):
                pending.append(scores(*work[n + ATT_LOOKAHEAD]))
            if modes[r] == "masked":
                s = jnp.where(vis[r], s, -jnp.inf)
            m_prev = m_ref[h, rs, :]
            m_new = jnp.maximum(m_prev, jnp.max(s, axis=-1, keepdims=True))
            alpha = jnp.exp2(m_prev - m_new)
            p = jnp.concatenate([jnp.exp2(s[:, c * LANES:(c + 1) * LANES] - m_new).astype(BF16)
                                 for c in range(tk // LANES)], axis=1)
            acc_ref[h, rs, :] = alpha * acc_ref[h, rs, :] + _dot(p, v_ref[0, :, hs])
            m_ref[h, rs, :] = m_new

    def cond(r, mode):
        if mode == "full":
            return k_hi <= q_lo[r]
        if mode == "masked":
            return jnp.logical_and(k_hi > q_lo[r], k_lo <= q_hi[r])
        return k_lo > q_hi[r]

    combos = [("full",) * n_sub] + [("skip",) * r + ("masked",) + ("full",) * (n_sub - r - 1) for r in range(n_sub)]
    for modes in combos:
        pred = functools.reduce(jnp.logical_and, [cond(r, mode) for r, mode in enumerate(modes)])
        pl.when(pred)(functools.partial(tile, modes))

    @pl.when(j == j_last)
    def _():
        for h in range(MLA_HEADS):
            a = acc_ref[h]
            o_ref[0, :, h * MLA_DV:(h + 1) * MLA_DV] = (a[:, :MLA_DV] / a[:, MLA_DV:MLA_DV + 1]).astype(o_ref.dtype)


def _attention(q, k, v, *, tq, tk):
    b, t_q, _ = q.shape
    t_k = k.shape[1]
    assert tq % ATT_SUB == 0 and t_q % tq == 0 and t_k % tk == 0
    nq, nk = t_q // tq, t_k // tk
    width = MLA_HEADS * MLA_HEAD_PAD
    pairs = [(i, j) for i in range(nq)
             for j in range(min(nk - 1, (((i * tq + tq - 1) // CHUNK + 1) * CHUNK - 1) // tk) + 1)]
    i_tab = jnp.asarray([p[0] for p in pairs], jnp.int32)
    j_tab = jnp.asarray([p[1] for p in pairs], jnp.int32)

    kern = functools.partial(_attn_kernel, tq=tq, tk=tk, nk=nk)
    grid_spec = pltpu.PrefetchScalarGridSpec(
        num_scalar_prefetch=2,
        grid=(b, len(pairs)),
        in_specs=[pl.BlockSpec((1, tq, width), lambda bi, p, it, jt: (bi, it[p], 0)),
                  pl.BlockSpec((1, tk, width), lambda bi, p, it, jt: (bi, jt[p], 0)),
                  pl.BlockSpec((1, tk, width), lambda bi, p, it, jt: (bi, jt[p], 0))],
        out_specs=pl.BlockSpec((1, tq, MLA_HEADS * MLA_DV), lambda bi, p, it, jt: (bi, it[p], 0)),
        scratch_shapes=[pltpu.VMEM((MLA_HEADS, tq, LANES), F32), pltpu.VMEM((MLA_HEADS, tq, MLA_HEAD_PAD), F32)],
    )
    return pl.pallas_call(
        kern,
        grid_spec=grid_spec,
        out_shape=jax.ShapeDtypeStruct((b, t_q, MLA_HEADS * MLA_DV), BF16),
        compiler_params=_params("parallel", "arbitrary"),
        name="mla_attention",
    )(i_tab, j_tab, q, k, v)


def _attn_cached_kernel(q_ref, cp_ref, krp_ref, cn_ref, krn_ref, wk_ref, e_ref, wv_ref, o_ref):
    t_new = q_ref.shape[1]
    hsl = lambda h: slice(h * MLA_HEAD_PAD, (h + 1) * MLA_HEAD_PAD)
    ckv_p = cp_ref[...].astype(BF16)
    ckv_n = cn_ref[...].astype(BF16)
    kr_p = _dot(krp_ref[...].astype(BF16), e_ref[:MLA_ROPE, :]).astype(BF16)
    kr_n = _dot(krn_ref[...].astype(BF16), e_ref[...]).astype(BF16)
    q_all = jnp.concatenate([q_ref[0, :, hsl(h)] for h in range(MLA_HEADS)], axis=0)
    q_lat = jnp.concatenate(
        [lax.dot_general(q_ref[0, :, hsl(h)], wk_ref[:, hsl(h)], _NT, preferred_element_type=F32)
         for h in range(MLA_HEADS)], axis=0).astype(BF16)
    s_past = (lax.dot_general(q_lat, ckv_p, _NT, preferred_element_type=F32)
              + lax.dot_general(q_all, kr_p, _NT, preferred_element_type=F32))
    s_new = (lax.dot_general(q_lat, ckv_n, _NT, preferred_element_type=F32)
             + lax.dot_general(q_all, kr_n, _NT, preferred_element_type=F32))
    q_chunk = jnp.concatenate([lax.broadcasted_iota(jnp.int32, (t_new, t_new), 0) >> CHUNK_SHIFT] * MLA_HEADS, axis=0)
    k_chunk = lax.broadcasted_iota(jnp.int32, (MLA_HEADS * t_new, t_new), 1) >> CHUNK_SHIFT
    s_new = jnp.where(k_chunk <= q_chunk, s_new, -jnp.inf)
    m = jnp.maximum(jnp.max(s_past, axis=-1, keepdims=True), jnp.max(s_new, axis=-1, keepdims=True))
    p_past = jnp.exp2(s_past - m)
    p_new = jnp.exp2(s_new - m)
    denom = jnp.sum(p_past, axis=-1, keepdims=True) + jnp.sum(p_new, axis=-1, keepdims=True)
    o_lat = ((_dot(p_past.astype(BF16), ckv_p) + _dot(p_new.astype(BF16), ckv_n)) / denom).astype(BF16)
    for h in range(MLA_HEADS):
        o = _dot(o_lat[h * t_new:(h + 1) * t_new], wv_ref[:, hsl(h)])
        o_ref[0, :, h * MLA_DV:(h + 1) * MLA_DV] = o[:, :MLA_DV].astype(o_ref.dtype)


def _attention_cached(q, cache_ckv, cache_kr, ckv_all, krp, lw, layer):
    b, t, width = q.shape
    t_past = cache_ckv.shape[2]
    weights = [lw["wk"], lw["e_head"], lw["wv"]]
    return pl.pallas_call(
        _attn_cached_kernel,
        grid=(b,),
        in_specs=[pl.BlockSpec((1, t, width), lambda bi: (bi, 0, 0)),
                  pl.BlockSpec((t_past, MLA_KV_LORA), lambda bi: (layer * b + bi, 0)),
                  pl.BlockSpec((t_past, MLA_ROPE), lambda bi: (layer * b + bi, 0)),
                  pl.BlockSpec((t, MLA_KV_LORA), lambda bi: (layer * b + bi, 0)),
                  pl.BlockSpec((t, LANES), lambda bi: (bi, 0))] + [_wspec(p) for p in weights],
        out_specs=pl.BlockSpec((1, t, MLA_HEADS * MLA_DV), lambda bi: (bi, 0, 0)),
        out_shape=jax.ShapeDtypeStruct((b, t, MLA_HEADS * MLA_DV), BF16),
        compiler_params=_params("parallel"),
        name="mla_attention_cached",
    )(q, cache_ckv.reshape(-1, MLA_KV_LORA), cache_kr.reshape(-1, MLA_ROPE), ckv_all.reshape(-1, MLA_KV_LORA),
      krp, *[p[0] for p in weights])


OUT_TM = 512
OUT_SUB = 256


def _out_proj_kernel(x_ref, or_ref, om_ref, og_ref, wrg_ref, wg3_ref, wro_ref, wmo_ref, wgo_ref, wout_ref,
                     rgn_ref, ggn_ref, g_ref, b_ref, o_ref, hr_ref, hg_ref):
    tm = x_ref.shape[0]
    sub = min(OUT_SUB, tm)
    for r in range(tm // sub):
        rs = slice(r * sub, (r + 1) * sub)
        x = x_ref[rs, :]
        xb = x.astype(BF16)

        ret_gate = _dot(xb, wrg_ref[...])
        gla_gate = [_dot(xb, wg3_ref[:, _G_GOG + h * GLA_DV:_G_GOG + (h + 1) * GLA_DV]) for h in range(GLA_HEADS)]

        for h in range(RET_HEADS):
            sl = slice(h * RET_DV, (h + 1) * RET_DV)
            gate = ret_gate[:, sl]
            o = or_ref[rs, sl]
            oc = o - jnp.mean(o, axis=-1, keepdims=True)
            normed = oc * lax.rsqrt(jnp.mean(oc * oc, axis=-1, keepdims=True) + EPS) * rgn_ref[:, sl]
            hr_ref[rs, sl] = (normed * (gate * jax.nn.sigmoid(gate))).astype(BF16)
        y_m = _dot(om_ref[rs, :], wmo_ref[...])
        gate_m = _dot(xb, wg3_ref[:, _G_BR + D_MODEL:_G_BR + 2 * D_MODEL])

        for h in range(GLA_HEADS):
            sl = slice(h * GLA_DV, (h + 1) * GLA_DV)
            gate = gla_gate[h]
            o = og_ref[rs, sl]
            normed = o * lax.rsqrt(jnp.mean(o * o, axis=-1, keepdims=True) + EPS) * ggn_ref[:, sl]
            hg_ref[rs, sl] = (normed * (gate * jax.nn.sigmoid(gate))).astype(BF16)
        y_r = _dot(hr_ref[rs, :], wro_ref[...])
        gate_r = _dot(xb, wg3_ref[:, _G_BR:_G_BR + D_MODEL])
        mix = jax.nn.sigmoid(gate_m) * y_m + jax.nn.sigmoid(gate_r) * y_r
        y_g = _dot(hg_ref[rs, :], wgo_ref[...])
        gate_g = _dot(xb, wg3_ref[:, _G_BR + 2 * D_MODEL:_G_BR + 3 * D_MODEL])
        mix = mix + jax.nn.sigmoid(gate_g) * y_g
        y = _dot(mix.astype(BF16), wout_ref[...])
        o_ref[rs, :] = _layer_norm(ALPHA * x + y, g_ref[...], b_ref[...])


def _out_proj_ln(x, o_r, o_m, o_g, lw, g, b):
    n = x.shape[0]
    tm = min(OUT_TM, n)
    weights = [lw["wrg"], lw["wg3"], lw["w_ret_o"], lw["w_mla_o"], lw["w_gla_o"], lw["w_out"],
               lw["ret_gn"], lw["gla_gn"], g, b]
    return pl.pallas_call(
        _out_proj_kernel,
        grid=(n // tm,),
        in_specs=[_row_spec(tm, D_MODEL), _row_spec(tm, RET_HEADS * RET_DV), _row_spec(tm, MLA_HEADS * MLA_DV),
                  _row_spec(tm, GLA_HEADS * GLA_DV)] + [_wspec(p) for p in weights],
        out_specs=_row_spec(tm, D_MODEL),
        out_shape=jax.ShapeDtypeStruct((n, D_MODEL), F32),
        scratch_shapes=[pltpu.VMEM((tm, RET_HEADS * RET_DV), BF16), pltpu.VMEM((tm, GLA_HEADS * GLA_DV), BF16)],
        compiler_params=_params("parallel"),
        name="out_proj_ln",
    )(x, o_r, o_m, o_g, *[p[0] for p in weights])


def _prep_weights(w):
    offs = np.cumsum((0,) + IN_SPLITS)
    w_in = w["w_in"]
    (r_q, r_k, r_v, r_g, m_cq, m_ckv, m_kr, g_q, g_k, g_v, g_lr, g_og, br) = [
        w_in[:, :, offs[i]:offs[i + 1]] for i in range(len(IN_SPLITS))]

    def pad_last(a, n):
        return jnp.pad(a, [(0, 0)] * (a.ndim - 1) + [(0, n - a.shape[-1])])

    w1 = jnp.concatenate([r_q, r_k, r_v, m_cq, pad_last(m_kr, LANES), m_ckv, g_q, g_k, g_v,
                          pad_last(g_lr, LANES)], axis=2).astype(BF16)
    dq = MLA_NOPE + MLA_ROPE
    width = MLA_HEADS * MLA_HEAD_PAD
    wq = pad_last(w["mla_w_q_up"].reshape(DEPTH, MLA_Q_LORA, MLA_HEADS, dq), MLA_HEAD_PAD).reshape(
        DEPTH, MLA_Q_LORA, width).astype(BF16)
    kv = w["mla_w_kv_up"].reshape(DEPTH, MLA_KV_LORA, MLA_HEADS, MLA_NOPE + MLA_DV)
    wk = pad_last(kv[..., :MLA_NOPE], MLA_HEAD_PAD).reshape(DEPTH, MLA_KV_LORA, width).astype(BF16)
    wv = pad_last(kv[..., MLA_NOPE:], MLA_HEAD_PAD).reshape(DEPTH, MLA_KV_LORA, width).astype(BF16)
    place = np.zeros((1, LANES, width), np.float32)
    v_ones = np.zeros((1, 1, width), np.float32)
    for h in range(MLA_HEADS):
        place[0, np.arange(MLA_ROPE), h * MLA_HEAD_PAD + MLA_NOPE + np.arange(MLA_ROPE)] = 1.0
        v_ones[0, 0, h * MLA_HEAD_PAD + MLA_DV] = 1.0
    row = lambda a: a.reshape(a.shape[0], 1, a.shape[-1])
    return {
        "w1": w1, "wq": wq, "wk": wk, "wv": wv,
        "e": jnp.asarray(place, BF16), "e_head": jnp.asarray(place[:, :, :MLA_HEAD_PAD], BF16), "v_ones": jnp.asarray(v_ones),
        "wgate": jnp.pad(w["gla_w_gate_up"], ((0, 0), (0, LANES - GLA_GATE_RANK), (0, 0))).astype(BF16),
        "bgate": row(w["gla_b_gate"]), "qn": row(w["mla_q_norm_g"]), "kvn": row(w["mla_kv_norm_g"]),
        "wrg": r_g.astype(BF16), "wg3": w_in[:, :, offs[11]:offs[13]].astype(BF16),
        "w_ret_o": w["w_ret_o"].astype(BF16), "w_mla_o": w["w_mla_o"].astype(BF16),
        "w_gla_o": w["w_gla_o"].astype(BF16), "w_out": w["w_out"].astype(BF16),
        "ret_gn": row(w["ret_gn_g"]), "gla_gn": row(w["gla_gn_g"]),
        "f1u": w["ffn1_up"].astype(BF16), "f1d": w["ffn1_down"].astype(BF16),
        "f2u": w["ffn2_up"].astype(BF16), "f2d": w["ffn2_down"].astype(BF16),
        "ln_g": w["ln_g"].reshape(DEPTH * 3, 1, D_MODEL), "ln_b": w["ln_b"].reshape(DEPTH * 3, 1, D_MODEL),
    }


def _layer_params(sw, layer):
    lw = {k: (v, layer if v.shape[0] == DEPTH else 0) for k, v in sw.items() if not k.startswith("ln_")}
    for k in range(3):
        lw["ln_g%d" % k] = (sw["ln_g"], layer * 3 + k)
        lw["ln_b%d" % k] = (sw["ln_b"], layer * 3 + k)
    return lw


def _rope_tables(pos, tm):
    def cos_sin(half):
        inv = ROPE_THETA ** (-jnp.arange(half, dtype=F32) / half)
        ang = pos.astype(F32)[:, None] * inv[None, :]
        return jnp.cos(ang), jnp.sin(ang)

    t = pos.shape[0]
    c32, s32 = cos_sin(RET_DK // 2)
    c16, s16 = cos_sin(MLA_ROPE // 2)
    cr_h = jnp.tile(jnp.concatenate([c32, c32], axis=1), (1, RET_HEADS))
    sr_h = jnp.tile(jnp.concatenate([-s32, s32], axis=1), (1, RET_HEADS))
    k_scale = RET_DK ** -0.5
    q_scale = (MLA_NOPE + MLA_ROPE) ** -0.5 * float(np.log2(np.e))
    zeros = lambda n: jnp.zeros((t, n), F32)
    tabs = {
        "cr": jnp.concatenate([cr_h, cr_h * k_scale], axis=1),
        "sr": jnp.concatenate([sr_h, sr_h * k_scale], axis=1),
        "cq": jnp.concatenate([jnp.ones((t, MLA_NOPE), F32), c16, c16, zeros(LANES - MLA_NOPE - MLA_ROPE)], axis=1) * q_scale,
        "sq": jnp.concatenate([zeros(MLA_NOPE), -s16, s16, zeros(LANES - MLA_NOPE - MLA_ROPE)], axis=1) * q_scale,
        "ck": jnp.concatenate([c16, c16, zeros(LANES - MLA_ROPE)], axis=1),
        "sk": jnp.concatenate([-s16, s16, zeros(LANES - MLA_ROPE)], axis=1),
    }
    if t < tm:
        tabs = {k: jnp.tile(v, (tm // t, 1)) for k, v in tabs.items()}
    return tabs


def _group_layer(x, b, t, lw, tabs, past, layer, carried):
    n = b * t
    x = _ffn_ln(x, lw["f1u"], lw["f1d"], lw["ln_g0"], lw["ln_b0"])
    rqk, rv, qm, ckv_all, krp, kr_all, gqk, gv, la, *kv = _in_proj(x, lw, tabs, layer, carried, past is None)

    log_gamma = jnp.log(1.0 - 2.0 ** (-5.0 - jnp.arange(RET_HEADS, dtype=F32)))
    ret_la = jnp.repeat(log_gamma, RET_DK)[None, :]
    s_ret0 = None if past is None else jnp.swapaxes(past[2], -1, -2)
    s_gla0 = None if past is None else jnp.swapaxes(past[3], -1, -2)
    o_r, s_retT = _scan(rqk.reshape(b, t, -1), rv.reshape(b, t, -1), ret_la, s_ret0,
                        heads=RET_HEADS, dk=RET_DK, dv=RET_DV)
    o_g, s_glaT = _scan(gqk.reshape(b, t, -1), gv.reshape(b, t, -1), la.reshape(b, t, -1), s_gla0,
                        heads=GLA_HEADS, dk=GLA_DK, dv=GLA_DV)

    qm = qm.reshape(b, t, -1)
    if past is None:
        o_m = _attention(qm, kv[0].reshape(b, t, -1), kv[1].reshape(b, t, -1), tq=min(ATT_TQ, t), tk=min(ATT_TK, t))
    else:
        o_m = _attention_cached(qm, past[0], past[1], ckv_all, krp, lw, layer)

    x = _out_proj_ln(x, o_r.reshape(n, -1), o_m.reshape(n, -1), o_g.reshape(n, -1), lw, lw["ln_g1"], lw["ln_b1"])
    x = _ffn_ln(x, lw["f2u"], lw["f2d"], lw["ln_g2"], lw["ln_b2"])
    return x, (ckv_all, kr_all), (jnp.swapaxes(s_retT, -1, -2), jnp.swapaxes(s_glaT, -1, -2))


def kernel(x_prompt, x_sample, cache_mla_ckv, cache_mla_krope, state_ret, state_gla, w_in, ret_gn_g, mla_q_norm_g, mla_w_q_up, mla_kv_norm_g, mla_w_kv_up, gla_w_gate_up, gla_b_gate, gla_gn_g, w_ret_o, w_mla_o, w_gla_o, w_out, ffn1_up, ffn1_down, ffn2_up, ffn2_down, ln_g, ln_b):
    w = dict(w_in=w_in, ret_gn_g=ret_gn_g, mla_q_norm_g=mla_q_norm_g, mla_w_q_up=mla_w_q_up,
             mla_kv_norm_g=mla_kv_norm_g, mla_w_kv_up=mla_w_kv_up, gla_w_gate_up=gla_w_gate_up,
             gla_b_gate=gla_b_gate, gla_gn_g=gla_gn_g, w_ret_o=w_ret_o, w_mla_o=w_mla_o, w_gla_o=w_gla_o,
             w_out=w_out, ffn1_up=ffn1_up, ffn1_down=ffn1_down, ffn2_up=ffn2_up, ffn2_down=ffn2_down,
             ln_g=ln_g, ln_b=ln_b)
    bp, tp, _ = x_prompt.shape
    bs, ts, _ = x_sample.shape
    t_past = cache_mla_ckv.shape[2]
    assert t_past % CHUNK == 0 and tp % CHUNK == 0 and ts % CHUNK == 0
    tabs_p = _rope_tables(jnp.arange(tp), min(INP_TM, bp * tp))
    tabs_s = _rope_tables(t_past + jnp.arange(ts), min(INP_TM, bs * ts))
    xp = x_prompt.reshape(bp * tp, D_MODEL)
    xs = x_sample.reshape(bs * ts, D_MODEL)
    carried_p = carried_s = None
    st_p, st_s = [], []
    sw = _prep_weights(w)
    for l in range(DEPTH):
        lw = _layer_params(sw, l)
        xp, carried_p, st = _group_layer(xp, bp, tp, lw, tabs_p, None, l, carried_p)
        st_p.append(st)
        past = (cache_mla_ckv, cache_mla_krope, state_ret[l], state_gla[l])
        xs, carried_s, st = _group_layer(xs, bs, ts, lw, tabs_s, past, l, carried_s)
        st_s.append(st)
    stack = lambda sts, i: jnp.stack([s[i] for s in sts])
    return (xp.reshape(bp, tp, D_MODEL), xs.reshape(bs, ts, D_MODEL),
            carried_p[0].reshape(DEPTH, bp, tp, -1), carried_p[1].reshape(DEPTH, bp, tp, -1),
            stack(st_p, 0), stack(st_p, 1),
            carried_s[0].reshape(DEPTH, bs, ts, -1), carried_s[1].reshape(DEPTH, bs, ts, -1),
            stack(st_s, 0), stack(st_s, 1))
```

```python
import functools

import numpy as np
import jax
import jax.numpy as jnp
from jax import lax
from jax.experimental import pallas as pl
from jax.experimental.pallas import tpu as pltpu

F32 = jnp.float32
BF16 = jnp.bfloat16

D_MODEL = 1024
DEPTH = 2
CHUNK = 64
CHUNK_SHIFT = 6
ALPHA = (2 * DEPTH) ** 0.25
EPS = 1e-5
ROPE_THETA = 10000.0
RET_HEADS, RET_DK, RET_DV = 4, 64, 128
MLA_HEADS, MLA_Q_LORA, MLA_KV_LORA, MLA_NOPE, MLA_ROPE, MLA_DV = 8, 384, 256, 64, 32, 64
GLA_HEADS, GLA_DK, GLA_DV, GLA_GATE_RANK, GLA_TAU = 4, 128, 256, 16, 16.0
D_FF = 2816
N_BRANCH = 3
IN_SPLITS = (RET_HEADS * RET_DK, RET_HEADS * RET_DK, RET_HEADS * RET_DV, RET_HEADS * RET_DV,
             MLA_Q_LORA, MLA_KV_LORA, MLA_ROPE,
             GLA_HEADS * GLA_DK, GLA_HEADS * GLA_DK, GLA_HEADS * GLA_DV, GLA_GATE_RANK, GLA_HEADS * GLA_DV,
             N_BRANCH * D_MODEL)

LANES = 128
MLA_HEAD_PAD = LANES
VMEM_LIMIT = 56 * 1024 * 1024

_C_RQK, _C_RV, _C_CQ, _C_KR, _C_CKV, _C_GQK, _C_GV, _C_LR, _C_END = (
    0, 512, 1024, 1408, 1536, 1792, 2816, 3840, 3968)
_G_GOG, _G_BR = 0, 1024

_NT = (((1,), (1,)), ((), ()))
_TN = (((0,), (0,)), ((), ()))


def _params(*sem):
    return pltpu.CompilerParams(dimension_semantics=sem, vmem_limit_bytes=VMEM_LIMIT)


def _dot(a, b):
    return jnp.dot(a, b, preferred_element_type=F32)


def _layer_norm(z, g, b):
    mu = jnp.mean(z, axis=-1, keepdims=True)
    zc = z - mu
    var = jnp.mean(zc * zc, axis=-1, keepdims=True)
    return zc * lax.rsqrt(var + EPS) * g + b


def _row_spec(tm, cols):
    return pl.BlockSpec((tm, cols), lambda i: (i, 0))


def _full_spec(shape):
    return pl.BlockSpec(shape, lambda *_: (0,) * len(shape), pipeline_mode=pl.Buffered(1))


def _wspec(param):
    arr, idx = param
    return pl.BlockSpec((None,) + arr.shape[1:], lambda *_: (idx,) + (0,) * (arr.ndim - 1), pipeline_mode=pl.Buffered(1))


FFN_TM = 512
FFN_FC = 256


def _ffn_ln_kernel(x_ref, wup_ref, wd_ref, g_ref, b_ref, o_ref, h_ref):
    x = x_ref[...]
    xb = x.astype(BF16)
    for c in range(D_FF // FFN_FC):
        sl = slice(c * FFN_FC, (c + 1) * FFN_FC)
        gate = _dot(xb, wup_ref[:, sl])
        up = _dot(xb, wup_ref[:, D_FF + c * FFN_FC:D_FF + (c + 1) * FFN_FC])
        h_ref[:, sl] = (gate * jax.nn.sigmoid(gate) * up).astype(BF16)
    y = _dot(h_ref[...], wd_ref[...])
    o_ref[...] = _layer_norm(ALPHA * x + 0.5 * y, g_ref[...], b_ref[...])


def _ffn_ln(x, wup, wd, g, b):
    n = x.shape[0]
    tm = min(FFN_TM, n)
    params = [wup, wd, g, b]
    return pl.pallas_call(
        _ffn_ln_kernel,
        grid=(n // tm,),
        in_specs=[_row_spec(tm, D_MODEL)] + [_wspec(p) for p in params],
        out_specs=_row_spec(tm, D_MODEL),
        out_shape=jax.ShapeDtypeStruct((n, D_MODEL), F32),
        scratch_shapes=[pltpu.VMEM((tm, D_FF), BF16)],
        compiler_params=_params("parallel"),
        name="ffn_ln",
    )(x, *[p[0] for p in params])


INP_TM = 512


def _swap_halves(x, first_mask, half):
    return jnp.where(first_mask, pltpu.roll(x, LANES - half, 1), pltpu.roll(x, half, 1))


def _in_proj_kernel(x_ref, w_ref, wq_ref, wgate_ref, bgate_ref, qn_ref, kvn_ref,
                    cr_ref, sr_ref, cq_ref, sq_ref, ck_ref, sk_ref, *rest, n_kv_out):
    rqk_ref, rv_ref, qm_ref, ckv_ref, kr_ref, kr_out_ref, gqk_ref, gv_ref, la_ref = rest[-9 - n_kv_out:][:9]
    xb = x_ref[...].astype(BF16)
    tm = xb.shape[0]
    lane = lax.broadcasted_iota(jnp.int32, (tm, LANES), 1)
    ghd = GLA_HEADS * GLA_DK

    h_rqk = _dot(xb, w_ref[:, _C_RQK:_C_RV])
    h_cq_kr = _dot(xb, w_ref[:, _C_CQ:_C_CKV])

    ret_first = (lane & (RET_DK - 1)) < RET_DK // 2
    for c in range(2 * RET_HEADS * RET_DK // LANES):
        sl = slice(c * LANES, (c + 1) * LANES)
        h = h_rqk[:, sl]
        rqk_ref[:, sl] = h * cr_ref[:, sl] + _swap_halves(h, ret_first, RET_DK // 2) * sr_ref[:, sl]
    rv_ref[...] = _dot(xb, w_ref[:, _C_RV:_C_CQ]).astype(BF16)

    hq = h_cq_kr[:, :MLA_Q_LORA]
    cq = hq * lax.rsqrt(jnp.mean(hq * hq, axis=-1, keepdims=True) + EPS) * qn_ref[...]
    q_up = _dot(cq.astype(BF16), wq_ref[...])
    hkv = _dot(xb, w_ref[:, _C_CKV:_C_GQK])
    gv_ref[...] = _dot(xb, w_ref[:, _C_GV:_C_LR]).astype(BF16)
    q_first = lane < MLA_NOPE + MLA_ROPE // 2
    for h_i in range(MLA_HEADS):
        sl = slice(h_i * MLA_HEAD_PAD, (h_i + 1) * MLA_HEAD_PAD)
        qh = q_up[:, sl]
        qm_ref[:, sl] = (qh * cq_ref[...] + _swap_halves(qh, q_first, MLA_ROPE // 2) * sq_ref[...]).astype(BF16)

    hkr = h_cq_kr[:, MLA_Q_LORA:]
    kr = hkr * ck_ref[...] + _swap_halves(hkr, lane < MLA_ROPE // 2, MLA_ROPE // 2) * sk_ref[...]
    kr_ref[...] = kr
    kr_out_ref[...] = kr[:, :MLA_ROPE]
    ckv = hkv * lax.rsqrt(jnp.mean(hkv * hkv, axis=-1, keepdims=True) + EPS) * kvn_ref[...]
    ckv_ref[...] = ckv
    lr = _dot(xb, w_ref[:, _C_LR:_C_END]).astype(BF16)
    gla_q = _dot(xb, w_ref[:, _C_GQK:_C_GQK + ghd])
    if n_kv_out:
        wkv_ref = rest[0]
        k_ref, v_ref = rest[-2:]
        kv = _dot(ckv.astype(BF16), wkv_ref[...])
        kr_at_rope = pltpu.roll(kr, MLA_NOPE, 1)
        ones_col = jnp.where(lane == MLA_DV, 1.0, 0.0)
        for h_i in range(MLA_HEADS):
            sl = slice(h_i * MLA_HEAD_PAD, (h_i + 1) * MLA_HEAD_PAD)
            g = kv[:, sl]
            k_ref[:, sl] = jnp.where(lane < MLA_NOPE, g, kr_at_rope).astype(BF16)
            v_ref[:, sl] = jnp.where(lane < MLA_DV, pltpu.roll(g, LANES - MLA_NOPE, 1), ones_col).astype(BF16)

    gqk_ref[:, :ghd] = gla_q * (GLA_DK ** -0.5)
    logit = _dot(lr, wgate_ref[...]) + bgate_ref[...]
    gqk_ref[:, ghd:] = _dot(xb, w_ref[:, _C_GQK + ghd:_C_GV])
    log_sig = jnp.minimum(logit, 0.0) - jnp.log1p(jnp.exp(-jnp.abs(logit)))
    la_ref[...] = log_sig / GLA_TAU


def _in_proj(x, lw, tabs, layer, carried, emit_kv):
    n = x.shape[0]
    tm = min(INP_TM, n)
    period = tabs["cr"].shape[0] // tm

    def tab_spec(cols):
        return pl.BlockSpec((tm, cols), lambda i: (i % period, 0))

    def layer_spec(cols):
        return pl.BlockSpec((None, tm, cols), lambda i: (layer, i, 0))

    rows = lambda cols, dtype: (_row_spec(tm, cols), jax.ShapeDtypeStruct((n, cols), dtype))
    stacked = lambda cols: (layer_spec(cols), jax.ShapeDtypeStruct((DEPTH, n, cols), F32))
    outs = [rows(2 * RET_HEADS * RET_DK, F32), rows(RET_HEADS * RET_DV, BF16), rows(MLA_HEADS * MLA_HEAD_PAD, BF16),
            stacked(MLA_KV_LORA), rows(LANES, F32), stacked(MLA_ROPE), rows(2 * GLA_HEADS * GLA_DK, F32),
            rows(GLA_HEADS * GLA_DV, BF16), rows(GLA_HEADS * GLA_DK, F32)]
    weights = [lw["w1"], lw["wq"], lw["wgate"], lw["bgate"], lw["qn"], lw["kvn"]]
    tables = [tabs["cr"], tabs["sr"], tabs["cq"], tabs["sq"], tabs["ck"], tabs["sk"]]
    in_specs = ([_row_spec(tm, D_MODEL)] + [_wspec(p) for p in weights]
                + [tab_spec(t.shape[1]) for t in tables])
    args = [x, *[p[0] for p in weights], *tables]
    if emit_kv:
        kv_weights = [lw["wkv"]]
        in_specs += [_wspec(p) for p in kv_weights]
        args += [p[0] for p in kv_weights]
        outs += [rows(MLA_HEADS * MLA_HEAD_PAD, BF16)] * 2
    aliases = {}
    if carried is not None:
        aliases = {len(args): 3, len(args) + 1: 5}
        in_specs += [pl.BlockSpec(memory_space=pl.ANY)] * 2
        args += list(carried)
    return pl.pallas_call(
        functools.partial(_in_proj_kernel, n_kv_out=2 if emit_kv else 0),
        grid=(n // tm,),
        in_specs=in_specs,
        out_specs=[o[0] for o in outs],
        out_shape=[o[1] for o in outs],
        input_output_aliases=aliases,
        compiler_params=_params("parallel"),
        name="in_proj",
    )(*args)


SCAN_CHUNKS_PER_STEP = 4
SCAN_STREAMS_PER_STEP = 4


def _cumsum_rows(a):
    rows = lax.broadcasted_iota(jnp.int32, a.shape, 0)
    s = 1
    while s < a.shape[0]:
        a = a + jnp.where(rows >= s, pltpu.roll(a, s, 0), 0.0)
        s *= 2
    return a


def _scan_kernel(*refs, heads, dk, dv, n_chunks, nb, has_la, has_s0):
    it = iter(refs)
    qk_ref, v_ref = next(it), next(it)
    la_ref = next(it)
    s0_ref = next(it) if has_s0 else None
    o_ref, sT_ref, st_ref = next(it), next(it), next(it)
    hd = heads * dk
    step = pl.program_id(1)

    @pl.when(step == 0)
    def _():
        if has_s0:
            st_ref[...] = s0_ref[...]
        else:
            st_ref[...] = jnp.zeros_like(st_ref)

    row = lax.broadcasted_iota(jnp.int32, (CHUNK, CHUNK), 0)
    col = lax.broadcasted_iota(jnp.int32, (CHUNK, CHUNK), 1)
    causal = row >= col
    ksl = lambda h: slice(h * dk, (h + 1) * dk)
    vsl = lambda h: slice(h * dv, (h + 1) * dv)
    chains = [(bi, h) for h in range(heads) for bi in range(nb)]

    def chunk(c, carry):
        rows = pl.ds(pl.multiple_of(c * CHUNK, CHUNK), CHUNK)
        qe, ke, kd, el = [], [], [], []
        for bi in range(nb):
            if has_la:
                bc = _cumsum_rows(la_ref[bi, rows, :])
            else:
                steps = lax.broadcasted_iota(jnp.int32, (CHUNK, hd), 0) + 1
                bc = steps.astype(F32) * la_ref[...]
            bl = bc[CHUNK - 1:CHUNK, :]
            q = qk_ref[bi, rows, :hd]
            k = qk_ref[bi, rows, hd:]
            qe.append((q * jnp.exp(bc)).astype(BF16))
            ke.append((k * jnp.exp(-bc)).astype(BF16))
            kd.append((k * jnp.exp(bl - bc)).astype(BF16))
            el.append(jnp.exp(bl))
        att = [lax.dot_general(qe[bi][:, ksl(h)], ke[bi][:, ksl(h)], _NT, preferred_element_type=F32)
               for bi, h in chains]
        cross = [lax.dot_general(qe[bi][:, ksl(h)], st_ref[bi, h].astype(BF16), _NT, preferred_element_type=F32)
                 for bi, h in chains]
        upd = [lax.dot_general(v_ref[bi, rows, vsl(h)], kd[bi][:, ksl(h)], _TN, preferred_element_type=F32)
               for bi, h in chains]
        for n, (bi, h) in enumerate(chains):
            a = jnp.where(causal, att[n], 0.0).astype(BF16)
            o_ref[bi, rows, vsl(h)] = _dot(a, v_ref[bi, rows, vsl(h)]) + cross[n]
        for n, (bi, h) in enumerate(chains):
            st_ref[bi, h] = st_ref[bi, h] * el[bi][:, ksl(h)] + upd[n]
        return carry

    lax.fori_loop(0, n_chunks, chunk, 0)

    @pl.when(step == pl.num_programs(1) - 1)
    def _():
        sT_ref[...] = st_ref[...]


def _scan(qk, v, la, s0T, *, heads, dk, dv):
    b, t, _ = qk.shape
    has_la = la.ndim == 3
    has_s0 = s0T is not None
    nb = min(SCAN_STREAMS_PER_STEP, b)
    ncs = min(SCAN_CHUNKS_PER_STEP, t // CHUNK)
    rows = ncs * CHUNK
    hd, hv = heads * dk, heads * dv

    def seq_spec(cols):
        return pl.BlockSpec((nb, rows, cols), lambda bi, si: (bi, si, 0))

    st_spec = pl.BlockSpec((nb, heads, dv, dk), lambda bi, si: (bi, 0, 0, 0))
    in_specs = [seq_spec(2 * hd), seq_spec(hv), seq_spec(hd) if has_la else _full_spec(la.shape)]
    args = [qk, v, la]
    if has_s0:
        in_specs.append(st_spec)
        args.append(s0T)
    kern = functools.partial(_scan_kernel, heads=heads, dk=dk, dv=dv, n_chunks=ncs, nb=nb,
                             has_la=has_la, has_s0=has_s0)
    return pl.pallas_call(
        kern,
        grid=(b // nb, t // rows),
        in_specs=in_specs,
        out_specs=[seq_spec(hv), st_spec],
        out_shape=[jax.ShapeDtypeStruct((b, t, hv), F32), jax.ShapeDtypeStruct((b, heads, dv, dk), F32)],
        scratch_shapes=[pltpu.VMEM((nb, heads, dv, dk), F32)],
        compiler_params=_params("parallel", "arbitrary"),
        name="scan_h%d_dk%d" % (heads, dk),
    )(*args)


ATT_TQ = 1024
ATT_TK = 512
ATT_SUB = 512
ATT_LOOKAHEAD = 2


def _attn_kernel(i_ref, j_ref, q_ref, k_ref, v_ref, o_ref, m_ref, acc_ref, *, tq, tk, nk):
    i, j = i_ref[pl.program_id(1)], j_ref[pl.program_id(1)]
    n_sub = tq // ATT_SUB
    q_lo = [(i * tq + r * ATT_SUB) // CHUNK for r in range(n_sub)]
    q_hi = [(i * tq + (r + 1) * ATT_SUB - 1) // CHUNK for r in range(n_sub)]
    k_lo = (j * tk) // CHUNK
    k_hi = (j * tk + tk - 1) // CHUNK
    j_last = jnp.minimum(nk - 1, ((q_hi[-1] + 1) * CHUNK - 1) // tk)

    @pl.when(j == 0)
    def _():
        m_ref[...] = jnp.full_like(m_ref, -jnp.inf)
        acc_ref[...] = jnp.zeros_like(acc_ref)

    def tile(modes):
        vis = {}
        for r, mode in enumerate(modes):
            if mode == "masked":
                row0 = i * tq + r * ATT_SUB
                qc = (row0 + lax.broadcasted_iota(jnp.int32, (ATT_SUB, tk), 0)) >> CHUNK_SHIFT
                kc = (j * tk + lax.broadcasted_iota(jnp.int32, (ATT_SUB, tk), 1)) >> CHUNK_SHIFT
                vis[r] = kc <= qc
        work = [(h, r) for h in range(MLA_HEADS) for r, mode in enumerate(modes) if mode != "skip"]

        def scores(h, r):
            hs = slice(h * MLA_HEAD_PAD, (h + 1) * MLA_HEAD_PAD)
            return lax.dot_general(q_ref[0, r * ATT_SUB:(r + 1) * ATT_SUB, hs], k_ref[0, :, hs], _NT,
                                   preferred_element_type=F32)

        pending = [scores(*w) for w in work[:ATT_LOOKAHEAD]]
        for n, (h, r) in enumerate(work):
            hs = slice(h * MLA_HEAD_PAD, (h + 1) * MLA_HEAD_PAD)
            rs = slice(r * ATT_SUB, (r + 1) * ATT_SUB)
            s = pending.pop(0)
            if n + ATT_LOOKAHEAD < len(work):
                pending.append(scores(*work[n + ATT_LOOKAHEAD]))
            if modes[r] == "masked":
                s = jnp.where(vis[r], s, -jnp.inf)
            m_prev = m_ref[h, rs, :]
            m_new = jnp.maximum(m_prev, jnp.max(s, axis=-1, keepdims=True))
            alpha = jnp.exp2(m_prev - m_new)
            p = jnp.concatenate([jnp.exp2(s[:, c * LANES:(c + 1) * LANES] - m_new).astype(BF16)
                                 for c in range(tk // LANES)], axis=1)
            acc_ref[h, rs, :] = alpha * acc_ref[h, rs, :] + _dot(p, v_ref[0, :, hs])
            m_ref[h, rs, :] = m_new

    def cond(r, mode):
        if mode == "full":
            return k_hi <= q_lo[r]
        if mode == "masked":
            return jnp.logical_and(k_hi > q_lo[r], k_lo <= q_hi[r])
        return k_lo > q_hi[r]

    combos = [("full",) * n_sub] + [("skip",) * r + ("masked",) + ("full",) * (n_sub - r - 1) for r in range(n_sub)]
    for modes in combos:
        pred = functools.reduce(jnp.logical_and, [cond(r, mode) for r, mode in enumerate(modes)])
        pl.when(pred)(functools.partial(tile, modes))

    @pl.when(j == j_last)
    def _():
        for h in range(MLA_HEADS):
            a = acc_ref[h]
            o_ref[0, :, h * MLA_DV:(h + 1) * MLA_DV] = (a[:, :MLA_DV] / a[:, MLA_DV:MLA_DV + 1]).astype(o_ref.dtype)


def _attention(q, k, v, *, tq, tk):
    b, t_q, _ = q.shape
    t_k = k.shape[1]
    assert tq % ATT_SUB == 0 and t_q % tq == 0 and t_k % tk == 0
    nq, nk = t_q // tq, t_k // tk
    width = MLA_HEADS * MLA_HEAD_PAD
    pairs = [(i, j) for i in range(nq)
             for j in range(min(nk - 1, (((i * tq + tq - 1) // CHUNK + 1) * CHUNK - 1) // tk) + 1)]
    i_tab = jnp.asarray([p[0] for p in pairs], jnp.int32)
    j_tab = jnp.asarray([p[1] for p in pairs], jnp.int32)

    kern = functools.partial(_attn_kernel, tq=tq, tk=tk, nk=nk)
    grid_spec = pltpu.PrefetchScalarGridSpec(
        num_scalar_prefetch=2,
        grid=(b, len(pairs)),
        in_specs=[pl.BlockSpec((1, tq, width), lambda bi, p, it, jt: (bi, it[p], 0)),
                  pl.BlockSpec((1, tk, width), lambda bi, p, it, jt: (bi, jt[p], 0)),
                  pl.BlockSpec((1, tk, width), lambda bi, p, it, jt: (bi, jt[p], 0))],
        out_specs=pl.BlockSpec((1, tq, MLA_HEADS * MLA_DV), lambda bi, p, it, jt: (bi, it[p], 0)),
        scratch_shapes=[pltpu.VMEM((MLA_HEADS, tq, LANES), F32), pltpu.VMEM((MLA_HEADS, tq, MLA_HEAD_PAD), F32)],
    )
    return pl.pallas_call(
        kern,
        grid_spec=grid_spec,
        out_shape=jax.ShapeDtypeStruct((b, t_q, MLA_HEADS * MLA_DV), BF16),
        compiler_params=_params("parallel", "arbitrary"),
        name="mla_attention",
    )(i_tab, j_tab, q, k, v)


def _attn_cached_kernel(q_ref, cp_ref, krp_ref, cn_ref, krn_ref, wk_ref, e_ref, wv_ref, o_ref):
    t_new = q_ref.shape[1]
    hsl = lambda h: slice(h * MLA_HEAD_PAD, (h + 1) * MLA_HEAD_PAD)
    ckv_p = cp_ref[...].astype(BF16)
    ckv_n = cn_ref[...].astype(BF16)
    kr_p = _dot(krp_ref[...].astype(BF16), e_ref[:MLA_ROPE, :]).astype(BF16)
    kr_n = _dot(krn_ref[...].astype(BF16), e_ref[...]).astype(BF16)
    q_all = jnp.concatenate([q_ref[0, :, hsl(h)] for h in range(MLA_HEADS)], axis=0)
    q_lat = jnp.concatenate(
        [lax.dot_general(q_ref[0, :, hsl(h)], wk_ref[:, hsl(h)], _NT, preferred_element_type=F32)
         for h in range(MLA_HEADS)], axis=0).astype(BF16)
    s_past = (lax.dot_general(q_lat, ckv_p, _NT, preferred_element_type=F32)
              + lax.dot_general(q_all, kr_p, _NT, preferred_element_type=F32))
    s_new = (lax.dot_general(q_lat, ckv_n, _NT, preferred_element_type=F32)
             + lax.dot_general(q_all, kr_n, _NT, preferred_element_type=F32))
    q_chunk = jnp.concatenate([lax.broadcasted_iota(jnp.int32, (t_new, t_new), 0) >> CHUNK_SHIFT] * MLA_HEADS, axis=0)
    k_chunk = lax.broadcasted_iota(jnp.int32, (MLA_HEADS * t_new, t_new), 1) >> CHUNK_SHIFT
    s_new = jnp.where(k_chunk <= q_chunk, s_new, -jnp.inf)
    m = jnp.maximum(jnp.max(s_past, axis=-1, keepdims=True), jnp.max(s_new, axis=-1, keepdims=True))
    p_past = jnp.exp2(s_past - m)
    p_new = jnp.exp2(s_new - m)
    denom = jnp.sum(p_past, axis=-1, keepdims=True) + jnp.sum(p_new, axis=-1, keepdims=True)
    o_lat = ((_dot(p_past.astype(BF16), ckv_p) + _dot(p_new.astype(BF16), ckv_n)) / denom).astype(BF16)
    for h in range(MLA_HEADS):
        o = _dot(o_lat[h * t_new:(h + 1) * t_new], wv_ref[:, hsl(h)])
        o_ref[0, :, h * MLA_DV:(h + 1) * MLA_DV] = o[:, :MLA_DV].astype(o_ref.dtype)


def _attention_cached(q, cache_ckv, cache_kr, ckv_all, krp, lw, layer):
    b, t, width = q.shape
    t_past = cache_ckv.shape[2]
    weights = [lw["wk"], lw["e_head"], lw["wv"]]
    return pl.pallas_call(
        _attn_cached_kernel,
        grid=(b,),
        in_specs=[pl.BlockSpec((1, t, width), lambda bi: (bi, 0, 0)),
                  pl.BlockSpec((t_past, MLA_KV_LORA), lambda bi: (layer * b + bi, 0)),
                  pl.BlockSpec((t_past, MLA_ROPE), lambda bi: (layer * b + bi, 0)),
                  pl.BlockSpec((t, MLA_KV_LORA), lambda bi: (layer * b + bi, 0)),
                  pl.BlockSpec((t, LANES), lambda bi: (bi, 0))] + [_wspec(p) for p in weights],
        out_specs=pl.BlockSpec((1, t, MLA_HEADS * MLA_DV), lambda bi: (bi, 0, 0)),
        out_shape=jax.ShapeDtypeStruct((b, t, MLA_HEADS * MLA_DV), BF16),
        compiler_params=_params("parallel"),
        name="mla_attention_cached",
    )(q, cache_ckv.reshape(-1, MLA_KV_LORA), cache_kr.reshape(-1, MLA_ROPE), ckv_all.reshape(-1, MLA_KV_LORA),
      krp, *[p[0] for p in weights])


OUT_TM = 512
OUT_SUB = 256


def _out_proj_kernel(x_ref, or_ref, om_ref, og_ref, wrg_ref, wg3_ref, wro_ref, wmo_ref, wgo_ref, wout_ref,
                     rgn_ref, ggn_ref, g_ref, b_ref, o_ref, hr_ref, hg_ref):
    tm = x_ref.shape[0]
    sub = min(OUT_SUB, tm)
    for r in range(tm // sub):
        rs = slice(r * sub, (r + 1) * sub)
        x = x_ref[rs, :]
        xb = x.astype(BF16)

        ret_gate = _dot(xb, wrg_ref[...])
        gla_gate = [_dot(xb, wg3_ref[:, _G_GOG + h * GLA_DV:_G_GOG + (h + 1) * GLA_DV]) for h in range(GLA_HEADS)]

        for h in range(RET_HEADS):
            sl = slice(h * RET_DV, (h + 1) * RET_DV)
            gate = ret_gate[:, sl]
            o = or_ref[rs, sl]
            oc = o - jnp.mean(o, axis=-1, keepdims=True)
            normed = oc * lax.rsqrt(jnp.mean(oc * oc, axis=-1, keepdims=True) + EPS) * rgn_ref[:, sl]
            hr_ref[rs, sl] = (normed * (gate * jax.nn.sigmoid(gate))).astype(BF16)
        y_m = _dot(om_ref[rs, :], wmo_ref[...])
        gate_m = _dot(xb, wg3_ref[:, _G_BR + D_MODEL:_G_BR + 2 * D_MODEL])

        for h in range(GLA_HEADS):
            sl = slice(h * GLA_DV, (h + 1) * GLA_DV)
            gate = gla_gate[h]
            o = og_ref[rs, sl]
            normed = o * lax.rsqrt(jnp.mean(o * o, axis=-1, keepdims=True) + EPS) * ggn_ref[:, sl]
            hg_ref[rs, sl] = (normed * (gate * jax.nn.sigmoid(gate))).astype(BF16)
        y_r = _dot(hr_ref[rs, :], wro_ref[...])
        gate_r = _dot(xb, wg3_ref[:, _G_BR:_G_BR + D_MODEL])
        mix = jax.nn.sigmoid(gate_m) * y_m + jax.nn.sigmoid(gate_r) * y_r
        y_g = _dot(hg_ref[rs, :], wgo_ref[...])
        gate_g = _dot(xb, wg3_ref[:, _G_BR + 2 * D_MODEL:_G_BR + 3 * D_MODEL])
        mix = mix + jax.nn.sigmoid(gate_g) * y_g
        y = _dot(mix.astype(BF16), wout_ref[...])
        o_ref[rs, :] = _layer_norm(ALPHA * x + y, g_ref[...], b_ref[...])


def _out_proj_ln(x, o_r, o_m, o_g, lw, g, b):
    n = x.shape[0]
    tm = min(OUT_TM, n)
    weights = [lw["wrg"], lw["wg3"], lw["w_ret_o"], lw["w_mla_o"], lw["w_gla_o"], lw["w_out"],
               lw["ret_gn"], lw["gla_gn"], g, b]
    return pl.pallas_call(
        _out_proj_kernel,
        grid=(n // tm,),
        in_specs=[_row_spec(tm, D_MODEL), _row_spec(tm, RET_HEADS * RET_DV), _row_spec(tm, MLA_HEADS * MLA_DV),
                  _row_spec(tm, GLA_HEADS * GLA_DV)] + [_wspec(p) for p in weights],
        out_specs=_row_spec(tm, D_MODEL),
        out_shape=jax.ShapeDtypeStruct((n, D_MODEL), F32),
        scratch_shapes=[pltpu.VMEM((tm, RET_HEADS * RET_DV), BF16), pltpu.VMEM((tm, GLA_HEADS * GLA_DV), BF16)],
        compiler_params=_params("parallel"),
        name="out_proj_ln",
    )(x, o_r, o_m, o_g, *[p[0] for p in weights])


def _prep_weights(w):
    offs = np.cumsum((0,) + IN_SPLITS)
    w_in = w["w_in"]
    (r_q, r_k, r_v, r_g, m_cq, m_ckv, m_kr, g_q, g_k, g_v, g_lr, g_og, br) = [
        w_in[:, :, offs[i]:offs[i + 1]] for i in range(len(IN_SPLITS))]

    def pad_last(a, n):
        return jnp.pad(a, [(0, 0)] * (a.ndim - 1) + [(0, n - a.shape[-1])])

    w1 = jnp.concatenate([r_q, r_k, r_v, m_cq, pad_last(m_kr, LANES), m_ckv, g_q, g_k, g_v,
                          pad_last(g_lr, LANES)], axis=2).astype(BF16)
    dq = MLA_NOPE + MLA_ROPE
    width = MLA_HEADS * MLA_HEAD_PAD
    wq = pad_last(w["mla_w_q_up"].reshape(DEPTH, MLA_Q_LORA, MLA_HEADS, dq), MLA_HEAD_PAD).reshape(
        DEPTH, MLA_Q_LORA, width).astype(BF16)
    kv = w["mla_w_kv_up"].reshape(DEPTH, MLA_KV_LORA, MLA_HEADS, MLA_NOPE + MLA_DV)
    wk = pad_last(kv[..., :MLA_NOPE], MLA_HEAD_PAD).reshape(DEPTH, MLA_KV_LORA, width).astype(BF16)
    wv = pad_last(kv[..., MLA_NOPE:], MLA_HEAD_PAD).reshape(DEPTH, MLA_KV_LORA, width).astype(BF16)
    place = np.zeros((1, LANES, MLA_HEAD_PAD), np.float32)
    place[0, np.arange(MLA_ROPE), MLA_NOPE + np.arange(MLA_ROPE)] = 1.0
    row = lambda a: a.reshape(a.shape[0], 1, a.shape[-1])
    return {
        "w1": w1, "wq": wq, "wk": wk, "wv": wv, "wkv": w["mla_w_kv_up"].astype(BF16), "e_head": jnp.asarray(place, BF16),
        "wgate": jnp.pad(w["gla_w_gate_up"], ((0, 0), (0, LANES - GLA_GATE_RANK), (0, 0))).astype(BF16),
        "bgate": row(w["gla_b_gate"]), "qn": row(w["mla_q_norm_g"]), "kvn": row(w["mla_kv_norm_g"]),
        "wrg": r_g.astype(BF16), "wg3": w_in[:, :, offs[11]:offs[13]].astype(BF16),
        "w_ret_o": w["w_ret_o"].astype(BF16), "w_mla_o": w["w_mla_o"].astype(BF16),
        "w_gla_o": w["w_gla_o"].astype(BF16), "w_out": w["w_out"].astype(BF16),
        "ret_gn": row(w["ret_gn_g"]), "gla_gn": row(w["gla_gn_g"]),
        "f1u": w["ffn1_up"].astype(BF16), "f1d": w["ffn1_down"].astype(BF16),
        "f2u": w["ffn2_up"].astype(BF16), "f2d": w["ffn2_down"].astype(BF16),
        "ln_g": w["ln_g"].reshape(DEPTH * 3, 1, D_MODEL), "ln_b": w["ln_b"].reshape(DEPTH * 3, 1, D_MODEL),
    }


def _layer_params(sw, layer):
    lw = {k: (v, layer if v.shape[0] == DEPTH else 0) for k, v in sw.items() if not k.startswith("ln_")}
    for k in range(3):
        lw["ln_g%d" % k] = (sw["ln_g"], layer * 3 + k)
        lw["ln_b%d" % k] = (sw["ln_b"], layer * 3 + k)
    return lw


def _rope_tables(pos, tm):
    def cos_sin(half):
        inv = ROPE_THETA ** (-jnp.arange(half, dtype=F32) / half)
        ang = pos.astype(F32)[:, None] * inv[None, :]
        return jnp.cos(ang), jnp.sin(ang)

    t = pos.shape[0]
    c32, s32 = cos_sin(RET_DK // 2)
    c16, s16 = cos_sin(MLA_ROPE // 2)
    cr_h = jnp.tile(jnp.concatenate([c32, c32], axis=1), (1, RET_HEADS))
    sr_h = jnp.tile(jnp.concatenate([-s32, s32], axis=1), (1, RET_HEADS))
    k_scale = RET_DK ** -0.5
    q_scale = (MLA_NOPE + MLA_ROPE) ** -0.5 * float(np.log2(np.e))
    zeros = lambda n: jnp.zeros((t, n), F32)
    tabs = {
        "cr": jnp.concatenate([cr_h, cr_h * k_scale], axis=1),
        "sr": jnp.concatenate([sr_h, sr_h * k_scale], axis=1),
        "cq": jnp.concatenate([jnp.ones((t, MLA_NOPE), F32), c16, c16, zeros(LANES - MLA_NOPE - MLA_ROPE)], axis=1) * q_scale,
        "sq": jnp.concatenate([zeros(MLA_NOPE), -s16, s16, zeros(LANES - MLA_NOPE - MLA_ROPE)], axis=1) * q_scale,
        "ck": jnp.concatenate([c16, c16, zeros(LANES - MLA_ROPE)], axis=1),
        "sk": jnp.concatenate([-s16, s16, zeros(LANES - MLA_ROPE)], axis=1),
    }
    if t < tm:
        tabs = {k: jnp.tile(v, (tm // t, 1)) for k, v in tabs.items()}
    return tabs


def _group_layer(x, b, t, lw, tabs, past, layer, carried):
    n = b * t
    x = _ffn_ln(x, lw["f1u"], lw["f1d"], lw["ln_g0"], lw["ln_b0"])
    rqk, rv, qm, ckv_all, krp, kr_all, gqk, gv, la, *kv = _in_proj(x, lw, tabs, layer, carried, past is None)

    log_gamma = jnp.log(1.0 - 2.0 ** (-5.0 - jnp.arange(RET_HEADS, dtype=F32)))
    ret_la = jnp.repeat(log_gamma, RET_DK)[None, :]
    s_ret0 = None if past is None else jnp.swapaxes(past[2], -1, -2)
    s_gla0 = None if past is None else jnp.swapaxes(past[3], -1, -2)
    o_r, s_retT = _scan(rqk.reshape(b, t, -1), rv.reshape(b, t, -1), ret_la, s_ret0,
                        heads=RET_HEADS, dk=RET_DK, dv=RET_DV)
    o_g, s_glaT = _scan(gqk.reshape(b, t, -1), gv.reshape(b, t, -1), la.reshape(b, t, -1), s_gla0,
                        heads=GLA_HEADS, dk=GLA_DK, dv=GLA_DV)

    qm = qm.reshape(b, t, -1)
    if past is None:
        o_m = _attention(qm, kv[0].reshape(b, t, -1), kv[1].reshape(b, t, -1), tq=min(ATT_TQ, t), tk=min(ATT_TK, t))
    else:
        o_m = _attention_cached(qm, past[0], past[1], ckv_all, krp, lw, layer)

    x = _out_proj_ln(x, o_r.reshape(n, -1), o_m.reshape(n, -1), o_g.reshape(n, -1), lw, lw["ln_g1"], lw["ln_b1"])
    x = _ffn_ln(x, lw["f2u"], lw["f2d"], lw["ln_g2"], lw["ln_b2"])
    return x, (ckv_all, kr_all), (jnp.swapaxes(s_retT, -1, -2), jnp.swapaxes(s_glaT, -1, -2))


def kernel(x_prompt, x_sample, cache_mla_ckv, cache_mla_krope, state_ret, state_gla, w_in, ret_gn_g, mla_q_norm_g, mla_w_q_up, mla_kv_norm_g, mla_w_kv_up, gla_w_gate_up, gla_b_gate, gla_gn_g, w_ret_o, w_mla_o, w_gla_o, w_out, ffn1_up, ffn1_down, ffn2_up, ffn2_down, ln_g, ln_b):
    w = dict(w_in=w_in, ret_gn_g=ret_gn_g, mla_q_norm_g=mla_q_norm_g, mla_w_q_up=mla_w_q_up,
             mla_kv_norm_g=mla_kv_norm_g, mla_w_kv_up=mla_w_kv_up, gla_w_gate_up=gla_w_gate_up,
             gla_b_gate=gla_b_gate, gla_gn_g=gla_gn_g, w_ret_o=w_ret_o, w_mla_o=w_mla_o, w_gla_o=w_gla_o,
             w_out=w_out, ffn1_up=ffn1_up, ffn1_down=ffn1_down, ffn2_up=ffn2_up, ffn2_down=ffn2_down,
             ln_g=ln_g, ln_b=ln_b)
    bp, tp, _ = x_prompt.shape
    bs, ts, _ = x_sample.shape
    t_past = cache_mla_ckv.shape[2]
    assert t_past % CHUNK == 0 and tp % CHUNK == 0 and ts % CHUNK == 0
    tabs_p = _rope_tables(jnp.arange(tp), min(INP_TM, bp * tp))
    tabs_s = _rope_tables(t_past + jnp.arange(ts), min(INP_TM, bs * ts))
    xp = x_prompt.reshape(bp * tp, D_MODEL)
    xs = x_sample.reshape(bs * ts, D_MODEL)
    carried_p = carried_s = None
    st_p, st_s = [], []
    sw = _prep_weights(w)
    for l in range(DEPTH):
        lw = _layer_params(sw, l)
        xp, carried_p, st = _group_layer(xp, bp, tp, lw, tabs_p, None, l, carried_p)
        st_p.append(st)
        past = (cache_mla_ckv, cache_mla_krope, state_ret[l], state_gla[l])
        xs, carried_s, st = _group_layer(xs, bs, ts, lw, tabs_s, past, l, carried_s)
        st_s.append(st)
    stack = lambda sts, i: jnp.stack([s[i] for s in sts])
    return (xp.reshape(bp, tp, D_MODEL), xs.reshape(bs, ts, D_MODEL),
            carried_p[0].reshape(DEPTH, bp, tp, -1), carried_p[1].reshape(DEPTH, bp, tp, -1),
            stack(st_p, 0), stack(st_p, 1),
            carried_s[0].reshape(DEPTH, bs, ts, -1), carried_s[1].reshape(DEPTH, bs, ts, -1),
            stack(st_s, 0), stack(st_s, 1))
```

```python
import functools

import numpy as np
import jax
import jax.numpy as jnp
from jax import lax
from jax.experimental import pallas as pl
from jax.experimental.pallas import tpu as pltpu

F32 = jnp.float32
BF16 = jnp.bfloat16

D_MODEL = 1024
DEPTH = 2
CHUNK = 64
CHUNK_SHIFT = 6
ALPHA = (2 * DEPTH) ** 0.25
EPS = 1e-5
ROPE_THETA = 10000.0
RET_HEADS, RET_DK, RET_DV = 4, 64, 128
MLA_HEADS, MLA_Q_LORA, MLA_KV_LORA, MLA_NOPE, MLA_ROPE, MLA_DV = 8, 384, 256, 64, 32, 64
GLA_HEADS, GLA_DK, GLA_DV, GLA_GATE_RANK, GLA_TAU = 4, 128, 256, 16, 16.0
D_FF = 2816
N_BRANCH = 3
IN_SPLITS = (RET_HEADS * RET_DK, RET_HEADS * RET_DK, RET_HEADS * RET_DV, RET_HEADS * RET_DV,
             MLA_Q_LORA, MLA_KV_LORA, MLA_ROPE,
             GLA_HEADS * GLA_DK, GLA_HEADS * GLA_DK, GLA_HEADS * GLA_DV, GLA_GATE_RANK, GLA_HEADS * GLA_DV,
             N_BRANCH * D_MODEL)

LANES = 128
MLA_HEAD_PAD = LANES
VMEM_LIMIT = 56 * 1024 * 1024

_C_RQK, _C_RV, _C_CQ, _C_KR, _C_CKV, _C_GQK, _C_GV, _C_LR, _C_END = (
    0, 512, 1024, 1408, 1536, 1792, 2816, 3840, 3968)
_G_GOG, _G_BR = 0, 1024

_NT = (((1,), (1,)), ((), ()))
_TN = (((0,), (0,)), ((), ()))


def _params(*sem):
    return pltpu.CompilerParams(dimension_semantics=sem, vmem_limit_bytes=VMEM_LIMIT)


def _dot(a, b):
    return jnp.dot(a, b, preferred_element_type=F32)


def _layer_norm(z, g, b):
    mu = jnp.mean(z, axis=-1, keepdims=True)
    zc = z - mu
    var = jnp.mean(zc * zc, axis=-1, keepdims=True)
    return zc * lax.rsqrt(var + EPS) * g + b


def _row_spec(tm, cols):
    return pl.BlockSpec((tm, cols), lambda i: (i, 0))


def _full_spec(shape):
    return pl.BlockSpec(shape, lambda *_: (0,) * len(shape), pipeline_mode=pl.Buffered(1))


def _wspec(param):
    arr, idx = param
    return pl.BlockSpec((None,) + arr.shape[1:], lambda *_: (idx,) + (0,) * (arr.ndim - 1), pipeline_mode=pl.Buffered(1))


FFN_TM = 512
FFN_FC = 256


def _ffn_ln_kernel(x_ref, wup_ref, wd_ref, g_ref, b_ref, o_ref, h_ref):
    x = x_ref[...]
    xb = x.astype(BF16)
    for c in range(D_FF // FFN_FC):
        sl = slice(c * FFN_FC, (c + 1) * FFN_FC)
        gate = _dot(xb, wup_ref[:, sl])
        up = _dot(xb, wup_ref[:, D_FF + c * FFN_FC:D_FF + (c + 1) * FFN_FC])
        h_ref[:, sl] = (gate * jax.nn.sigmoid(gate) * up).astype(BF16)
    y = _dot(h_ref[...], wd_ref[...])
    o_ref[...] = _layer_norm(ALPHA * x + 0.5 * y, g_ref[...], b_ref[...])


def _ffn_ln(x, wup, wd, g, b):
    n = x.shape[0]
    tm = min(FFN_TM, n)
    params = [wup, wd, g, b]
    return pl.pallas_call(
        _ffn_ln_kernel,
        grid=(n // tm,),
        in_specs=[_row_spec(tm, D_MODEL)] + [_wspec(p) for p in params],
        out_specs=_row_spec(tm, D_MODEL),
        out_shape=jax.ShapeDtypeStruct((n, D_MODEL), F32),
        scratch_shapes=[pltpu.VMEM((tm, D_FF), BF16)],
        compiler_params=_params("parallel"),
        name="ffn_ln",
    )(x, *[p[0] for p in params])


INP_TM = 512


def _swap_halves(x, first_mask, half):
    return jnp.where(first_mask, pltpu.roll(x, LANES - half, 1), pltpu.roll(x, half, 1))


def _in_proj_kernel(x_ref, w_ref, wq_ref, wgate_ref, bgate_ref, qn_ref, kvn_ref,
                    cr_ref, sr_ref, cq_ref, sq_ref, ck_ref, sk_ref, *rest, n_kv_out):
    rqk_ref, rv_ref, qm_ref, ckv_ref, kr_ref, kr_out_ref, gqk_ref, gv_ref, la_ref = rest[-9 - n_kv_out:][:9]
    xb = x_ref[...].astype(BF16)
    tm = xb.shape[0]
    lane = lax.broadcasted_iota(jnp.int32, (tm, LANES), 1)
    ghd = GLA_HEADS * GLA_DK

    h_rqk = _dot(xb, w_ref[:, _C_RQK:_C_RV])
    h_cq_kr = _dot(xb, w_ref[:, _C_CQ:_C_CKV])

    ret_first = (lane & (RET_DK - 1)) < RET_DK // 2
    for c in range(2 * RET_HEADS * RET_DK // LANES):
        sl = slice(c * LANES, (c + 1) * LANES)
        h = h_rqk[:, sl]
        rqk_ref[:, sl] = h * cr_ref[:, sl] + _swap_halves(h, ret_first, RET_DK // 2) * sr_ref[:, sl]
    rv_ref[...] = _dot(xb, w_ref[:, _C_RV:_C_CQ]).astype(BF16)

    hq = h_cq_kr[:, :MLA_Q_LORA]
    cq = hq * lax.rsqrt(jnp.mean(hq * hq, axis=-1, keepdims=True) + EPS) * qn_ref[...]
    q_up = _dot(cq.astype(BF16), wq_ref[...])
    hkv = _dot(xb, w_ref[:, _C_CKV:_C_GQK])
    gv_ref[...] = _dot(xb, w_ref[:, _C_GV:_C_LR]).astype(BF16)
    q_first = lane < MLA_NOPE + MLA_ROPE // 2
    for h_i in range(MLA_HEADS):
        sl = slice(h_i * MLA_HEAD_PAD, (h_i + 1) * MLA_HEAD_PAD)
        qh = q_up[:, sl]
        qm_ref[:, sl] = (qh * cq_ref[...] + _swap_halves(qh, q_first, MLA_ROPE // 2) * sq_ref[...]).astype(BF16)

    hkr = h_cq_kr[:, MLA_Q_LORA:]
    kr = hkr * ck_ref[...] + _swap_halves(hkr, lane < MLA_ROPE // 2, MLA_ROPE // 2) * sk_ref[...]
    kr_ref[...] = kr
    kr_out_ref[...] = kr[:, :MLA_ROPE]
    ckv = hkv * lax.rsqrt(jnp.mean(hkv * hkv, axis=-1, keepdims=True) + EPS) * kvn_ref[...]
    ckv_ref[...] = ckv
    lr = _dot(xb, w_ref[:, _C_LR:_C_END]).astype(BF16)
    gla_q = _dot(xb, w_ref[:, _C_GQK:_C_GQK + ghd])
    if n_kv_out:
        wkv_ref = rest[0]
        k_ref, v_ref = rest[-2:]
        kv = _dot(ckv.astype(BF16), wkv_ref[...])
        kr_at_rope = pltpu.roll(kr, MLA_NOPE, 1)
        ones_cols = jnp.where(lane >= MLA_DV, 1.0, 0.0)
        for h_i in range(MLA_HEADS):
            sl = slice(h_i * MLA_HEAD_PAD, (h_i + 1) * MLA_HEAD_PAD)
            g = kv[:, sl]
            k_ref[:, sl] = jnp.where(lane < MLA_NOPE, g, kr_at_rope).astype(BF16)
            v_ref[:, sl] = jnp.where(lane < MLA_DV, pltpu.roll(g, LANES - MLA_NOPE, 1), ones_cols).astype(BF16)

    gqk_ref[:, :ghd] = gla_q * (GLA_DK ** -0.5)
    logit = _dot(lr, wgate_ref[...]) + bgate_ref[...]
    gqk_ref[:, ghd:] = _dot(xb, w_ref[:, _C_GQK + ghd:_C_GV])
    log_sig = jnp.minimum(logit, 0.0) - jnp.log1p(jnp.exp(-jnp.abs(logit)))
    la_ref[...] = log_sig / GLA_TAU


def _in_proj(x, lw, tabs, layer, carried, emit_kv):
    n = x.shape[0]
    tm = min(INP_TM, n)
    period = tabs["cr"].shape[0] // tm

    def tab_spec(cols):
        return pl.BlockSpec((tm, cols), lambda i: (i % period, 0))

    def layer_spec(cols):
        return pl.BlockSpec((None, tm, cols), lambda i: (layer, i, 0))

    rows = lambda cols, dtype: (_row_spec(tm, cols), jax.ShapeDtypeStruct((n, cols), dtype))
    stacked = lambda cols: (layer_spec(cols), jax.ShapeDtypeStruct((DEPTH, n, cols), F32))
    outs = [rows(2 * RET_HEADS * RET_DK, F32), rows(RET_HEADS * RET_DV, BF16), rows(MLA_HEADS * MLA_HEAD_PAD, BF16),
            stacked(MLA_KV_LORA), rows(LANES, F32), stacked(MLA_ROPE), rows(2 * GLA_HEADS * GLA_DK, F32),
            rows(GLA_HEADS * GLA_DV, BF16), rows(GLA_HEADS * GLA_DK, F32)]
    weights = [lw["w1"], lw["wq"], lw["wgate"], lw["bgate"], lw["qn"], lw["kvn"]]
    tables = [tabs["cr"], tabs["sr"], tabs["cq"], tabs["sq"], tabs["ck"], tabs["sk"]]
    in_specs = ([_row_spec(tm, D_MODEL)] + [_wspec(p) for p in weights]
                + [tab_spec(t.shape[1]) for t in tables])
    args = [x, *[p[0] for p in weights], *tables]
    if emit_kv:
        kv_weights = [lw["wkv"]]
        in_specs += [_wspec(p) for p in kv_weights]
        args += [p[0] for p in kv_weights]
        outs += [rows(MLA_HEADS * MLA_HEAD_PAD, BF16)] * 2
    aliases = {}
    if carried is not None:
        aliases = {len(args): 3, len(args) + 1: 5}
        in_specs += [pl.BlockSpec(memory_space=pl.ANY)] * 2
        args += list(carried)
    return pl.pallas_call(
        functools.partial(_in_proj_kernel, n_kv_out=2 if emit_kv else 0),
        grid=(n // tm,),
        in_specs=in_specs,
        out_specs=[o[0] for o in outs],
        out_shape=[o[1] for o in outs],
        input_output_aliases=aliases,
        compiler_params=_params("parallel"),
        name="in_proj",
    )(*args)


SCAN_CHUNKS_PER_STEP = 4
SCAN_STREAMS_PER_STEP = 4


def _cumsum_rows(a):
    rows = lax.broadcasted_iota(jnp.int32, a.shape, 0)
    s = 1
    while s < a.shape[0]:
        a = a + jnp.where(rows >= s, pltpu.roll(a, s, 0), 0.0)
        s *= 2
    return a


def _scan_kernel(*refs, heads, dk, dv, n_chunks, nb, has_la, has_s0):
    it = iter(refs)
    qk_ref, v_ref = next(it), next(it)
    la_ref = next(it)
    s0_ref = next(it) if has_s0 else None
    o_ref, sT_ref, st_ref = next(it), next(it), next(it)
    hd = heads * dk
    step = pl.program_id(1)

    @pl.when(step == 0)
    def _():
        if has_s0:
            st_ref[...] = s0_ref[...]
        else:
            st_ref[...] = jnp.zeros_like(st_ref)

    row = lax.broadcasted_iota(jnp.int32, (CHUNK, CHUNK), 0)
    col = lax.broadcasted_iota(jnp.int32, (CHUNK, CHUNK), 1)
    causal = row >= col
    ksl = lambda h: slice(h * dk, (h + 1) * dk)
    vsl = lambda h: slice(h * dv, (h + 1) * dv)

    def decayed_operands(bi, rows):
        if has_la:
            bc = _cumsum_rows(la_ref[bi, rows, :])
        else:
            steps = lax.broadcasted_iota(jnp.int32, (CHUNK, hd), 0) + 1
            bc = steps.astype(F32) * la_ref[...]
        bl = bc[CHUNK - 1:CHUNK, :]
        q = qk_ref[bi, rows, :hd]
        k = qk_ref[bi, rows, hd:]
        return ((q * jnp.exp(bc)).astype(BF16), (k * jnp.exp(-bc)).astype(BF16),
                (k * jnp.exp(bl - bc)).astype(BF16), jnp.exp(bl))

    def advance(chains, ops, rows):
        att = [lax.dot_general(ops[bi][0][:, ksl(h)], ops[bi][1][:, ksl(h)], _NT, preferred_element_type=F32)
               for bi, h in chains]
        cross = [lax.dot_general(ops[bi][0][:, ksl(h)], st_ref[bi, h].astype(BF16), _NT, preferred_element_type=F32)
                 for bi, h in chains]
        upd = [lax.dot_general(v_ref[bi, rows, vsl(h)], ops[bi][2][:, ksl(h)], _TN, preferred_element_type=F32)
               for bi, h in chains]
        for n, (bi, h) in enumerate(chains):
            a = jnp.where(causal, att[n], 0.0).astype(BF16)
            o_ref[bi, rows, vsl(h)] = _dot(a, v_ref[bi, rows, vsl(h)]) + cross[n]
        for n, (bi, h) in enumerate(chains):
            st_ref[bi, h] = st_ref[bi, h] * ops[bi][3][:, ksl(h)] + upd[n]

    def chunk(c, carry):
        rows = pl.ds(pl.multiple_of(c * CHUNK, CHUNK), CHUNK)
        if has_la:
            ops = {0: decayed_operands(0, rows)}
            for bi in range(nb):
                if bi + 1 < nb:
                    ops[bi + 1] = decayed_operands(bi + 1, rows)
                advance([(bi, h) for h in range(heads)], ops, rows)
        else:
            ops = {bi: decayed_operands(bi, rows) for bi in range(nb)}
            advance([(bi, h) for h in range(heads) for bi in range(nb)], ops, rows)
        return carry

    lax.fori_loop(0, n_chunks, chunk, 0)

    @pl.when(step == pl.num_programs(1) - 1)
    def _():
        sT_ref[...] = st_ref[...]


def _scan(qk, v, la, s0T, *, heads, dk, dv):
    b, t, _ = qk.shape
    has_la = la.ndim == 3
    has_s0 = s0T is not None
    nb = min(SCAN_STREAMS_PER_STEP, b)
    ncs = min(SCAN_CHUNKS_PER_STEP, t // CHUNK)
    rows = ncs * CHUNK
    hd, hv = heads * dk, heads * dv

    def seq_spec(cols):
        return pl.BlockSpec((nb, rows, cols), lambda bi, si: (bi, si, 0))

    st_spec = pl.BlockSpec((nb, heads, dv, dk), lambda bi, si: (bi, 0, 0, 0))
    in_specs = [seq_spec(2 * hd), seq_spec(hv), seq_spec(hd) if has_la else _full_spec(la.shape)]
    args = [qk, v, la]
    if has_s0:
        in_specs.append(st_spec)
        args.append(s0T)
    kern = functools.partial(_scan_kernel, heads=heads, dk=dk, dv=dv, n_chunks=ncs, nb=nb,
                             has_la=has_la, has_s0=has_s0)
    return pl.pallas_call(
        kern,
        grid=(b // nb, t // rows),
        in_specs=in_specs,
        out_specs=[seq_spec(hv), st_spec],
        out_shape=[jax.ShapeDtypeStruct((b, t, hv), F32), jax.ShapeDtypeStruct((b, heads, dv, dk), F32)],
        scratch_shapes=[pltpu.VMEM((nb, heads, dv, dk), F32)],
        compiler_params=_params("parallel", "arbitrary"),
        name="scan_h%d_dk%d" % (heads, dk),
    )(*args)


ATT_TQ = 1024
ATT_TK = 512
ATT_SUB = 512
ATT_LOOKAHEAD = 2


def _attn_kernel(i_ref, j_ref, q_ref, k_ref, v_ref, o_ref, m_ref, acc_ref, *, tq, tk, nk):
    i, j = i_ref[pl.program_id(1)], j_ref[pl.program_id(1)]
    n_sub = tq // ATT_SUB
    q_lo = [(i * tq + r * ATT_SUB) // CHUNK for r in range(n_sub)]
    q_hi = [(i * tq + (r + 1) * ATT_SUB - 1) // CHUNK for r in range(n_sub)]
    k_lo = (j * tk) // CHUNK
    k_hi = (j * tk + tk - 1) // CHUNK
    j_last = jnp.minimum(nk - 1, ((q_hi[-1] + 1) * CHUNK - 1) // tk)

    @pl.when(j == 0)
    def _():
        m_ref[...] = jnp.full_like(m_ref, -jnp.inf)
        acc_ref[...] = jnp.zeros_like(acc_ref)

    def tile(modes):
        vis = {}
        for r, mode in enumerate(modes):
            if mode == "masked":
                row0 = i * tq + r * ATT_SUB
                qc = (row0 + lax.broadcasted_iota(jnp.int32, (ATT_SUB, tk), 0)) >> CHUNK_SHIFT
                kc = (j * tk + lax.broadcasted_iota(jnp.int32, (ATT_SUB, tk), 1)) >> CHUNK_SHIFT
                vis[r] = kc <= qc
        work = [(h, r) for h in range(MLA_HEADS) for r, mode in enumerate(modes) if mode != "skip"]

        def scores(h, r):
            hs = slice(h * MLA_HEAD_PAD, (h + 1) * MLA_HEAD_PAD)
            return lax.dot_general(q_ref[0, r * ATT_SUB:(r + 1) * ATT_SUB, hs], k_ref[0, :, hs], _NT,
                                   preferred_element_type=F32)

        pending = [scores(*w) for w in work[:ATT_LOOKAHEAD]]
        for n, (h, r) in enumerate(work):
            hs = slice(h * MLA_HEAD_PAD, (h + 1) * MLA_HEAD_PAD)
            rs = slice(r * ATT_SUB, (r + 1) * ATT_SUB)
            s = pending.pop(0)
            if n + ATT_LOOKAHEAD < len(work):
                pending.append(scores(*work[n + ATT_LOOKAHEAD]))
            if modes[r] == "masked":
                s = jnp.where(vis[r], s, -jnp.inf)
            m_prev = m_ref[h, rs, :]
            m_new = jnp.maximum(m_prev, jnp.max(s, axis=-1, keepdims=True))
            alpha = jnp.exp2(m_prev - m_new)
            p = jnp.concatenate([jnp.exp2(s[:, c * LANES:(c + 1) * LANES] - m_new).astype(BF16)
                                 for c in range(tk // LANES)], axis=1)
            acc_ref[h, rs, :] = alpha * acc_ref[h, rs, :] + _dot(p, v_ref[0, :, hs])
            m_ref[h, rs, :] = m_new

    def cond(r, mode):
        if mode == "full":
            return k_hi <= q_lo[r]
        if mode == "masked":
            return jnp.logical_and(k_hi > q_lo[r], k_lo <= q_hi[r])
        return k_lo > q_hi[r]

    combos = [("full",) * n_sub] + [("skip",) * r + ("masked",) + ("full",) * (n_sub - r - 1) for r in range(n_sub)]
    for modes in combos:
        pred = functools.reduce(jnp.logical_and, [cond(r, mode) for r, mode in enumerate(modes)])
        pl.when(pred)(functools.partial(tile, modes))

    @pl.when(j == j_last)
    def _():
        first = lax.broadcasted_iota(jnp.int32, (tq, LANES), 1) < MLA_DV
        for pair in range(MLA_HEADS // 2):
            a0, a1 = acc_ref[2 * pair], acc_ref[2 * pair + 1]
            even = a0 / pltpu.roll(a0, MLA_DV, 1)
            odd = pltpu.roll(a1, MLA_DV, 1) / a1
            o_ref[0, :, pair * LANES:(pair + 1) * LANES] = jnp.where(first, even, odd).astype(o_ref.dtype)


def _attention(q, k, v, *, tq, tk):
    b, t_q, _ = q.shape
    t_k = k.shape[1]
    assert tq % ATT_SUB == 0 and t_q % tq == 0 and t_k % tk == 0
    nq, nk = t_q // tq, t_k // tk
    width = MLA_HEADS * MLA_HEAD_PAD
    pairs = [(i, j) for i in range(nq)
             for j in range(min(nk - 1, (((i * tq + tq - 1) // CHUNK + 1) * CHUNK - 1) // tk) + 1)]
    i_tab = jnp.asarray([p[0] for p in pairs], jnp.int32)
    j_tab = jnp.asarray([p[1] for p in pairs], jnp.int32)

    kern = functools.partial(_attn_kernel, tq=tq, tk=tk, nk=nk)
    grid_spec = pltpu.PrefetchScalarGridSpec(
        num_scalar_prefetch=2,
        grid=(b, len(pairs)),
        in_specs=[pl.BlockSpec((1, tq, width), lambda bi, p, it, jt: (bi, it[p], 0)),
                  pl.BlockSpec((1, tk, width), lambda bi, p, it, jt: (bi, jt[p], 0)),
                  pl.BlockSpec((1, tk, width), lambda bi, p, it, jt: (bi, jt[p], 0))],
        out_specs=pl.BlockSpec((1, tq, MLA_HEADS * MLA_DV), lambda bi, p, it, jt: (bi, it[p], 0)),
        scratch_shapes=[pltpu.VMEM((MLA_HEADS, tq, LANES), F32), pltpu.VMEM((MLA_HEADS, tq, MLA_HEAD_PAD), F32)],
    )
    return pl.pallas_call(
        kern,
        grid_spec=grid_spec,
        out_shape=jax.ShapeDtypeStruct((b, t_q, MLA_HEADS * MLA_DV), BF16),
        compiler_params=_params("parallel", "arbitrary"),
        name="mla_attention",
    )(i_tab, j_tab, q, k, v)


def _attn_cached_kernel(q_ref, cp_ref, krp_ref, cn_ref, krn_ref, wk_ref, e_ref, wv_ref, o_ref):
    t_new = q_ref.shape[1]
    hsl = lambda h: slice(h * MLA_HEAD_PAD, (h + 1) * MLA_HEAD_PAD)
    ckv_p = cp_ref[...].astype(BF16)
    ckv_n = cn_ref[...].astype(BF16)
    kr_p = _dot(krp_ref[...].astype(BF16), e_ref[:MLA_ROPE, :]).astype(BF16)
    kr_n = _dot(krn_ref[...].astype(BF16), e_ref[...]).astype(BF16)
    q_all = jnp.concatenate([q_ref[0, :, hsl(h)] for h in range(MLA_HEADS)], axis=0)
    q_lat = jnp.concatenate(
        [lax.dot_general(q_ref[0, :, hsl(h)], wk_ref[:, hsl(h)], _NT, preferred_element_type=F32)
         for h in range(MLA_HEADS)], axis=0).astype(BF16)
    s_past = (lax.dot_general(q_lat, ckv_p, _NT, preferred_element_type=F32)
              + lax.dot_general(q_all, kr_p, _NT, preferred_element_type=F32))
    s_new = (lax.dot_general(q_lat, ckv_n, _NT, preferred_element_type=F32)
             + lax.dot_general(q_all, kr_n, _NT, preferred_element_type=F32))
    q_chunk = jnp.concatenate([lax.broadcasted_iota(jnp.int32, (t_new, t_new), 0) >> CHUNK_SHIFT] * MLA_HEADS, axis=0)
    k_chunk = lax.broadcasted_iota(jnp.int32, (MLA_HEADS * t_new, t_new), 1) >> CHUNK_SHIFT
    s_new = jnp.where(k_chunk <= q_chunk, s_new, -jnp.inf)
    m = jnp.maximum(jnp.max(s_past, axis=-1, keepdims=True), jnp.max(s_new, axis=-1, keepdims=True))
    p_past = jnp.exp2(s_past - m)
    p_new = jnp.exp2(s_new - m)
    denom = jnp.sum(p_past, axis=-1, keepdims=True) + jnp.sum(p_new, axis=-1, keepdims=True)
    o_lat = ((_dot(p_past.astype(BF16), ckv_p) + _dot(p_new.astype(BF16), ckv_n)) / denom).astype(BF16)
    for h in range(MLA_HEADS):
        o = _dot(o_lat[h * t_new:(h + 1) * t_new], wv_ref[:, hsl(h)])
        o_ref[0, :, h * MLA_DV:(h + 1) * MLA_DV] = o[:, :MLA_DV].astype(o_ref.dtype)


def _attention_cached(q, cache_ckv, cache_kr, ckv_all, krp, lw, layer):
    b, t, width = q.shape
    t_past = cache_ckv.shape[2]
    weights = [lw["wk"], lw["e_head"], lw["wv"]]
    return pl.pallas_call(
        _attn_cached_kernel,
        grid=(b,),
        in_specs=[pl.BlockSpec((1, t, width), lambda bi: (bi, 0, 0)),
                  pl.BlockSpec((t_past, MLA_KV_LORA), lambda bi: (layer * b + bi, 0)),
                  pl.BlockSpec((t_past, MLA_ROPE), lambda bi: (layer * b + bi, 0)),
                  pl.BlockSpec((t, MLA_KV_LORA), lambda bi: (layer * b + bi, 0)),
                  pl.BlockSpec((t, LANES), lambda bi: (bi, 0))] + [_wspec(p) for p in weights],
        out_specs=pl.BlockSpec((1, t, MLA_HEADS * MLA_DV), lambda bi: (bi, 0, 0)),
        out_shape=jax.ShapeDtypeStruct((b, t, MLA_HEADS * MLA_DV), BF16),
        compiler_params=_params("parallel"),
        name="mla_attention_cached",
    )(q, cache_ckv.reshape(-1, MLA_KV_LORA), cache_kr.reshape(-1, MLA_ROPE), ckv_all.reshape(-1, MLA_KV_LORA),
      krp, *[p[0] for p in weights])


OUT_TM = 512
OUT_SUB = 256


def _out_proj_kernel(x_ref, or_ref, om_ref, og_ref, wrg_ref, wg3_ref, wro_ref, wmo_ref, wgo_ref, wout_ref,
                     rgn_ref, ggn_ref, g_ref, b_ref, o_ref, hr_ref, hg_ref):
    tm = x_ref.shape[0]
    sub = min(OUT_SUB, tm)
    for r in range(tm // sub):
        rs = slice(r * sub, (r + 1) * sub)
        x = x_ref[rs, :]
        xb = x.astype(BF16)

        ret_gate = _dot(xb, wrg_ref[...])
        gla_gate = [_dot(xb, wg3_ref[:, _G_GOG + h * GLA_DV:_G_GOG + (h + 1) * GLA_DV]) for h in range(GLA_HEADS)]

        for h in range(RET_HEADS):
            sl = slice(h * RET_DV, (h + 1) * RET_DV)
            gate = ret_gate[:, sl]
            o = or_ref[rs, sl]
            oc = o - jnp.mean(o, axis=-1, keepdims=True)
            normed = oc * lax.rsqrt(jnp.mean(oc * oc, axis=-1, keepdims=True) + EPS) * rgn_ref[:, sl]
            hr_ref[rs, sl] = (normed * (gate * jax.nn.sigmoid(gate))).astype(BF16)
        y_m = _dot(om_ref[rs, :], wmo_ref[...])
        gate_m = _dot(xb, wg3_ref[:, _G_BR + D_MODEL:_G_BR + 2 * D_MODEL])

        for h in range(GLA_HEADS):
            sl = slice(h * GLA_DV, (h + 1) * GLA_DV)
            gate = gla_gate[h]
            o = og_ref[rs, sl]
            normed = o * lax.rsqrt(jnp.mean(o * o, axis=-1, keepdims=True) + EPS) * ggn_ref[:, sl]
            hg_ref[rs, sl] = (normed * (gate * jax.nn.sigmoid(gate))).astype(BF16)
        y_r = _dot(hr_ref[rs, :], wro_ref[...])
        gate_r = _dot(xb, wg3_ref[:, _G_BR:_G_BR + D_MODEL])
        mix = jax.nn.sigmoid(gate_m) * y_m + jax.nn.sigmoid(gate_r) * y_r
        y_g = _dot(hg_ref[rs, :], wgo_ref[...])
        gate_g = _dot(xb, wg3_ref[:, _G_BR + 2 * D_MODEL:_G_BR + 3 * D_MODEL])
        mix = mix + jax.nn.sigmoid(gate_g) * y_g
        y = _dot(mix.astype(BF16), wout_ref[...])
        o_ref[rs, :] = _layer_norm(ALPHA * x + y, g_ref[...], b_ref[...])


def _out_proj_ln(x, o_r, o_m, o_g, lw, g, b):
    n = x.shape[0]
    tm = min(OUT_TM, n)
    weights = [lw["wrg"], lw["wg3"], lw["w_ret_o"], lw["w_mla_o"], lw["w_gla_o"], lw["w_out"],
               lw["ret_gn"], lw["gla_gn"], g, b]
    return pl.pallas_call(
        _out_proj_kernel,
        grid=(n // tm,),
        in_specs=[_row_spec(tm, D_MODEL), _row_spec(tm, RET_HEADS * RET_DV), _row_spec(tm, MLA_HEADS * MLA_DV),
                  _row_spec(tm, GLA_HEADS * GLA_DV)] + [_wspec(p) for p in weights],
        out_specs=_row_spec(tm, D_MODEL),
        out_shape=jax.ShapeDtypeStruct((n, D_MODEL), F32),
        scratch_shapes=[pltpu.VMEM((tm, RET_HEADS * RET_DV), BF16), pltpu.VMEM((tm, GLA_HEADS * GLA_DV), BF16)],
        compiler_params=_params("parallel"),
        name="out_proj_ln",
    )(x, o_r, o_m, o_g, *[p[0] for p in weights])


def _prep_weights(w):
    offs = np.cumsum((0,) + IN_SPLITS)
    w_in = w["w_in"]
    (r_q, r_k, r_v, r_g, m_cq, m_ckv, m_kr, g_q, g_k, g_v, g_lr, g_og, br) = [
        w_in[:, :, offs[i]:offs[i + 1]] for i in range(len(IN_SPLITS))]

    def pad_last(a, n):
        return jnp.pad(a, [(0, 0)] * (a.ndim - 1) + [(0, n - a.shape[-1])])

    w1 = jnp.concatenate([r_q, r_k, r_v, m_cq, pad_last(m_kr, LANES), m_ckv, g_q, g_k, g_v,
                          pad_last(g_lr, LANES)], axis=2).astype(BF16)
    dq = MLA_NOPE + MLA_ROPE
    width = MLA_HEADS * MLA_HEAD_PAD
    wq = pad_last(w["mla_w_q_up"].reshape(DEPTH, MLA_Q_LORA, MLA_HEADS, dq), MLA_HEAD_PAD).reshape(
        DEPTH, MLA_Q_LORA, width).astype(BF16)
    kv = w["mla_w_kv_up"].reshape(DEPTH, MLA_KV_LORA, MLA_HEADS, MLA_NOPE + MLA_DV)
    wk = pad_last(kv[..., :MLA_NOPE], MLA_HEAD_PAD).reshape(DEPTH, MLA_KV_LORA, width).astype(BF16)
    wv = pad_last(kv[..., MLA_NOPE:], MLA_HEAD_PAD).reshape(DEPTH, MLA_KV_LORA, width).astype(BF16)
    place = np.zeros((1, LANES, MLA_HEAD_PAD), np.float32)
    place[0, np.arange(MLA_ROPE), MLA_NOPE + np.arange(MLA_ROPE)] = 1.0
    row = lambda a: a.reshape(a.shape[0], 1, a.shape[-1])
    return {
        "w1": w1, "wq": wq, "wk": wk, "wv": wv, "wkv": w["mla_w_kv_up"].astype(BF16), "e_head": jnp.asarray(place, BF16),
        "wgate": jnp.pad(w["gla_w_gate_up"], ((0, 0), (0, LANES - GLA_GATE_RANK), (0, 0))).astype(BF16),
        "bgate": row(w["gla_b_gate"]), "qn": row(w["mla_q_norm_g"]), "kvn": row(w["mla_kv_norm_g"]),
        "wrg": r_g.astype(BF16), "wg3": w_in[:, :, offs[11]:offs[13]].astype(BF16),
        "w_ret_o": w["w_ret_o"].astype(BF16), "w_mla_o": w["w_mla_o"].astype(BF16),
        "w_gla_o": w["w_gla_o"].astype(BF16), "w_out": w["w_out"].astype(BF16),
        "ret_gn": row(w["ret_gn_g"]), "gla_gn": row(w["gla_gn_g"]),
        "f1u": w["ffn1_up"].astype(BF16), "f1d": w["ffn1_down"].astype(BF16),
        "f2u": w["ffn2_up"].astype(BF16), "f2d": w["ffn2_down"].astype(BF16),
        "ln_g": w["ln_g"].reshape(DEPTH * 3, 1, D_MODEL), "ln_b": w["ln_b"].reshape(DEPTH * 3, 1, D_MODEL),
    }


def _layer_params(sw, layer):
    lw = {k: (v, layer if v.shape[0] == DEPTH else 0) for k, v in sw.items() if not k.startswith("ln_")}
    for k in range(3):
        lw["ln_g%d" % k] = (sw["ln_g"], layer * 3 + k)
        lw["ln_b%d" % k] = (sw["ln_b"], layer * 3 + k)
    return lw


def _rope_tables(pos, tm):
    def cos_sin(half):
        inv = ROPE_THETA ** (-jnp.arange(half, dtype=F32) / half)
        ang = pos.astype(F32)[:, None] * inv[None, :]
        return jnp.cos(ang), jnp.sin(ang)

    t = pos.shape[0]
    c32, s32 = cos_sin(RET_DK // 2)
    c16, s16 = cos_sin(MLA_ROPE // 2)
    cr_h = jnp.tile(jnp.concatenate([c32, c32], axis=1), (1, RET_HEADS))
    sr_h = jnp.tile(jnp.concatenate([-s32, s32], axis=1), (1, RET_HEADS))
    k_scale = RET_DK ** -0.5
    q_scale = (MLA_NOPE + MLA_ROPE) ** -0.5 * float(np.log2(np.e))
    zeros = lambda n: jnp.zeros((t, n), F32)
    tabs = {
        "cr": jnp.concatenate([cr_h, cr_h * k_scale], axis=1),
        "sr": jnp.concatenate([sr_h, sr_h * k_scale], axis=1),
        "cq": jnp.concatenate([jnp.ones((t, MLA_NOPE), F32), c16, c16, zeros(LANES - MLA_NOPE - MLA_ROPE)], axis=1) * q_scale,
        "sq": jnp.concatenate([zeros(MLA_NOPE), -s16, s16, zeros(LANES - MLA_NOPE - MLA_ROPE)], axis=1) * q_scale,
        "ck": jnp.concatenate([c16, c16, zeros(LANES - MLA_ROPE)], axis=1),
        "sk": jnp.concatenate([-s16, s16, zeros(LANES - MLA_ROPE)], axis=1),
    }
    if t < tm:
        tabs = {k: jnp.tile(v, (tm // t, 1)) for k, v in tabs.items()}
    return tabs


def _group_layer(x, b, t, lw, tabs, past, layer, carried):
    n = b * t
    x = _ffn_ln(x, lw["f1u"], lw["f1d"], lw["ln_g0"], lw["ln_b0"])
    rqk, rv, qm, ckv_all, krp, kr_all, gqk, gv, la, *kv = _in_proj(x, lw, tabs, layer, carried, past is None)

    log_gamma = jnp.log(1.0 - 2.0 ** (-5.0 - jnp.arange(RET_HEADS, dtype=F32)))
    ret_la = jnp.repeat(log_gamma, RET_DK)[None, :]
    s_ret0 = None if past is None else jnp.swapaxes(past[2], -1, -2)
    s_gla0 = None if past is None else jnp.swapaxes(past[3], -1, -2)
    o_r, s_retT = _scan(rqk.reshape(b, t, -1), rv.reshape(b, t, -1), ret_la, s_ret0,
                        heads=RET_HEADS, dk=RET_DK, dv=RET_DV)
    o_g, s_glaT = _scan(gqk.reshape(b, t, -1), gv.reshape(b, t, -1), la.reshape(b, t, -1), s_gla0,
                        heads=GLA_HEADS, dk=GLA_DK, dv=GLA_DV)

    qm = qm.reshape(b, t, -1)
    if past is None:
        o_m = _attention(qm, kv[0].reshape(b, t, -1), kv[1].reshape(b, t, -1), tq=min(ATT_TQ, t), tk=min(ATT_TK, t))
    else:
        o_m = _attention_cached(qm, past[0], past[1], ckv_all, krp, lw, layer)

    x = _out_proj_ln(x, o_r.reshape(n, -1), o_m.reshape(n, -1), o_g.reshape(n, -1), lw, lw["ln_g1"], lw["ln_b1"])
    x = _ffn_ln(x, lw["f2u"], lw["f2d"], lw["ln_g2"], lw["ln_b2"])
    return x, (ckv_all, kr_all), (jnp.swapaxes(s_retT, -1, -2), jnp.swapaxes(s_glaT, -1, -2))


def kernel(x_prompt, x_sample, cache_mla_ckv, cache_mla_krope, state_ret, state_gla, w_in, ret_gn_g, mla_q_norm_g, mla_w_q_up, mla_kv_norm_g, mla_w_kv_up, gla_w_gate_up, gla_b_gate, gla_gn_g, w_ret_o, w_mla_o, w_gla_o, w_out, ffn1_up, ffn1_down, ffn2_up, ffn2_down, ln_g, ln_b):
    w = dict(w_in=w_in, ret_gn_g=ret_gn_g, mla_q_norm_g=mla_q_norm_g, mla_w_q_up=mla_w_q_up,
             mla_kv_norm_g=mla_kv_norm_g, mla_w_kv_up=mla_w_kv_up, gla_w_gate_up=gla_w_gate_up,
             gla_b_gate=gla_b_gate, gla_gn_g=gla_gn_g, w_ret_o=w_ret_o, w_mla_o=w_mla_o, w_gla_o=w_gla_o,
             w_out=w_out, ffn1_up=ffn1_up, ffn1_down=ffn1_down, ffn2_up=ffn2_up, ffn2_down=ffn2_down,
             ln_g=ln_g, ln_b=ln_b)
    bp, tp, _ = x_prompt.shape
    bs, ts, _ = x_sample.shape
    t_past = cache_mla_ckv.shape[2]
    assert t_past % CHUNK == 0 and tp % CHUNK == 0 and ts % CHUNK == 0
    tabs_p = _rope_tables(jnp.arange(tp), min(INP_TM, bp * tp))
    tabs_s = _rope_tables(t_past + jnp.arange(ts), min(INP_TM, bs * ts))
    xp = x_prompt.reshape(bp * tp, D_MODEL)
    xs = x_sample.reshape(bs * ts, D_MODEL)
    carried_p = carried_s = None
    st_p, st_s = [], []
    sw = _prep_weights(w)
    for l in range(DEPTH):
        lw = _layer_params(sw, l)
        xp, carried_p, st = _group_layer(xp, bp, tp, lw, tabs_p, None, l, carried_p)
        st_p.append(st)
        past = (cache_mla_ckv, cache_mla_krope, state_ret[l], state_gla[l])
        xs, carried_s, st = _group_layer(xs, bs, ts, lw, tabs_s, past, l, carried_s)
        st_s.append(st)
    stack = lambda sts, i: jnp.stack([s[i] for s in sts])
    return (xp.reshape(bp, tp, D_MODEL), xs.reshape(bs, ts, D_MODEL),
            carried_p[0].reshape(DEPTH, bp, tp, -1), carried_p[1].reshape(DEPTH, bp, tp, -1),
            stack(st_p, 0), stack(st_p, 1),
            carried_s[0].reshape(DEPTH, bs, ts, -1), carried_s[1].reshape(DEPTH, bs, ts, -1),
            stack(st_s, 0), stack(st_s, 1))
```

```python
import functools

import numpy as np
import jax
import jax.numpy as jnp
from jax import lax
from jax.experimental import pallas as pl
from jax.experimental.pallas import tpu as pltpu

F32 = jnp.float32
BF16 = jnp.bfloat16

D_MODEL = 1024
DEPTH = 2
CHUNK = 64
CHUNK_SHIFT = 6
ALPHA = (2 * DEPTH) ** 0.25
EPS = 1e-5
ROPE_THETA = 10000.0
RET_HEADS, RET_DK, RET_DV = 4, 64, 128
MLA_HEADS, MLA_Q_LORA, MLA_KV_LORA, MLA_NOPE, MLA_ROPE, MLA_DV = 8, 384, 256, 64, 32, 64
GLA_HEADS, GLA_DK, GLA_DV, GLA_GATE_RANK, GLA_TAU = 4, 128, 256, 16, 16.0
D_FF = 2816
N_BRANCH = 3
IN_SPLITS = (RET_HEADS * RET_DK, RET_HEADS * RET_DK, RET_HEADS * RET_DV, RET_HEADS * RET_DV,
             MLA_Q_LORA, MLA_KV_LORA, MLA_ROPE,
             GLA_HEADS * GLA_DK, GLA_HEADS * GLA_DK, GLA_HEADS * GLA_DV, GLA_GATE_RANK, GLA_HEADS * GLA_DV,
             N_BRANCH * D_MODEL)

LANES = 128
MLA_HEAD_PAD = LANES
VMEM_LIMIT = 56 * 1024 * 1024

_C_RQK, _C_RV, _C_CQ, _C_KR, _C_CKV, _C_GQK, _C_GV, _C_LR, _C_END = (
    0, 512, 1024, 1408, 1536, 1792, 2816, 3840, 3968)
_G_GOG, _G_BR = 0, 1024

_NT = (((1,), (1,)), ((), ()))
_TN = (((0,), (0,)), ((), ()))


def _params(*sem):
    return pltpu.CompilerParams(dimension_semantics=sem, vmem_limit_bytes=VMEM_LIMIT)


def _dot(a, b):
    return jnp.dot(a, b, preferred_element_type=F32)


def _layer_norm(z, g, b):
    mu = jnp.mean(z, axis=-1, keepdims=True)
    zc = z - mu
    var = jnp.mean(zc * zc, axis=-1, keepdims=True)
    return zc * lax.rsqrt(var + EPS) * g + b


def _row_spec(tm, cols):
    return pl.BlockSpec((tm, cols), lambda i: (i, 0))


def _full_spec(shape):
    return pl.BlockSpec(shape, lambda *_: (0,) * len(shape), pipeline_mode=pl.Buffered(1))


def _wspec(param):
    arr, idx = param
    return pl.BlockSpec((None,) + arr.shape[1:], lambda *_: (idx,) + (0,) * (arr.ndim - 1), pipeline_mode=pl.Buffered(1))


FFN_TM = 1024
FFN_SUB = 512
FFN_FC = 256


def _ffn_ln_kernel(x_ref, wup_ref, wd_ref, g_ref, b_ref, o_ref, h_ref):
    tm = x_ref.shape[0]
    sub = min(FFN_SUB, tm)
    for r in range(tm // sub):
        rs = slice(r * sub, (r + 1) * sub)
        x = x_ref[rs, :]
        xb = x.astype(BF16)
        for c in range(D_FF // FFN_FC):
            sl = slice(c * FFN_FC, (c + 1) * FFN_FC)
            gate = _dot(xb, wup_ref[:, sl])
            up = _dot(xb, wup_ref[:, D_FF + c * FFN_FC:D_FF + (c + 1) * FFN_FC])
            h_ref[rs, sl] = (gate * jax.nn.sigmoid(gate) * up).astype(BF16)
        y = _dot(h_ref[rs, :], wd_ref[...])
        o_ref[rs, :] = _layer_norm(ALPHA * x + 0.5 * y, g_ref[...], b_ref[...])


def _ffn_ln(x, wup, wd, g, b):
    n = x.shape[0]
    tm = min(FFN_TM, n)
    params = [wup, wd, g, b]
    return pl.pallas_call(
        _ffn_ln_kernel,
        grid=(n // tm,),
        in_specs=[_row_spec(tm, D_MODEL)] + [_wspec(p) for p in params],
        out_specs=_row_spec(tm, D_MODEL),
        out_shape=jax.ShapeDtypeStruct((n, D_MODEL), F32),
        scratch_shapes=[pltpu.VMEM((tm, D_FF), BF16)],
        compiler_params=_params("parallel"),
        name="ffn_ln",
    )(x, *[p[0] for p in params])


INP_TM = 512


def _swap_halves(x, first_mask, half):
    return jnp.where(first_mask, pltpu.roll(x, LANES - half, 1), pltpu.roll(x, half, 1))


def _in_proj_kernel(x_ref, w_ref, wq_ref, wgate_ref, bgate_ref, qn_ref, kvn_ref,
                    cr_ref, sr_ref, cq_ref, sq_ref, ck_ref, sk_ref, *rest, n_kv_out):
    rqk_ref, rv_ref, qm_ref, ckv_ref, kr_ref, kr_out_ref, gqk_ref, gv_ref, la_ref = rest[-9 - n_kv_out:][:9]
    xb = x_ref[...].astype(BF16)
    tm = xb.shape[0]
    lane = lax.broadcasted_iota(jnp.int32, (tm, LANES), 1)
    ghd = GLA_HEADS * GLA_DK

    h_rqk = _dot(xb, w_ref[:, _C_RQK:_C_RV])
    h_cq_kr = _dot(xb, w_ref[:, _C_CQ:_C_CKV])

    ret_first = (lane & (RET_DK - 1)) < RET_DK // 2
    for c in range(2 * RET_HEADS * RET_DK // LANES):
        sl = slice(c * LANES, (c + 1) * LANES)
        h = h_rqk[:, sl]
        rqk_ref[:, sl] = h * cr_ref[:, sl] + _swap_halves(h, ret_first, RET_DK // 2) * sr_ref[:, sl]
    rv_ref[...] = _dot(xb, w_ref[:, _C_RV:_C_CQ]).astype(BF16)

    hq = h_cq_kr[:, :MLA_Q_LORA]
    cq = hq * lax.rsqrt(jnp.mean(hq * hq, axis=-1, keepdims=True) + EPS) * qn_ref[...]
    q_up = _dot(cq.astype(BF16), wq_ref[...])
    hkv = _dot(xb, w_ref[:, _C_CKV:_C_GQK])
    gv_ref[...] = _dot(xb, w_ref[:, _C_GV:_C_LR]).astype(BF16)
    q_first = lane < MLA_NOPE + MLA_ROPE // 2
    for h_i in range(MLA_HEADS):
        sl = slice(h_i * MLA_HEAD_PAD, (h_i + 1) * MLA_HEAD_PAD)
        qh = q_up[:, sl]
        qm_ref[:, sl] = (qh * cq_ref[...] + _swap_halves(qh, q_first, MLA_ROPE // 2) * sq_ref[...]).astype(BF16)

    hkr = h_cq_kr[:, MLA_Q_LORA:]
    kr = hkr * ck_ref[...] + _swap_halves(hkr, lane < MLA_ROPE // 2, MLA_ROPE // 2) * sk_ref[...]
    kr_ref[...] = kr
    kr_out_ref[...] = kr[:, :MLA_ROPE]
    ckv = hkv * lax.rsqrt(jnp.mean(hkv * hkv, axis=-1, keepdims=True) + EPS) * kvn_ref[...]
    ckv_ref[...] = ckv
    lr = _dot(xb, w_ref[:, _C_LR:_C_END]).astype(BF16)
    gla_q = _dot(xb, w_ref[:, _C_GQK:_C_GQK + ghd])
    if n_kv_out:
        wkv_ref = rest[0]
        k_ref, v_ref = rest[-2:]
        kv = _dot(ckv.astype(BF16), wkv_ref[...])
        kr_at_rope = pltpu.roll(kr, MLA_NOPE, 1)
        ones_cols = jnp.where(lane >= MLA_DV, 1.0, 0.0)
        for h_i in range(MLA_HEADS):
            sl = slice(h_i * MLA_HEAD_PAD, (h_i + 1) * MLA_HEAD_PAD)
            g = kv[:, sl]
            k_ref[:, sl] = jnp.where(lane < MLA_NOPE, g, kr_at_rope).astype(BF16)
            v_ref[:, sl] = jnp.where(lane < MLA_DV, pltpu.roll(g, LANES - MLA_NOPE, 1), ones_cols).astype(BF16)

    gqk_ref[:, :ghd] = gla_q * (GLA_DK ** -0.5)
    logit = _dot(lr, wgate_ref[...]) + bgate_ref[...]
    gqk_ref[:, ghd:] = _dot(xb, w_ref[:, _C_GQK + ghd:_C_GV])
    log_sig = jnp.minimum(logit, 0.0) - jnp.log1p(jnp.exp(-jnp.abs(logit)))
    la_ref[...] = log_sig / GLA_TAU


def _in_proj(x, lw, tabs, layer, carried, emit_kv):
    n = x.shape[0]
    tm = min(INP_TM, n)
    period = tabs["cr"].shape[0] // tm

    def tab_spec(cols):
        return pl.BlockSpec((tm, cols), lambda i: (i % period, 0))

    def layer_spec(cols):
        return pl.BlockSpec((None, tm, cols), lambda i: (layer, i, 0))

    rows = lambda cols, dtype: (_row_spec(tm, cols), jax.ShapeDtypeStruct((n, cols), dtype))
    stacked = lambda cols: (layer_spec(cols), jax.ShapeDtypeStruct((DEPTH, n, cols), F32))
    outs = [rows(2 * RET_HEADS * RET_DK, F32), rows(RET_HEADS * RET_DV, BF16), rows(MLA_HEADS * MLA_HEAD_PAD, BF16),
            stacked(MLA_KV_LORA), rows(LANES, F32), stacked(MLA_ROPE), rows(2 * GLA_HEADS * GLA_DK, F32),
            rows(GLA_HEADS * GLA_DV, BF16), rows(GLA_HEADS * GLA_DK, F32)]
    weights = [lw["w1"], lw["wq"], lw["wgate"], lw["bgate"], lw["qn"], lw["kvn"]]
    tables = [tabs["cr"], tabs["sr"], tabs["cq"], tabs["sq"], tabs["ck"], tabs["sk"]]
    in_specs = ([_row_spec(tm, D_MODEL)] + [_wspec(p) for p in weights]
                + [tab_spec(t.shape[1]) for t in tables])
    args = [x, *[p[0] for p in weights], *tables]
    if emit_kv:
        kv_weights = [lw["wkv"]]
        in_specs += [_wspec(p) for p in kv_weights]
        args += [p[0] for p in kv_weights]
        outs += [rows(MLA_HEADS * MLA_HEAD_PAD, BF16)] * 2
    aliases = {}
    if carried is not None:
        aliases = {len(args): 3, len(args) + 1: 5}
        in_specs += [pl.BlockSpec(memory_space=pl.ANY)] * 2
        args += list(carried)
    return pl.pallas_call(
        functools.partial(_in_proj_kernel, n_kv_out=2 if emit_kv else 0),
        grid=(n // tm,),
        in_specs=in_specs,
        out_specs=[o[0] for o in outs],
        out_shape=[o[1] for o in outs],
        input_output_aliases=aliases,
        compiler_params=_params("parallel"),
        name="in_proj",
    )(*args)


SCAN_CHUNKS_PER_STEP = 4
SCAN_STREAMS_PER_STEP = 4


def _cumsum_rows(a):
    rows = lax.broadcasted_iota(jnp.int32, a.shape, 0)
    s = 1
    while s < a.shape[0]:
        a = a + jnp.where(rows >= s, pltpu.roll(a, s, 0), 0.0)
        s *= 2
    return a


def _scan_kernel(*refs, heads, dk, dv, n_chunks, nb, has_la, has_s0):
    it = iter(refs)
    qk_ref, v_ref = next(it), next(it)
    la_ref = next(it)
    s0_ref = next(it) if has_s0 else None
    o_ref, sT_ref, st_ref = next(it), next(it), next(it)
    hd = heads * dk
    step = pl.program_id(1)

    @pl.when(step == 0)
    def _():
        if has_s0:
            st_ref[...] = s0_ref[...]
        else:
            st_ref[...] = jnp.zeros_like(st_ref)

    row = lax.broadcasted_iota(jnp.int32, (CHUNK, CHUNK), 0)
    col = lax.broadcasted_iota(jnp.int32, (CHUNK, CHUNK), 1)
    causal = row >= col
    ksl = lambda h: slice(h * dk, (h + 1) * dk)
    vsl = lambda h: slice(h * dv, (h + 1) * dv)

    def decayed_operands(bi, rows):
        if has_la:
            bc = _cumsum_rows(la_ref[bi, rows, :])
        else:
            steps = lax.broadcasted_iota(jnp.int32, (CHUNK, hd), 0) + 1
            bc = steps.astype(F32) * la_ref[...]
        bl = bc[CHUNK - 1:CHUNK, :]
        q = qk_ref[bi, rows, :hd]
        k = qk_ref[bi, rows, hd:]
        return ((q * jnp.exp(bc)).astype(BF16), (k * jnp.exp(-bc)).astype(BF16),
                (k * jnp.exp(bl - bc)).astype(BF16), jnp.exp(bl))

    def advance(chains, ops, rows):
        att = [lax.dot_general(ops[bi][0][:, ksl(h)], ops[bi][1][:, ksl(h)], _NT, preferred_element_type=F32)
               for bi, h in chains]
        cross = [lax.dot_general(ops[bi][0][:, ksl(h)], st_ref[bi, h].astype(BF16), _NT, preferred_element_type=F32)
                 for bi, h in chains]
        upd = [lax.dot_general(v_ref[bi, rows, vsl(h)], ops[bi][2][:, ksl(h)], _TN, preferred_element_type=F32)
               for bi, h in chains]
        for n, (bi, h) in enumerate(chains):
            a = jnp.where(causal, att[n], 0.0).astype(BF16)
            o_ref[bi, rows, vsl(h)] = _dot(a, v_ref[bi, rows, vsl(h)]) + cross[n]
        for n, (bi, h) in enumerate(chains):
            st_ref[bi, h] = st_ref[bi, h] * ops[bi][3][:, ksl(h)] + upd[n]

    def chunk(c, carry):
        rows = pl.ds(pl.multiple_of(c * CHUNK, CHUNK), CHUNK)
        if has_la:
            ops = {0: decayed_operands(0, rows)}
            for bi in range(nb):
                if bi + 1 < nb:
                    ops[bi + 1] = decayed_operands(bi + 1, rows)
                advance([(bi, h) for h in range(heads)], ops, rows)
        else:
            ops = {bi: decayed_operands(bi, rows) for bi in range(nb)}
            advance([(bi, h) for h in range(heads) for bi in range(nb)], ops, rows)
        return carry

    lax.fori_loop(0, n_chunks, chunk, 0)

    @pl.when(step == pl.num_programs(1) - 1)
    def _():
        sT_ref[...] = st_ref[...]


def _scan(qk, v, la, s0T, *, heads, dk, dv):
    b, t, _ = qk.shape
    has_la = la.ndim == 3
    has_s0 = s0T is not None
    nb = min(SCAN_STREAMS_PER_STEP, b)
    ncs = min(SCAN_CHUNKS_PER_STEP, t // CHUNK)
    rows = ncs * CHUNK
    hd, hv = heads * dk, heads * dv

    def seq_spec(cols):
        return pl.BlockSpec((nb, rows, cols), lambda bi, si: (bi, si, 0))

    st_spec = pl.BlockSpec((nb, heads, dv, dk), lambda bi, si: (bi, 0, 0, 0))
    in_specs = [seq_spec(2 * hd), seq_spec(hv), seq_spec(hd) if has_la else _full_spec(la.shape)]
    args = [qk, v, la]
    if has_s0:
        in_specs.append(st_spec)
        args.append(s0T)
    kern = functools.partial(_scan_kernel, heads=heads, dk=dk, dv=dv, n_chunks=ncs, nb=nb,
                             has_la=has_la, has_s0=has_s0)
    return pl.pallas_call(
        kern,
        grid=(b // nb, t // rows),
        in_specs=in_specs,
        out_specs=[seq_spec(hv), st_spec],
        out_shape=[jax.ShapeDtypeStruct((b, t, hv), F32), jax.ShapeDtypeStruct((b, heads, dv, dk), F32)],
        scratch_shapes=[pltpu.VMEM((nb, heads, dv, dk), F32)],
        compiler_params=_params("parallel", "arbitrary"),
        name="scan_h%d_dk%d" % (heads, dk),
    )(*args)


ATT_TQ = 1024
ATT_TK = 512
ATT_SUB = 512
ATT_LOOKAHEAD = 2


def _attn_kernel(i_ref, j_ref, q_ref, k_ref, v_ref, o_ref, m_ref, acc_ref, *, tq, tk, nk):
    i, j = i_ref[pl.program_id(1)], j_ref[pl.program_id(1)]
    n_sub = tq // ATT_SUB
    q_lo = [(i * tq + r * ATT_SUB) // CHUNK for r in range(n_sub)]
    q_hi = [(i * tq + (r + 1) * ATT_SUB - 1) // CHUNK for r in range(n_sub)]
    k_lo = (j * tk) // CHUNK
    k_hi = (j * tk + tk - 1) // CHUNK
    j_last = jnp.minimum(nk - 1, ((q_hi[-1] + 1) * CHUNK - 1) // tk)

    @pl.when(j == 0)
    def _():
        m_ref[...] = jnp.full_like(m_ref, -jnp.inf)
        acc_ref[...] = jnp.zeros_like(acc_ref)

    def tile(modes):
        vis = {}
        for r, mode in enumerate(modes):
            if mode == "masked":
                row0 = i * tq + r * ATT_SUB
                qc = (row0 + lax.broadcasted_iota(jnp.int32, (ATT_SUB, tk), 0)) >> CHUNK_SHIFT
                kc = (j * tk + lax.broadcasted_iota(jnp.int32, (ATT_SUB, tk), 1)) >> CHUNK_SHIFT
                vis[r] = kc <= qc
        work = [(h, r) for h in range(MLA_HEADS) for r, mode in enumerate(modes) if mode != "skip"]

        def scores(h, r):
            hs = slice(h * MLA_HEAD_PAD, (h + 1) * MLA_HEAD_PAD)
            return lax.dot_general(q_ref[0, r * ATT_SUB:(r + 1) * ATT_SUB, hs], k_ref[0, :, hs], _NT,
                                   preferred_element_type=F32)

        pending = [scores(*w) for w in work[:ATT_LOOKAHEAD]]
        for n, (h, r) in enumerate(work):
            hs = slice(h * MLA_HEAD_PAD, (h + 1) * MLA_HEAD_PAD)
            rs = slice(r * ATT_SUB, (r + 1) * ATT_SUB)
            s = pending.pop(0)
            if n + ATT_LOOKAHEAD < len(work):
                pending.append(scores(*work[n + ATT_LOOKAHEAD]))
            if modes[r] == "masked":
                s = jnp.where(vis[r], s, -jnp.inf)
            m_prev = m_ref[h, rs, :]
            m_new = jnp.maximum(m_prev, jnp.max(s, axis=-1, keepdims=True))
            alpha = jnp.exp2(m_prev - m_new)
            p = jnp.concatenate([jnp.exp2(s[:, c * LANES:(c + 1) * LANES] - m_new).astype(BF16)
                                 for c in range(tk // LANES)], axis=1)
            acc_ref[h, rs, :] = alpha * acc_ref[h, rs, :] + _dot(p, v_ref[0, :, hs])
            m_ref[h, rs, :] = m_new

    def cond(r, mode):
        if mode == "full":
            return k_hi <= q_lo[r]
        if mode == "masked":
            return jnp.logical_and(k_hi > q_lo[r], k_lo <= q_hi[r])
        return k_lo > q_hi[r]

    combos = [("full",) * n_sub] + [("skip",) * r + ("masked",) + ("full",) * (n_sub - r - 1) for r in range(n_sub)]
    for modes in combos:
        pred = functools.reduce(jnp.logical_and, [cond(r, mode) for r, mode in enumerate(modes)])
        pl.when(pred)(functools.partial(tile, modes))

    @pl.when(j == j_last)
    def _():
        first = lax.broadcasted_iota(jnp.int32, (tq, LANES), 1) < MLA_DV
        for pair in range(MLA_HEADS // 2):
            a0, a1 = acc_ref[2 * pair], acc_ref[2 * pair + 1]
            even = a0 / pltpu.roll(a0, MLA_DV, 1)
            odd = pltpu.roll(a1, MLA_DV, 1) / a1
            o_ref[0, :, pair * LANES:(pair + 1) * LANES] = jnp.where(first, even, odd).astype(o_ref.dtype)


def _attention(q, k, v, *, tq, tk):
    b, t_q, _ = q.shape
    t_k = k.shape[1]
    assert tq % ATT_SUB == 0 and t_q % tq == 0 and t_k % tk == 0
    nq, nk = t_q // tq, t_k // tk
    width = MLA_HEADS * MLA_HEAD_PAD
    pairs = [(i, j) for i in range(nq)
             for j in range(min(nk - 1, (((i * tq + tq - 1) // CHUNK + 1) * CHUNK - 1) // tk) + 1)]
    i_tab = jnp.asarray([p[0] for p in pairs], jnp.int32)
    j_tab = jnp.asarray([p[1] for p in pairs], jnp.int32)

    kern = functools.partial(_attn_kernel, tq=tq, tk=tk, nk=nk)
    grid_spec = pltpu.PrefetchScalarGridSpec(
        num_scalar_prefetch=2,
        grid=(b, len(pairs)),
        in_specs=[pl.BlockSpec((1, tq, width), lambda bi, p, it, jt: (bi, it[p], 0)),
                  pl.BlockSpec((1, tk, width), lambda bi, p, it, jt: (bi, jt[p], 0)),
                  pl.BlockSpec((1, tk, width), lambda bi, p, it, jt: (bi, jt[p], 0))],
        out_specs=pl.BlockSpec((1, tq, MLA_HEADS * MLA_DV), lambda bi, p, it, jt: (bi, it[p], 0)),
        scratch_shapes=[pltpu.VMEM((MLA_HEADS, tq, LANES), F32), pltpu.VMEM((MLA_HEADS, tq, MLA_HEAD_PAD), F32)],
    )
    return pl.pallas_call(
        kern,
        grid_spec=grid_spec,
        out_shape=jax.ShapeDtypeStruct((b, t_q, MLA_HEADS * MLA_DV), BF16),
        compiler_params=_params("parallel", "arbitrary"),
        name="mla_attention",
    )(i_tab, j_tab, q, k, v)


def _attn_cached_kernel(q_ref, cp_ref, krp_ref, cn_ref, krn_ref, wk_ref, e_ref, wv_ref, o_ref):
    t_new = q_ref.shape[1]
    hsl = lambda h: slice(h * MLA_HEAD_PAD, (h + 1) * MLA_HEAD_PAD)
    ckv_p = cp_ref[...].astype(BF16)
    ckv_n = cn_ref[...].astype(BF16)
    kr_p = _dot(krp_ref[...].astype(BF16), e_ref[:MLA_ROPE, :]).astype(BF16)
    kr_n = _dot(krn_ref[...].astype(BF16), e_ref[...]).astype(BF16)
    q_all = jnp.concatenate([q_ref[0, :, hsl(h)] for h in range(MLA_HEADS)], axis=0)
    q_lat = jnp.concatenate(
        [lax.dot_general(q_ref[0, :, hsl(h)], wk_ref[:, hsl(h)], _NT, preferred_element_type=F32)
         for h in range(MLA_HEADS)], axis=0).astype(BF16)
    s_past = (lax.dot_general(q_lat, ckv_p, _NT, preferred_element_type=F32)
              + lax.dot_general(q_all, kr_p, _NT, preferred_element_type=F32))
    s_new = (lax.dot_general(q_lat, ckv_n, _NT, preferred_element_type=F32)
             + lax.dot_general(q_all, kr_n, _NT, preferred_element_type=F32))
    q_chunk = jnp.concatenate([lax.broadcasted_iota(jnp.int32, (t_new, t_new), 0) >> CHUNK_SHIFT] * MLA_HEADS, axis=0)
    k_chunk = lax.broadcasted_iota(jnp.int32, (MLA_HEADS * t_new, t_new), 1) >> CHUNK_SHIFT
    s_new = jnp.where(k_chunk <= q_chunk, s_new, -jnp.inf)
    m = jnp.maximum(jnp.max(s_past, axis=-1, keepdims=True), jnp.max(s_new, axis=-1, keepdims=True))
    p_past = jnp.exp2(s_past - m)
    p_new = jnp.exp2(s_new - m)
    denom = jnp.sum(p_past, axis=-1, keepdims=True) + jnp.sum(p_new, axis=-1, keepdims=True)
    o_lat = ((_dot(p_past.astype(BF16), ckv_p) + _dot(p_new.astype(BF16), ckv_n)) / denom).astype(BF16)
    for h in range(MLA_HEADS):
        o = _dot(o_lat[h * t_new:(h + 1) * t_new], wv_ref[:, hsl(h)])
        o_ref[0, :, h * MLA_DV:(h + 1) * MLA_DV] = o[:, :MLA_DV].astype(o_ref.dtype)


def _attention_cached(q, cache_ckv, cache_kr, ckv_all, krp, lw, layer):
    b, t, width = q.shape
    t_past = cache_ckv.shape[2]
    weights = [lw["wk"], lw["e_head"], lw["wv"]]
    return pl.pallas_call(
        _attn_cached_kernel,
        grid=(b,),
        in_specs=[pl.BlockSpec((1, t, width), lambda bi: (bi, 0, 0)),
                  pl.BlockSpec((t_past, MLA_KV_LORA), lambda bi: (layer * b + bi, 0)),
                  pl.BlockSpec((t_past, MLA_ROPE), lambda bi: (layer * b + bi, 0)),
                  pl.BlockSpec((t, MLA_KV_LORA), lambda bi: (layer * b + bi, 0)),
                  pl.BlockSpec((t, LANES), lambda bi: (bi, 0))] + [_wspec(p) for p in weights],
        out_specs=pl.BlockSpec((1, t, MLA_HEADS * MLA_DV), lambda bi: (bi, 0, 0)),
        out_shape=jax.ShapeDtypeStruct((b, t, MLA_HEADS * MLA_DV), BF16),
        compiler_params=_params("parallel"),
        name="mla_attention_cached",
    )(q, cache_ckv.reshape(-1, MLA_KV_LORA), cache_kr.reshape(-1, MLA_ROPE), ckv_all.reshape(-1, MLA_KV_LORA),
      krp, *[p[0] for p in weights])


OUT_TM = 512
OUT_SUB = 256


def _out_proj_kernel(x_ref, or_ref, om_ref, og_ref, wrg_ref, wg3_ref, wro_ref, wmo_ref, wgo_ref, wout_ref,
                     rgn_ref, ggn_ref, g_ref, b_ref, o_ref, hr_ref, hg_ref):
    tm = x_ref.shape[0]
    sub = min(OUT_SUB, tm)
    for r in range(tm // sub):
        rs = slice(r * sub, (r + 1) * sub)
        x = x_ref[rs, :]
        xb = x.astype(BF16)

        ret_gate = _dot(xb, wrg_ref[...])
        gla_gate = [_dot(xb, wg3_ref[:, _G_GOG + h * GLA_DV:_G_GOG + (h + 1) * GLA_DV]) for h in range(GLA_HEADS)]

        for h in range(RET_HEADS):
            sl = slice(h * RET_DV, (h + 1) * RET_DV)
            gate = ret_gate[:, sl]
            o = or_ref[rs, sl]
            oc = o - jnp.mean(o, axis=-1, keepdims=True)
            normed = oc * lax.rsqrt(jnp.mean(oc * oc, axis=-1, keepdims=True) + EPS) * rgn_ref[:, sl]
            hr_ref[rs, sl] = (normed * (gate * jax.nn.sigmoid(gate))).astype(BF16)
        y_m = _dot(om_ref[rs, :], wmo_ref[...])
        gate_m = _dot(xb, wg3_ref[:, _G_BR + D_MODEL:_G_BR + 2 * D_MODEL])

        for h in range(GLA_HEADS):
            sl = slice(h * GLA_DV, (h + 1) * GLA_DV)
            gate = gla_gate[h]
            o = og_ref[rs, sl]
            normed = o * lax.rsqrt(jnp.mean(o * o, axis=-1, keepdims=True) + EPS) * ggn_ref[:, sl]
            hg_ref[rs, sl] = (normed * (gate * jax.nn.sigmoid(gate))).astype(BF16)
        y_r = _dot(hr_ref[rs, :], wro_ref[...])
        gate_r = _dot(xb, wg3_ref[:, _G_BR:_G_BR + D_MODEL])
        mix = jax.nn.sigmoid(gate_m) * y_m + jax.nn.sigmoid(gate_r) * y_r
        y_g = _dot(hg_ref[rs, :], wgo_ref[...])
        gate_g = _dot(xb, wg3_ref[:, _G_BR + 2 * D_MODEL:_G_BR + 3 * D_MODEL])
        mix = mix + jax.nn.sigmoid(gate_g) * y_g
        y = _dot(mix.astype(BF16), wout_ref[...])
        o_ref[rs, :] = _layer_norm(ALPHA * x + y, g_ref[...], b_ref[...])


def _out_proj_ln(x, o_r, o_m, o_g, lw, g, b):
    n = x.shape[0]
    tm = min(OUT_TM, n)
    weights = [lw["wrg"], lw["wg3"], lw["w_ret_o"], lw["w_mla_o"], lw["w_gla_o"], lw["w_out"],
               lw["ret_gn"], lw["gla_gn"], g, b]
    return pl.pallas_call(
        _out_proj_kernel,
        grid=(n // tm,),
        in_specs=[_row_spec(tm, D_MODEL), _row_spec(tm, RET_HEADS * RET_DV), _row_spec(tm, MLA_HEADS * MLA_DV),
                  _row_spec(tm, GLA_HEADS * GLA_DV)] + [_wspec(p) for p in weights],
        out_specs=_row_spec(tm, D_MODEL),
        out_shape=jax.ShapeDtypeStruct((n, D_MODEL), F32),
        scratch_shapes=[pltpu.VMEM((tm, RET_HEADS * RET_DV), BF16), pltpu.VMEM((tm, GLA_HEADS * GLA_DV), BF16)],
        compiler_params=_params("parallel"),
        name="out_proj_ln",
    )(x, o_r, o_m, o_g, *[p[0] for p in weights])


def _prep_weights(w):
    offs = np.cumsum((0,) + IN_SPLITS)
    w_in = w["w_in"]
    (r_q, r_k, r_v, r_g, m_cq, m_ckv, m_kr, g_q, g_k, g_v, g_lr, g_og, br) = [
        w_in[:, :, offs[i]:offs[i + 1]] for i in range(len(IN_SPLITS))]

    def pad_last(a, n):
        return jnp.pad(a, [(0, 0)] * (a.ndim - 1) + [(0, n - a.shape[-1])])

    w1 = jnp.concatenate([r_q, r_k, r_v, m_cq, pad_last(m_kr, LANES), m_ckv, g_q, g_k, g_v,
                          pad_last(g_lr, LANES)], axis=2).astype(BF16)
    dq = MLA_NOPE + MLA_ROPE
    width = MLA_HEADS * MLA_HEAD_PAD
    wq = pad_last(w["mla_w_q_up"].reshape(DEPTH, MLA_Q_LORA, MLA_HEADS, dq), MLA_HEAD_PAD).reshape(
        DEPTH, MLA_Q_LORA, width).astype(BF16)
    kv = w["mla_w_kv_up"].reshape(DEPTH, MLA_KV_LORA, MLA_HEADS, MLA_NOPE + MLA_DV)
    wk = pad_last(kv[..., :MLA_NOPE], MLA_HEAD_PAD).reshape(DEPTH, MLA_KV_LORA, width).astype(BF16)
    wv = pad_last(kv[..., MLA_NOPE:], MLA_HEAD_PAD).reshape(DEPTH, MLA_KV_LORA, width).astype(BF16)
    place = np.zeros((1, LANES, MLA_HEAD_PAD), np.float32)
    place[0, np.arange(MLA_ROPE), MLA_NOPE + np.arange(MLA_ROPE)] = 1.0
    row = lambda a: a.reshape(a.shape[0], 1, a.shape[-1])
    return {
        "w1": w1, "wq": wq, "wk": wk, "wv": wv, "wkv": w["mla_w_kv_up"].astype(BF16), "e_head": jnp.asarray(place, BF16),
        "wgate": jnp.pad(w["gla_w_gate_up"], ((0, 0), (0, LANES - GLA_GATE_RANK), (0, 0))).astype(BF16),
        "bgate": row(w["gla_b_gate"]), "qn": row(w["mla_q_norm_g"]), "kvn": row(w["mla_kv_norm_g"]),
        "wrg": r_g.astype(BF16), "wg3": w_in[:, :, offs[11]:offs[13]].astype(BF16),
        "w_ret_o": w["w_ret_o"].astype(BF16), "w_mla_o": w["w_mla_o"].astype(BF16),
        "w_gla_o": w["w_gla_o"].astype(BF16), "w_out": w["w_out"].astype(BF16),
        "ret_gn": row(w["ret_gn_g"]), "gla_gn": row(w["gla_gn_g"]),
        "f1u": w["ffn1_up"].astype(BF16), "f1d": w["ffn1_down"].astype(BF16),
        "f2u": w["ffn2_up"].astype(BF16), "f2d": w["ffn2_down"].astype(BF16),
        "ln_g": w["ln_g"].reshape(DEPTH * 3, 1, D_MODEL), "ln_b": w["ln_b"].reshape(DEPTH * 3, 1, D_MODEL),
    }


def _layer_params(sw, layer):
    lw = {k: (v, layer if v.shape[0] == DEPTH else 0) for k, v in sw.items() if not k.startswith("ln_")}
    for k in range(3):
        lw["ln_g%d" % k] = (sw["ln_g"], layer * 3 + k)
        lw["ln_b%d" % k] = (sw["ln_b"], layer * 3 + k)
    return lw


def _rope_tables(pos, tm):
    def cos_sin(half):
        inv = ROPE_THETA ** (-jnp.arange(half, dtype=F32) / half)
        ang = pos.astype(F32)[:, None] * inv[None, :]
        return jnp.cos(ang), jnp.sin(ang)

    t = pos.shape[0]
    c32, s32 = cos_sin(RET_DK // 2)
    c16, s16 = cos_sin(MLA_ROPE // 2)
    cr_h = jnp.tile(jnp.concatenate([c32, c32], axis=1), (1, RET_HEADS))
    sr_h = jnp.tile(jnp.concatenate([-s32, s32], axis=1), (1, RET_HEADS))
    k_scale = RET_DK ** -0.5
    q_scale = (MLA_NOPE + MLA_ROPE) ** -0.5 * float(np.log2(np.e))
    zeros = lambda n: jnp.zeros((t, n), F32)
    tabs = {
        "cr": jnp.concatenate([cr_h, cr_h * k_scale], axis=1),
        "sr": jnp.concatenate([sr_h, sr_h * k_scale], axis=1),
        "cq": jnp.concatenate([jnp.ones((t, MLA_NOPE), F32), c16, c16, zeros(LANES - MLA_NOPE - MLA_ROPE)], axis=1) * q_scale,
        "sq": jnp.concatenate([zeros(MLA_NOPE), -s16, s16, zeros(LANES - MLA_NOPE - MLA_ROPE)], axis=1) * q_scale,
        "ck": jnp.concatenate([c16, c16, zeros(LANES - MLA_ROPE)], axis=1),
        "sk": jnp.concatenate([-s16, s16, zeros(LANES - MLA_ROPE)], axis=1),
    }
    if t < tm:
        tabs = {k: jnp.tile(v, (tm // t, 1)) for k, v in tabs.items()}
    return tabs


def _group_layer(x, b, t, lw, tabs, past, layer, carried):
    n = b * t
    x = _ffn_ln(x, lw["f1u"], lw["f1d"], lw["ln_g0"], lw["ln_b0"])
    rqk, rv, qm, ckv_all, krp, kr_all, gqk, gv, la, *kv = _in_proj(x, lw, tabs, layer, carried, past is None)

    log_gamma = jnp.log(1.0 - 2.0 ** (-5.0 - jnp.arange(RET_HEADS, dtype=F32)))
    ret_la = jnp.repeat(log_gamma, RET_DK)[None, :]
    s_ret0 = None if past is None else jnp.swapaxes(past[2], -1, -2)
    s_gla0 = None if past is None else jnp.swapaxes(past[3], -1, -2)
    o_r, s_retT = _scan(rqk.reshape(b, t, -1), rv.reshape(b, t, -1), ret_la, s_ret0,
                        heads=RET_HEADS, dk=RET_DK, dv=RET_DV)
    o_g, s_glaT = _scan(gqk.reshape(b, t, -1), gv.reshape(b, t, -1), la.reshape(b, t, -1), s_gla0,
                        heads=GLA_HEADS, dk=GLA_DK, dv=GLA_DV)

    qm = qm.reshape(b, t, -1)
    if past is None:
        o_m = _attention(qm, kv[0].reshape(b, t, -1), kv[1].reshape(b, t, -1), tq=min(ATT_TQ, t), tk=min(ATT_TK, t))
    else:
        o_m = _attention_cached(qm, past[0], past[1], ckv_all, krp, lw, layer)

    x = _out_proj_ln(x, o_r.reshape(n, -1), o_m.reshape(n, -1), o_g.reshape(n, -1), lw, lw["ln_g1"], lw["ln_b1"])
    x = _ffn_ln(x, lw["f2u"], lw["f2d"], lw["ln_g2"], lw["ln_b2"])
    return x, (ckv_all, kr_all), (jnp.swapaxes(s_retT, -1, -2), jnp.swapaxes(s_glaT, -1, -2))


def kernel(x_prompt, x_sample, cache_mla_ckv, cache_mla_krope, state_ret, state_gla, w_in, ret_gn_g, mla_q_norm_g, mla_w_q_up, mla_kv_norm_g, mla_w_kv_up, gla_w_gate_up, gla_b_gate, gla_gn_g, w_ret_o, w_mla_o, w_gla_o, w_out, ffn1_up, ffn1_down, ffn2_up, ffn2_down, ln_g, ln_b):
    w = dict(w_in=w_in, ret_gn_g=ret_gn_g, mla_q_norm_g=mla_q_norm_g, mla_w_q_up=mla_w_q_up,
             mla_kv_norm_g=mla_kv_norm_g, mla_w_kv_up=mla_w_kv_up, gla_w_gate_up=gla_w_gate_up,
             gla_b_gate=gla_b_gate, gla_gn_g=gla_gn_g, w_ret_o=w_ret_o, w_mla_o=w_mla_o, w_gla_o=w_gla_o,
             w_out=w_out, ffn1_up=ffn1_up, ffn1_down=ffn1_down, ffn2_up=ffn2_up, ffn2_down=ffn2_down,
             ln_g=ln_g, ln_b=ln_b)
    bp, tp, _ = x_prompt.shape
    bs, ts, _ = x_sample.shape
    t_past = cache_mla_ckv.shape[2]
    assert t_past % CHUNK == 0 and tp % CHUNK == 0 and ts % CHUNK == 0
    tabs_p = _rope_tables(jnp.arange(tp), min(INP_TM, bp * tp))
    tabs_s = _rope_tables(t_past + jnp.arange(ts), min(INP_TM, bs * ts))
    xp = x_prompt.reshape(bp * tp, D_MODEL)
    xs = x_sample.reshape(bs * ts, D_MODEL)
    carried_p = carried_s = None
    st_p, st_s = [], []
    sw = _prep_weights(w)
    for l in range(DEPTH):
        lw = _layer_params(sw, l)
        xp, carried_p, st = _group_layer(xp, bp, tp, lw, tabs_p, None, l, carried_p)
        st_p.append(st)
        past = (cache_mla_ckv, cache_mla_krope, state_ret[l], state_gla[l])
        xs, carried_s, st = _group_layer(xs, bs, ts, lw, tabs_s, past, l, carried_s)
        st_s.append(st)
    stack = lambda sts, i: jnp.stack([s[i] for s in sts])
    return (xp.reshape(bp, tp, D_MODEL), xs.reshape(bs, ts, D_MODEL),
            carried_p[0].reshape(DEPTH, bp, tp, -1), carried_p[1].reshape(DEPTH, bp, tp, -1),
            stack(st_p, 0), stack(st_p, 1),
            carried_s[0].reshape(DEPTH, bs, ts, -1), carried_s[1].reshape(DEPTH, bs, ts, -1),
            stack(st_s, 0), stack(st_s, 1))
```

```python
import functools

import numpy as np
import jax
import jax.numpy as jnp
from jax import lax
from jax.experimental import pallas as pl
from jax.experimental.pallas import tpu as pltpu

F32 = jnp.float32
BF16 = jnp.bfloat16

D_MODEL = 1024
DEPTH = 2
CHUNK = 64
CHUNK_SHIFT = 6
ALPHA = (2 * DEPTH) ** 0.25
EPS = 1e-5
ROPE_THETA = 10000.0
RET_HEADS, RET_DK, RET_DV = 4, 64, 128
MLA_HEADS, MLA_Q_LORA, MLA_KV_LORA, MLA_NOPE, MLA_ROPE, MLA_DV = 8, 384, 256, 64, 32, 64
GLA_HEADS, GLA_DK, GLA_DV, GLA_GATE_RANK, GLA_TAU = 4, 128, 256, 16, 16.0
D_FF = 2816
N_BRANCH = 3
IN_SPLITS = (RET_HEADS * RET_DK, RET_HEADS * RET_DK, RET_HEADS * RET_DV, RET_HEADS * RET_DV,
             MLA_Q_LORA, MLA_KV_LORA, MLA_ROPE,
             GLA_HEADS * GLA_DK, GLA_HEADS * GLA_DK, GLA_HEADS * GLA_DV, GLA_GATE_RANK, GLA_HEADS * GLA_DV,
             N_BRANCH * D_MODEL)

LANES = 128
MLA_HEAD_PAD = LANES
VMEM_LIMIT = 56 * 1024 * 1024

_C_RQK, _C_RV, _C_CQ, _C_KR, _C_CKV, _C_GQK, _C_GV, _C_LR, _C_END = (
    0, 512, 1024, 1408, 1536, 1792, 2816, 3840, 3968)
_G_GOG, _G_BR = 0, 1024

_NT = (((1,), (1,)), ((), ()))
_TN = (((0,), (0,)), ((), ()))


def _params(*sem):
    return pltpu.CompilerParams(dimension_semantics=sem, vmem_limit_bytes=VMEM_LIMIT)


def _dot(a, b):
    return jnp.dot(a, b, preferred_element_type=F32)


def _layer_norm(z, g, b):
    mu = jnp.mean(z, axis=-1, keepdims=True)
    zc = z - mu
    var = jnp.mean(zc * zc, axis=-1, keepdims=True)
    return zc * lax.rsqrt(var + EPS) * g + b


def _row_spec(tm, cols):
    return pl.BlockSpec((tm, cols), lambda i: (i, 0))


def _full_spec(shape):
    return pl.BlockSpec(shape, lambda *_: (0,) * len(shape), pipeline_mode=pl.Buffered(1))


def _wspec(param):
    arr, idx = param
    return pl.BlockSpec((None,) + arr.shape[1:], lambda *_: (idx,) + (0,) * (arr.ndim - 1), pipeline_mode=pl.Buffered(1))


FFN_TM = 1024
FFN_SUB = 512
FFN_FC = 256


def _ffn_ln_kernel(x_ref, wup_ref, wd_ref, g_ref, b_ref, o_ref, h_ref):
    tm = x_ref.shape[0]
    sub = min(FFN_SUB, tm)
    for r in range(tm // sub):
        rs = slice(r * sub, (r + 1) * sub)
        x = x_ref[rs, :]
        xb = x.astype(BF16)
        for c in range(D_FF // FFN_FC):
            sl = slice(c * FFN_FC, (c + 1) * FFN_FC)
            gate = _dot(xb, wup_ref[:, sl])
            up = _dot(xb, wup_ref[:, D_FF + c * FFN_FC:D_FF + (c + 1) * FFN_FC])
            h_ref[rs, sl] = (gate * jax.nn.sigmoid(gate) * up).astype(BF16)
        y = _dot(h_ref[rs, :], wd_ref[...])
        o_ref[rs, :] = _layer_norm(ALPHA * x + 0.5 * y, g_ref[...], b_ref[...])


def _ffn_ln(x, wup, wd, g, b):
    n = x.shape[0]
    tm = min(FFN_TM, n)
    params = [wup, wd, g, b]
    return pl.pallas_call(
        _ffn_ln_kernel,
        grid=(n // tm,),
        in_specs=[_row_spec(tm, D_MODEL)] + [_wspec(p) for p in params],
        out_specs=_row_spec(tm, D_MODEL),
        out_shape=jax.ShapeDtypeStruct((n, D_MODEL), F32),
        scratch_shapes=[pltpu.VMEM((tm, D_FF), BF16)],
        compiler_params=_params("parallel"),
        name="ffn_ln",
    )(x, *[p[0] for p in params])


INP_TM = 512


def _swap_halves(x, first_mask, half):
    return jnp.where(first_mask, pltpu.roll(x, LANES - half, 1), pltpu.roll(x, half, 1))


def _in_proj_kernel(x_ref, w_ref, wq_ref, wgate_ref, bgate_ref, qn_ref, kvn_ref,
                    cr_ref, sr_ref, cq_ref, sq_ref, ck_ref, sk_ref, *rest, n_kv_out):
    rqk_ref, rv_ref, qm_ref, ckv_ref, kr_ref, kr_out_ref, gqk_ref, gv_ref, la_ref = rest[-9 - n_kv_out:][:9]
    xb = x_ref[...].astype(BF16)
    tm = xb.shape[0]
    lane = lax.broadcasted_iota(jnp.int32, (tm, LANES), 1)
    ghd = GLA_HEADS * GLA_DK

    h_rqk = _dot(xb, w_ref[:, _C_RQK:_C_RV])
    h_cq_kr = _dot(xb, w_ref[:, _C_CQ:_C_CKV])

    ret_first = (lane & (RET_DK - 1)) < RET_DK // 2
    for c in range(2 * RET_HEADS * RET_DK // LANES):
        sl = slice(c * LANES, (c + 1) * LANES)
        h = h_rqk[:, sl]
        rqk_ref[:, sl] = h * cr_ref[:, sl] + _swap_halves(h, ret_first, RET_DK // 2) * sr_ref[:, sl]
    rv_ref[...] = _dot(xb, w_ref[:, _C_RV:_C_CQ]).astype(BF16)

    hq = h_cq_kr[:, :MLA_Q_LORA]
    cq = hq * lax.rsqrt(jnp.mean(hq * hq, axis=-1, keepdims=True) + EPS) * qn_ref[...]
    q_up = _dot(cq.astype(BF16), wq_ref[...])
    hkv = _dot(xb, w_ref[:, _C_CKV:_C_GQK])
    gv_ref[...] = _dot(xb, w_ref[:, _C_GV:_C_LR]).astype(BF16)
    q_first = lane < MLA_NOPE + MLA_ROPE // 2
    for h_i in range(MLA_HEADS):
        sl = slice(h_i * MLA_HEAD_PAD, (h_i + 1) * MLA_HEAD_PAD)
        qh = q_up[:, sl]
        qm_ref[:, sl] = (qh * cq_ref[...] + _swap_halves(qh, q_first, MLA_ROPE // 2) * sq_ref[...]).astype(BF16)

    hkr = h_cq_kr[:, MLA_Q_LORA:]
    kr = hkr * ck_ref[...] + _swap_halves(hkr, lane < MLA_ROPE // 2, MLA_ROPE // 2) * sk_ref[...]
    kr_ref[...] = kr
    kr_out_ref[...] = kr[:, :MLA_ROPE]
    ckv = hkv * lax.rsqrt(jnp.mean(hkv * hkv, axis=-1, keepdims=True) + EPS) * kvn_ref[...]
    ckv_ref[...] = ckv
    lr = _dot(xb, w_ref[:, _C_LR:_C_END]).astype(BF16)
    gla_q = _dot(xb, w_ref[:, _C_GQK:_C_GQK + ghd])
    if n_kv_out:
        wkv_ref = rest[0]
        k_ref, v_ref = rest[-2:]
        kv = _dot(ckv.astype(BF16), wkv_ref[...])
        kr_at_rope = pltpu.roll(kr, MLA_NOPE, 1)
        ones_cols = jnp.where(lane >= MLA_DV, 1.0, 0.0)
        for h_i in range(MLA_HEADS):
            sl = slice(h_i * MLA_HEAD_PAD, (h_i + 1) * MLA_HEAD_PAD)
            g = kv[:, sl]
            k_ref[:, sl] = jnp.where(lane < MLA_NOPE, g, kr_at_rope).astype(BF16)
            v_ref[:, sl] = jnp.where(lane < MLA_DV, pltpu.roll(g, LANES - MLA_NOPE, 1), ones_cols).astype(BF16)

    gqk_ref[:, :ghd] = gla_q * (GLA_DK ** -0.5)
    logit = _dot(lr, wgate_ref[...]) + bgate_ref[...]
    gqk_ref[:, ghd:] = _dot(xb, w_ref[:, _C_GQK + ghd:_C_GV])
    log_sig = jnp.minimum(logit, 0.0) - jnp.log1p(jnp.exp(-jnp.abs(logit)))
    la_ref[...] = log_sig / GLA_TAU


def _in_proj(x, lw, tabs, layer, carried, emit_kv):
    n = x.shape[0]
    tm = min(INP_TM, n)
    period = tabs["cr"].shape[0] // tm

    def tab_spec(cols):
        return pl.BlockSpec((tm, cols), lambda i: (i % period, 0))

    def layer_spec(cols):
        return pl.BlockSpec((None, tm, cols), lambda i: (layer, i, 0))

    rows = lambda cols, dtype: (_row_spec(tm, cols), jax.ShapeDtypeStruct((n, cols), dtype))
    stacked = lambda cols: (layer_spec(cols), jax.ShapeDtypeStruct((DEPTH, n, cols), F32))
    outs = [rows(2 * RET_HEADS * RET_DK, F32), rows(RET_HEADS * RET_DV, BF16), rows(MLA_HEADS * MLA_HEAD_PAD, BF16),
            stacked(MLA_KV_LORA), rows(LANES, F32), stacked(MLA_ROPE), rows(2 * GLA_HEADS * GLA_DK, F32),
            rows(GLA_HEADS * GLA_DV, BF16), rows(GLA_HEADS * GLA_DK, F32)]
    weights = [lw["w1"], lw["wq"], lw["wgate"], lw["bgate"], lw["qn"], lw["kvn"]]
    tables = [tabs["cr"], tabs["sr"], tabs["cq"], tabs["sq"], tabs["ck"], tabs["sk"]]
    in_specs = ([_row_spec(tm, D_MODEL)] + [_wspec(p) for p in weights]
                + [tab_spec(t.shape[1]) for t in tables])
    args = [x, *[p[0] for p in weights], *tables]
    if emit_kv:
        kv_weights = [lw["wkv"]]
        in_specs += [_wspec(p) for p in kv_weights]
        args += [p[0] for p in kv_weights]
        outs += [rows(MLA_HEADS * MLA_HEAD_PAD, BF16)] * 2
    aliases = {}
    if carried is not None:
        aliases = {len(args): 3, len(args) + 1: 5}
        in_specs += [pl.BlockSpec(memory_space=pl.ANY)] * 2
        args += list(carried)
    return pl.pallas_call(
        functools.partial(_in_proj_kernel, n_kv_out=2 if emit_kv else 0),
        grid=(n // tm,),
        in_specs=in_specs,
        out_specs=[o[0] for o in outs],
        out_shape=[o[1] for o in outs],
        input_output_aliases=aliases,
        compiler_params=_params("parallel"),
        name="in_proj",
    )(*args)


SCAN_CHUNKS_PER_STEP = 4
SCAN_STREAMS_PER_STEP = 4


def _cumsum_rows(a):
    rows = lax.broadcasted_iota(jnp.int32, a.shape, 0)
    s = 1
    while s < a.shape[0]:
        a = a + jnp.where(rows >= s, pltpu.roll(a, s, 0), 0.0)
        s *= 2
    return a


def _scan_kernel(*refs, heads, dk, dv, n_chunks, nb, has_la, has_s0):
    it = iter(refs)
    qk_ref, v_ref = next(it), next(it)
    la_ref = next(it)
    s0_ref = next(it) if has_s0 else None
    o_ref, sT_ref, st_ref = next(it), next(it), next(it)
    hd = heads * dk
    step = pl.program_id(1)

    @pl.when(step == 0)
    def _():
        if has_s0:
            st_ref[...] = s0_ref[...]
        else:
            st_ref[...] = jnp.zeros_like(st_ref)

    row = lax.broadcasted_iota(jnp.int32, (CHUNK, CHUNK), 0)
    col = lax.broadcasted_iota(jnp.int32, (CHUNK, CHUNK), 1)
    causal = row >= col
    ksl = lambda h: slice(h * dk, (h + 1) * dk)
    vsl = lambda h: slice(h * dv, (h + 1) * dv)

    def decayed_operands(bi, rows):
        if has_la:
            bc = _cumsum_rows(la_ref[bi, rows, :])
        else:
            steps = lax.broadcasted_iota(jnp.int32, (CHUNK, hd), 0) + 1
            bc = steps.astype(F32) * la_ref[...]
        bl = bc[CHUNK - 1:CHUNK, :]
        q = qk_ref[bi, rows, :hd]
        k = qk_ref[bi, rows, hd:]
        return ((q * jnp.exp(bc)).astype(BF16), (k * jnp.exp(-bc)).astype(BF16),
                (k * jnp.exp(bl - bc)).astype(BF16), jnp.exp(bl))

    def advance(chains, ops, rows):
        att = [lax.dot_general(ops[bi][0][:, ksl(h)], ops[bi][1][:, ksl(h)], _NT, preferred_element_type=F32)
               for bi, h in chains]
        cross = [lax.dot_general(ops[bi][0][:, ksl(h)], st_ref[bi, h].astype(BF16), _NT, preferred_element_type=F32)
                 for bi, h in chains]
        upd = [lax.dot_general(v_ref[bi, rows, vsl(h)], ops[bi][2][:, ksl(h)], _TN, preferred_element_type=F32)
               for bi, h in chains]
        for n, (bi, h) in enumerate(chains):
            a = jnp.where(causal, att[n], 0.0).astype(BF16)
            o_ref[bi, rows, vsl(h)] = _dot(a, v_ref[bi, rows, vsl(h)]) + cross[n]
        for n, (bi, h) in enumerate(chains):
            st_ref[bi, h] = st_ref[bi, h] * ops[bi][3][:, ksl(h)] + upd[n]

    def chunk(c, carry):
        rows = pl.ds(pl.multiple_of(c * CHUNK, CHUNK), CHUNK)
        if has_la:
            ops = {0: decayed_operands(0, rows)}
            for bi in range(nb):
                if bi + 1 < nb:
                    ops[bi + 1] = decayed_operands(bi + 1, rows)
                advance([(bi, h) for h in range(heads)], ops, rows)
        else:
            ops = {bi: decayed_operands(bi, rows) for bi in range(nb)}
            advance([(bi, h) for h in range(heads) for bi in range(nb)], ops, rows)
        return carry

    lax.fori_loop(0, n_chunks, chunk, 0)

    @pl.when(step == pl.num_programs(1) - 1)
    def _():
        sT_ref[...] = st_ref[...]


def _scan(qk, v, la, s0T, *, heads, dk, dv):
    b, t, _ = qk.shape
    has_la = la.ndim == 3
    has_s0 = s0T is not None
    nb = min(SCAN_STREAMS_PER_STEP, b)
    ncs = min(SCAN_CHUNKS_PER_STEP, t // CHUNK)
    rows = ncs * CHUNK
    hd, hv = heads * dk, heads * dv

    def seq_spec(cols):
        return pl.BlockSpec((nb, rows, cols), lambda bi, si: (bi, si, 0))

    st_spec = pl.BlockSpec((nb, heads, dv, dk), lambda bi, si: (bi, 0, 0, 0))
    in_specs = [seq_spec(2 * hd), seq_spec(hv), seq_spec(hd) if has_la else _full_spec(la.shape)]
    args = [qk, v, la]
    if has_s0:
        in_specs.append(st_spec)
        args.append(s0T)
    kern = functools.partial(_scan_kernel, heads=heads, dk=dk, dv=dv, n_chunks=ncs, nb=nb,
                             has_la=has_la, has_s0=has_s0)
    return pl.pallas_call(
        kern,
        grid=(b // nb, t // rows),
        in_specs=in_specs,
        out_specs=[seq_spec(hv), st_spec],
        out_shape=[jax.ShapeDtypeStruct((b, t, hv), F32), jax.ShapeDtypeStruct((b, heads, dv, dk), F32)],
        scratch_shapes=[pltpu.VMEM((nb, heads, dv, dk), F32)],
        compiler_params=_params("parallel", "arbitrary"),
        name="scan_h%d_dk%d" % (heads, dk),
    )(*args)


ATT_TQ = 1024
ATT_TK = 512
ATT_SUB = 512
ATT_LOOKAHEAD = 2


def _attn_kernel(i_ref, j_ref, q_ref, k_ref, v_ref, o_ref, m_ref, acc_ref, *, tq, tk, nk):
    i, j = i_ref[pl.program_id(1)], j_ref[pl.program_id(1)]
    n_sub = tq // ATT_SUB
    q_lo = [(i * tq + r * ATT_SUB) // CHUNK for r in range(n_sub)]
    q_hi = [(i * tq + (r + 1) * ATT_SUB - 1) // CHUNK for r in range(n_sub)]
    k_lo = (j * tk) // CHUNK
    k_hi = (j * tk + tk - 1) // CHUNK
    j_last = jnp.minimum(nk - 1, ((q_hi[-1] + 1) * CHUNK - 1) // tk)

    @pl.when(j == 0)
    def _():
        m_ref[...] = jnp.full_like(m_ref, -jnp.inf)
        acc_ref[...] = jnp.zeros_like(acc_ref)

    def tile(modes):
        vis = {}
        for r, mode in enumerate(modes):
            if mode == "masked":
                row0 = i * tq + r * ATT_SUB
                qc = (row0 + lax.broadcasted_iota(jnp.int32, (ATT_SUB, tk), 0)) >> CHUNK_SHIFT
                kc = (j * tk + lax.broadcasted_iota(jnp.int32, (ATT_SUB, tk), 1)) >> CHUNK_SHIFT
                vis[r] = kc <= qc
        work = [(h, r) for h in range(MLA_HEADS) for r, mode in enumerate(modes) if mode != "skip"]

        def scores(h, r):
            hs = slice(h * MLA_HEAD_PAD, (h + 1) * MLA_HEAD_PAD)
            return lax.dot_general(q_ref[0, r * ATT_SUB:(r + 1) * ATT_SUB, hs], k_ref[0, :, hs], _NT,
                                   preferred_element_type=F32)

        pending = [scores(*w) for w in work[:ATT_LOOKAHEAD]]
        for n, (h, r) in enumerate(work):
            hs = slice(h * MLA_HEAD_PAD, (h + 1) * MLA_HEAD_PAD)
            rs = slice(r * ATT_SUB, (r + 1) * ATT_SUB)
            s = pending.pop(0)
            if n + ATT_LOOKAHEAD < len(work):
                pending.append(scores(*work[n + ATT_LOOKAHEAD]))
            if modes[r] == "masked":
                s = jnp.where(vis[r], s, -jnp.inf)
            m_prev = m_ref[h, rs, :]
            m_new = jnp.maximum(m_prev, jnp.max(s, axis=-1, keepdims=True))
            alpha = jnp.exp2(m_prev - m_new)
            p = jnp.concatenate([jnp.exp2(s[:, c * LANES:(c + 1) * LANES] - m_new).astype(BF16)
                                 for c in range(tk // LANES)], axis=1)
            acc_ref[h, rs, :] = alpha * acc_ref[h, rs, :] + _dot(p, v_ref[0, :, hs])
            m_ref[h, rs, :] = m_new

    def cond(r, mode):
        if mode == "full":
            return k_hi <= q_lo[r]
        if mode == "masked":
            return jnp.logical_and(k_hi > q_lo[r], k_lo <= q_hi[r])
        return k_lo > q_hi[r]

    combos = [("full",) * n_sub] + [("skip",) * r + ("masked",) + ("full",) * (n_sub - r - 1) for r in range(n_sub)]
    for modes in combos:
        pred = functools.reduce(jnp.logical_and, [cond(r, mode) for r, mode in enumerate(modes)])
        pl.when(pred)(functools.partial(tile, modes))

    @pl.when(j == j_last)
    def _():
        first = lax.broadcasted_iota(jnp.int32, (tq, LANES), 1) < MLA_DV
        for pair in range(MLA_HEADS // 2):
            a0, a1 = acc_ref[2 * pair], acc_ref[2 * pair + 1]
            even = a0 / pltpu.roll(a0, MLA_DV, 1)
            odd = pltpu.roll(a1, MLA_DV, 1) / a1
            o_ref[0, :, pair * LANES:(pair + 1) * LANES] = jnp.where(first, even, odd).astype(o_ref.dtype)


def _attention(q, k, v, *, tq, tk):
    b, t_q, _ = q.shape
    t_k = k.shape[1]
    assert tq % ATT_SUB == 0 and t_q % tq == 0 and t_k % tk == 0
    nq, nk = t_q // tq, t_k // tk
    width = MLA_HEADS * MLA_HEAD_PAD
    pairs = [(i, j) for i in range(nq)
             for j in range(min(nk - 1, (((i * tq + tq - 1) // CHUNK + 1) * CHUNK - 1) // tk) + 1)]
    i_tab = jnp.asarray([p[0] for p in pairs], jnp.int32)
    j_tab = jnp.asarray([p[1] for p in pairs], jnp.int32)

    kern = functools.partial(_attn_kernel, tq=tq, tk=tk, nk=nk)
    grid_spec = pltpu.PrefetchScalarGridSpec(
        num_scalar_prefetch=2,
        grid=(b, len(pairs)),
        in_specs=[pl.BlockSpec((1, tq, width), lambda bi, p, it, jt: (bi, it[p], 0)),
                  pl.BlockSpec((1, tk, width), lambda bi, p, it, jt: (bi, jt[p], 0)),
                  pl.BlockSpec((1, tk, width), lambda bi, p, it, jt: (bi, jt[p], 0))],
        out_specs=pl.BlockSpec((1, tq, MLA_HEADS * MLA_DV), lambda bi, p, it, jt: (bi, it[p], 0)),
        scratch_shapes=[pltpu.VMEM((MLA_HEADS, tq, LANES), F32), pltpu.VMEM((MLA_HEADS, tq, MLA_HEAD_PAD), F32)],
    )
    return pl.pallas_call(
        kern,
        grid_spec=grid_spec,
        out_shape=jax.ShapeDtypeStruct((b, t_q, MLA_HEADS * MLA_DV), BF16),
        compiler_params=_params("parallel", "arbitrary"),
        name="mla_attention",
    )(i_tab, j_tab, q, k, v)


CACHED_GROUPS = 2


def _attn_cached_kernel(q_ref, cp_ref, krp_ref, cn_ref, krn_ref, wk_ref, e_ref, wv_ref, o_ref):
    t_new = q_ref.shape[1]
    hsl = lambda h: slice(h * MLA_HEAD_PAD, (h + 1) * MLA_HEAD_PAD)
    ckv_p = cp_ref[...].astype(BF16)
    ckv_n = cn_ref[...].astype(BF16)
    kr_p = _dot(krp_ref[...].astype(BF16), e_ref[:MLA_ROPE, :]).astype(BF16)
    kr_n = _dot(krn_ref[...].astype(BF16), e_ref[...]).astype(BF16)
    q_all = jnp.concatenate([q_ref[0, :, hsl(h)] for h in range(MLA_HEADS)], axis=0)
    q_lat = jnp.concatenate(
        [lax.dot_general(q_ref[0, :, hsl(h)], wk_ref[:, hsl(h)], _NT, preferred_element_type=F32)
         for h in range(MLA_HEADS)], axis=0).astype(BF16)
    q_chunk = jnp.concatenate([lax.broadcasted_iota(jnp.int32, (t_new, t_new), 0) >> CHUNK_SHIFT] * MLA_HEADS, axis=0)
    k_chunk = lax.broadcasted_iota(jnp.int32, (MLA_HEADS * t_new, t_new), 1) >> CHUNK_SHIFT
    vis_new = k_chunk <= q_chunk
    n_rows = MLA_HEADS * t_new
    grp = n_rows // CACHED_GROUPS

    def scores(g):
        rs = slice(g * grp, (g + 1) * grp)
        s_past = (lax.dot_general(q_lat[rs], ckv_p, _NT, preferred_element_type=F32)
                  + lax.dot_general(q_all[rs], kr_p, _NT, preferred_element_type=F32))
        s_new = (lax.dot_general(q_lat[rs], ckv_n, _NT, preferred_element_type=F32)
                 + lax.dot_general(q_all[rs], kr_n, _NT, preferred_element_type=F32))
        return s_past, jnp.where(vis_new[rs], s_new, -jnp.inf)

    pending = scores(0)
    o_lat = []
    for g in range(CACHED_GROUPS):
        s_past, s_new = pending
        if g + 1 < CACHED_GROUPS:
            pending = scores(g + 1)
        m = jnp.maximum(jnp.max(s_past, axis=-1, keepdims=True), jnp.max(s_new, axis=-1, keepdims=True))
        p_past = jnp.exp2(s_past - m)
        p_new = jnp.exp2(s_new - m)
        denom = jnp.sum(p_past, axis=-1, keepdims=True) + jnp.sum(p_new, axis=-1, keepdims=True)
        o_lat.append(((_dot(p_past.astype(BF16), ckv_p) + _dot(p_new.astype(BF16), ckv_n)) / denom).astype(BF16))
    o_lat = jnp.concatenate(o_lat, axis=0)
    for h in range(MLA_HEADS):
        o = _dot(o_lat[h * t_new:(h + 1) * t_new], wv_ref[:, hsl(h)])
        o_ref[0, :, h * MLA_DV:(h + 1) * MLA_DV] = o[:, :MLA_DV].astype(o_ref.dtype)


def _attention_cached(q, cache_ckv, cache_kr, ckv_all, krp, lw, layer):
    b, t, width = q.shape
    t_past = cache_ckv.shape[2]
    weights = [lw["wk"], lw["e_head"], lw["wv"]]
    return pl.pallas_call(
        _attn_cached_kernel,
        grid=(b,),
        in_specs=[pl.BlockSpec((1, t, width), lambda bi: (bi, 0, 0)),
                  pl.BlockSpec((t_past, MLA_KV_LORA), lambda bi: (layer * b + bi, 0)),
                  pl.BlockSpec((t_past, MLA_ROPE), lambda bi: (layer * b + bi, 0)),
                  pl.BlockSpec((t, MLA_KV_LORA), lambda bi: (layer * b + bi, 0)),
                  pl.BlockSpec((t, LANES), lambda bi: (bi, 0))] + [_wspec(p) for p in weights],
        out_specs=pl.BlockSpec((1, t, MLA_HEADS * MLA_DV), lambda bi: (bi, 0, 0)),
        out_shape=jax.ShapeDtypeStruct((b, t, MLA_HEADS * MLA_DV), BF16),
        compiler_params=_params("parallel"),
        name="mla_attention_cached",
    )(q, cache_ckv.reshape(-1, MLA_KV_LORA), cache_kr.reshape(-1, MLA_ROPE), ckv_all.reshape(-1, MLA_KV_LORA),
      krp, *[p[0] for p in weights])


OUT_TM = 512
OUT_SUB = 256


def _out_proj_kernel(x_ref, or_ref, om_ref, og_ref, wrg_ref, wg3_ref, wro_ref, wmo_ref, wgo_ref, wout_ref,
                     rgn_ref, ggn_ref, g_ref, b_ref, o_ref, hr_ref, hg_ref):
    tm = x_ref.shape[0]
    sub = min(OUT_SUB, tm)
    for r in range(tm // sub):
        rs = slice(r * sub, (r + 1) * sub)
        x = x_ref[rs, :]
        xb = x.astype(BF16)

        ret_gate = _dot(xb, wrg_ref[...])
        gla_gate = [_dot(xb, wg3_ref[:, _G_GOG + h * GLA_DV:_G_GOG + (h + 1) * GLA_DV]) for h in range(GLA_HEADS)]

        for h in range(RET_HEADS):
            sl = slice(h * RET_DV, (h + 1) * RET_DV)
            gate = ret_gate[:, sl]
            o = or_ref[rs, sl]
            oc = o - jnp.mean(o, axis=-1, keepdims=True)
            normed = oc * lax.rsqrt(jnp.mean(oc * oc, axis=-1, keepdims=True) + EPS) * rgn_ref[:, sl]
            hr_ref[rs, sl] = (normed * (gate * jax.nn.sigmoid(gate))).astype(BF16)
        y_m = _dot(om_ref[rs, :], wmo_ref[...])
        gate_m = _dot(xb, wg3_ref[:, _G_BR + D_MODEL:_G_BR + 2 * D_MODEL])

        for h in range(GLA_HEADS):
            sl = slice(h * GLA_DV, (h + 1) * GLA_DV)
            gate = gla_gate[h]
            o = og_ref[rs, sl]
            normed = o * lax.rsqrt(jnp.mean(o * o, axis=-1, keepdims=True) + EPS) * ggn_ref[:, sl]
            hg_ref[rs, sl] = (normed * (gate * jax.nn.sigmoid(gate))).astype(BF16)
        y_r = _dot(hr_ref[rs, :], wro_ref[...])
        gate_r = _dot(xb, wg3_ref[:, _G_BR:_G_BR + D_MODEL])
        mix = jax.nn.sigmoid(gate_m) * y_m + jax.nn.sigmoid(gate_r) * y_r
        y_g = _dot(hg_ref[rs, :], wgo_ref[...])
        gate_g = _dot(xb, wg3_ref[:, _G_BR + 2 * D_MODEL:_G_BR + 3 * D_MODEL])
        mix = mix + jax.nn.sigmoid(gate_g) * y_g
        y = _dot(mix.astype(BF16), wout_ref[...])
        o_ref[rs, :] = _layer_norm(ALPHA * x + y, g_ref[...], b_ref[...])


def _out_proj_ln(x, o_r, o_m, o_g, lw, g, b):
    n = x.shape[0]
    tm = min(OUT_TM, n)
    weights = [lw["wrg"], lw["wg3"], lw["w_ret_o"], lw["w_mla_o"], lw["w_gla_o"], lw["w_out"],
               lw["ret_gn"], lw["gla_gn"], g, b]
    return pl.pallas_call(
        _out_proj_kernel,
        grid=(n // tm,),
        in_specs=[_row_spec(tm, D_MODEL), _row_spec(tm, RET_HEADS * RET_DV), _row_spec(tm, MLA_HEADS * MLA_DV),
                  _row_spec(tm, GLA_HEADS * GLA_DV)] + [_wspec(p) for p in weights],
        out_specs=_row_spec(tm, D_MODEL),
        out_shape=jax.ShapeDtypeStruct((n, D_MODEL), F32),
        scratch_shapes=[pltpu.VMEM((tm, RET_HEADS * RET_DV), BF16), pltpu.VMEM((tm, GLA_HEADS * GLA_DV), BF16)],
        compiler_params=_params("parallel"),
        name="out_proj_ln",
    )(x, o_r, o_m, o_g, *[p[0] for p in weights])


def _prep_weights(w):
    offs = np.cumsum((0,) + IN_SPLITS)
    w_in = w["w_in"]
    (r_q, r_k, r_v, r_g, m_cq, m_ckv, m_kr, g_q, g_k, g_v, g_lr, g_og, br) = [
        w_in[:, :, offs[i]:offs[i + 1]] for i in range(len(IN_SPLITS))]

    def pad_last(a, n):
        return jnp.pad(a, [(0, 0)] * (a.ndim - 1) + [(0, n - a.shape[-1])])

    w1 = jnp.concatenate([r_q, r_k, r_v, m_cq, pad_last(m_kr, LANES), m_ckv, g_q, g_k, g_v,
                          pad_last(g_lr, LANES)], axis=2).astype(BF16)
    dq = MLA_NOPE + MLA_ROPE
    width = MLA_HEADS * MLA_HEAD_PAD
    wq = pad_last(w["mla_w_q_up"].reshape(DEPTH, MLA_Q_LORA, MLA_HEADS, dq), MLA_HEAD_PAD).reshape(
        DEPTH, MLA_Q_LORA, width).astype(BF16)
    kv = w["mla_w_kv_up"].reshape(DEPTH, MLA_KV_LORA, MLA_HEADS, MLA_NOPE + MLA_DV)
    wk = pad_last(kv[..., :MLA_NOPE], MLA_HEAD_PAD).reshape(DEPTH, MLA_KV_LORA, width).astype(BF16)
    wv = pad_last(kv[..., MLA_NOPE:], MLA_HEAD_PAD).reshape(DEPTH, MLA_KV_LORA, width).astype(BF16)
    place = np.zeros((1, LANES, MLA_HEAD_PAD), np.float32)
    place[0, np.arange(MLA_ROPE), MLA_NOPE + np.arange(MLA_ROPE)] = 1.0
    row = lambda a: a.reshape(a.shape[0], 1, a.shape[-1])
    return {
        "w1": w1, "wq": wq, "wk": wk, "wv": wv, "wkv": w["mla_w_kv_up"].astype(BF16), "e_head": jnp.asarray(place, BF16),
        "wgate": jnp.pad(w["gla_w_gate_up"], ((0, 0), (0, LANES - GLA_GATE_RANK), (0, 0))).astype(BF16),
        "bgate": row(w["gla_b_gate"]), "qn": row(w["mla_q_norm_g"]), "kvn": row(w["mla_kv_norm_g"]),
        "wrg": r_g.astype(BF16), "wg3": w_in[:, :, offs[11]:offs[13]].astype(BF16),
        "w_ret_o": w["w_ret_o"].astype(BF16), "w_mla_o": w["w_mla_o"].astype(BF16),
        "w_gla_o": w["w_gla_o"].astype(BF16), "w_out": w["w_out"].astype(BF16),
        "ret_gn": row(w["ret_gn_g"]), "gla_gn": row(w["gla_gn_g"]),
        "f1u": w["ffn1_up"].astype(BF16), "f1d": w["ffn1_down"].astype(BF16),
        "f2u": w["ffn2_up"].astype(BF16), "f2d": w["ffn2_down"].astype(BF16),
        "ln_g": w["ln_g"].reshape(DEPTH * 3, 1, D_MODEL), "ln_b": w["ln_b"].reshape(DEPTH * 3, 1, D_MODEL),
    }


def _layer_params(sw, layer):
    lw = {k: (v, layer if v.shape[0] == DEPTH else 0) for k, v in sw.items() if not k.startswith("ln_")}
    for k in range(3):
        lw["ln_g%d" % k] = (sw["ln_g"], layer * 3 + k)
        lw["ln_b%d" % k] = (sw["ln_b"], layer * 3 + k)
    return lw


def _rope_tables(pos, tm):
    def cos_sin(half):
        inv = ROPE_THETA ** (-jnp.arange(half, dtype=F32) / half)
        ang = pos.astype(F32)[:, None] * inv[None, :]
        return jnp.cos(ang), jnp.sin(ang)

    t = pos.shape[0]
    c32, s32 = cos_sin(RET_DK // 2)
    c16, s16 = cos_sin(MLA_ROPE // 2)
    cr_h = jnp.tile(jnp.concatenate([c32, c32], axis=1), (1, RET_HEADS))
    sr_h = jnp.tile(jnp.concatenate([-s32, s32], axis=1), (1, RET_HEADS))
    k_scale = RET_DK ** -0.5
    q_scale = (MLA_NOPE + MLA_ROPE) ** -0.5 * float(np.log2(np.e))
    zeros = lambda n: jnp.zeros((t, n), F32)
    tabs = {
        "cr": jnp.concatenate([cr_h, cr_h * k_scale], axis=1),
        "sr": jnp.concatenate([sr_h, sr_h * k_scale], axis=1),
        "cq": jnp.concatenate([jnp.ones((t, MLA_NOPE), F32), c16, c16, zeros(LANES - MLA_NOPE - MLA_ROPE)], axis=1) * q_scale,
        "sq": jnp.concatenate([zeros(MLA_NOPE), -s16, s16, zeros(LANES - MLA_NOPE - MLA_ROPE)], axis=1) * q_scale,
        "ck": jnp.concatenate([c16, c16, zeros(LANES - MLA_ROPE)], axis=1),
        "sk": jnp.concatenate([-s16, s16, zeros(LANES - MLA_ROPE)], axis=1),
    }
    if t < tm:
        tabs = {k: jnp.tile(v, (tm // t, 1)) for k, v in tabs.items()}
    return tabs


def _group_layer(x, b, t, lw, tabs, past, layer, carried):
    n = b * t
    x = _ffn_ln(x, lw["f1u"], lw["f1d"], lw["ln_g0"], lw["ln_b0"])
    rqk, rv, qm, ckv_all, krp, kr_all, gqk, gv, la, *kv = _in_proj(x, lw, tabs, layer, carried, past is None)

    log_gamma = jnp.log(1.0 - 2.0 ** (-5.0 - jnp.arange(RET_HEADS, dtype=F32)))
    ret_la = jnp.repeat(log_gamma, RET_DK)[None, :]
    s_ret0 = None if past is None else jnp.swapaxes(past[2], -1, -2)
    s_gla0 = None if past is None else jnp.swapaxes(past[3], -1, -2)
    o_r, s_retT = _scan(rqk.reshape(b, t, -1), rv.reshape(b, t, -1), ret_la, s_ret0,
                        heads=RET_HEADS, dk=RET_DK, dv=RET_DV)
    o_g, s_glaT = _scan(gqk.reshape(b, t, -1), gv.reshape(b, t, -1), la.reshape(b, t, -1), s_gla0,
                        heads=GLA_HEADS, dk=GLA_DK, dv=GLA_DV)

    qm = qm.reshape(b, t, -1)
    if past is None:
        o_m = _attention(qm, kv[0].reshape(b, t, -1), kv[1].reshape(b, t, -1), tq=min(ATT_TQ, t), tk=min(ATT_TK, t))
    else:
        o_m = _attention_cached(qm, past[0], past[1], ckv_all, krp, lw, layer)

    x = _out_proj_ln(x, o_r.reshape(n, -1), o_m.reshape(n, -1), o_g.reshape(n, -1), lw, lw["ln_g1"], lw["ln_b1"])
    x = _ffn_ln(x, lw["f2u"], lw["f2d"], lw["ln_g2"], lw["ln_b2"])
    return x, (ckv_all, kr_all), (jnp.swapaxes(s_retT, -1, -2), jnp.swapaxes(s_glaT, -1, -2))


def kernel(x_prompt, x_sample, cache_mla_ckv, cache_mla_krope, state_ret, state_gla, w_in, ret_gn_g, mla_q_norm_g, mla_w_q_up, mla_kv_norm_g, mla_w_kv_up, gla_w_gate_up, gla_b_gate, gla_gn_g, w_ret_o, w_mla_o, w_gla_o, w_out, ffn1_up, ffn1_down, ffn2_up, ffn2_down, ln_g, ln_b):
    w = dict(w_in=w_in, ret_gn_g=ret_gn_g, mla_q_norm_g=mla_q_norm_g, mla_w_q_up=mla_w_q_up,
             mla_kv_norm_g=mla_kv_norm_g, mla_w_kv_up=mla_w_kv_up, gla_w_gate_up=gla_w_gate_up,
             gla_b_gate=gla_b_gate, gla_gn_g=gla_gn_g, w_ret_o=w_ret_o, w_mla_o=w_mla_o, w_gla_o=w_gla_o,
             w_out=w_out, ffn1_up=ffn1_up, ffn1_down=ffn1_down, ffn2_up=ffn2_up, ffn2_down=ffn2_down,
             ln_g=ln_g, ln_b=ln_b)
    bp, tp, _ = x_prompt.shape
    bs, ts, _ = x_sample.shape
    t_past = cache_mla_ckv.shape[2]
    assert t_past % CHUNK == 0 and tp % CHUNK == 0 and ts % CHUNK == 0
    tabs_p = _rope_tables(jnp.arange(tp), min(INP_TM, bp * tp))
    tabs_s = _rope_tables(t_past + jnp.arange(ts), min(INP_TM, bs * ts))
    xp = x_prompt.reshape(bp * tp, D_MODEL)
    xs = x_sample.reshape(bs * ts, D_MODEL)
    carried_p = carried_s = None
    st_p, st_s = [], []
    sw = _prep_weights(w)
    for l in range(DEPTH):
        lw = _layer_params(sw, l)
        xp, carried_p, st = _group_layer(xp, bp, tp, lw, tabs_p, None, l, carried_p)
        st_p.append(st)
        past = (cache_mla_ckv, cache_mla_krope, state_ret[l], state_gla[l])
        xs, carried_s, st = _group_layer(xs, bs, ts, lw, tabs_s, past, l, carried_s)
        st_s.append(st)
    stack = lambda sts, i: jnp.stack([s[i] for s in sts])
    return (xp.reshape(bp, tp, D_MODEL), xs.reshape(bs, ts, D_MODEL),
            carried_p[0].reshape(DEPTH, bp, tp, -1), carried_p[1].reshape(DEPTH, bp, tp, -1),
            stack(st_p, 0), stack(st_p, 1),
            carried_s[0].reshape(DEPTH, bs, ts, -1), carried_s[1].reshape(DEPTH, bs, ts, -1),
            stack(st_s, 0), stack(st_s, 1))
```

```python
import functools

import numpy as np
import jax
import jax.numpy as jnp
from jax import lax
from jax.experimental import pallas as pl
from jax.experimental.pallas import tpu as pltpu

F32 = jnp.float32
BF16 = jnp.bfloat16

D_MODEL = 1024
DEPTH = 2
CHUNK = 64
CHUNK_SHIFT = 6
ALPHA = (2 * DEPTH) ** 0.25
EPS = 1e-5
ROPE_THETA = 10000.0
RET_HEADS, RET_DK, RET_DV = 4, 64, 128
MLA_HEADS, MLA_Q_LORA, MLA_KV_LORA, MLA_NOPE, MLA_ROPE, MLA_DV = 8, 384, 256, 64, 32, 64
GLA_HEADS, GLA_DK, GLA_DV, GLA_GATE_RANK, GLA_TAU = 4, 128, 256, 16, 16.0
D_FF = 2816
N_BRANCH = 3
IN_SPLITS = (RET_HEADS * RET_DK, RET_HEADS * RET_DK, RET_HEADS * RET_DV, RET_HEADS * RET_DV,
             MLA_Q_LORA, MLA_KV_LORA, MLA_ROPE,
             GLA_HEADS * GLA_DK, GLA_HEADS * GLA_DK, GLA_HEADS * GLA_DV, GLA_GATE_RANK, GLA_HEADS * GLA_DV,
             N_BRANCH * D_MODEL)

LANES = 128
MLA_HEAD_PAD = LANES
VMEM_LIMIT = 56 * 1024 * 1024

_C_RQK, _C_RV, _C_CQ, _C_KR, _C_CKV, _C_GQK, _C_GV, _C_LR, _C_END = (
    0, 512, 1024, 1408, 1536, 1792, 2816, 3840, 3968)
_G_GOG, _G_BR = 0, 1024

_NT = (((1,), (1,)), ((), ()))
_TN = (((0,), (0,)), ((), ()))


def _params(*sem):
    return pltpu.CompilerParams(dimension_semantics=sem, vmem_limit_bytes=VMEM_LIMIT)


def _dot(a, b):
    return jnp.dot(a, b, preferred_element_type=F32)


def _layer_norm(z, g, b):
    mu = jnp.mean(z, axis=-1, keepdims=True)
    zc = z - mu
    var = jnp.mean(zc * zc, axis=-1, keepdims=True)
    return zc * lax.rsqrt(var + EPS) * g + b


def _row_spec(tm, cols):
    return pl.BlockSpec((tm, cols), lambda i: (i, 0))


def _full_spec(shape):
    return pl.BlockSpec(shape, lambda *_: (0,) * len(shape), pipeline_mode=pl.Buffered(1))


def _wspec(param):
    arr, idx = param
    return pl.BlockSpec((None,) + arr.shape[1:], lambda *_: (idx,) + (0,) * (arr.ndim - 1), pipeline_mode=pl.Buffered(1))


FFN_TM = 1024
FFN_SUB = 512
FFN_FC = 256


def _ffn_ln_kernel(x_ref, wup_ref, wd_ref, g_ref, b_ref, o_ref, h_ref):
    tm = x_ref.shape[0]
    sub = min(FFN_SUB, tm)
    for r in range(tm // sub):
        rs = slice(r * sub, (r + 1) * sub)
        x = x_ref[rs, :]
        xb = x.astype(BF16)
        for c in range(D_FF // FFN_FC):
            sl = slice(c * FFN_FC, (c + 1) * FFN_FC)
            gate = _dot(xb, wup_ref[:, sl])
            up = _dot(xb, wup_ref[:, D_FF + c * FFN_FC:D_FF + (c + 1) * FFN_FC])
            h_ref[rs, sl] = (gate * jax.nn.sigmoid(gate) * up).astype(BF16)
        y = _dot(h_ref[rs, :], wd_ref[...])
        o_ref[rs, :] = _layer_norm(ALPHA * x + 0.5 * y, g_ref[...], b_ref[...])


def _ffn_ln(x, wup, wd, g, b):
    n = x.shape[0]
    tm = min(FFN_TM, n)
    params = [wup, wd, g, b]
    return pl.pallas_call(
        _ffn_ln_kernel,
        grid=(n // tm,),
        in_specs=[_row_spec(tm, D_MODEL)] + [_wspec(p) for p in params],
        out_specs=_row_spec(tm, D_MODEL),
        out_shape=jax.ShapeDtypeStruct((n, D_MODEL), F32),
        scratch_shapes=[pltpu.VMEM((tm, D_FF), BF16)],
        compiler_params=_params("parallel"),
        name="ffn_ln",
    )(x, *[p[0] for p in params])


INP_TM = 512


def _swap_halves(x, first_mask, half):
    return jnp.where(first_mask, pltpu.roll(x, LANES - half, 1), pltpu.roll(x, half, 1))


def _in_proj_kernel(x_ref, w_ref, wq_ref, wgate_ref, bgate_ref, qn_ref, kvn_ref,
                    cr_ref, sr_ref, cq_ref, sq_ref, ck_ref, sk_ref, *rest, n_kv_out):
    rqk_ref, rv_ref, qm_ref, ckv_ref, kr_ref, kr_out_ref, gqk_ref, gv_ref, la_ref = rest[-9 - n_kv_out:][:9]
    xb = x_ref[...].astype(BF16)
    tm = xb.shape[0]
    lane = lax.broadcasted_iota(jnp.int32, (tm, LANES), 1)
    ghd = GLA_HEADS * GLA_DK

    h_rqk = _dot(xb, w_ref[:, _C_RQK:_C_RV])
    h_cq_kr = _dot(xb, w_ref[:, _C_CQ:_C_CKV])

    ret_first = (lane & (RET_DK - 1)) < RET_DK // 2
    for c in range(2 * RET_HEADS * RET_DK // LANES):
        sl = slice(c * LANES, (c + 1) * LANES)
        h = h_rqk[:, sl]
        rqk_ref[:, sl] = h * cr_ref[:, sl] + _swap_halves(h, ret_first, RET_DK // 2) * sr_ref[:, sl]
    rv_ref[...] = _dot(xb, w_ref[:, _C_RV:_C_CQ]).astype(BF16)

    hq = h_cq_kr[:, :MLA_Q_LORA]
    cq = hq * lax.rsqrt(jnp.mean(hq * hq, axis=-1, keepdims=True) + EPS) * qn_ref[...]
    q_up = _dot(cq.astype(BF16), wq_ref[...])
    hkv = _dot(xb, w_ref[:, _C_CKV:_C_GQK])
    gv_ref[...] = _dot(xb, w_ref[:, _C_GV:_C_LR]).astype(BF16)
    q_first = lane < MLA_NOPE + MLA_ROPE // 2
    for h_i in range(MLA_HEADS):
        sl = slice(h_i * MLA_HEAD_PAD, (h_i + 1) * MLA_HEAD_PAD)
        qh = q_up[:, sl]
        qm_ref[:, sl] = (qh * cq_ref[...] + _swap_halves(qh, q_first, MLA_ROPE // 2) * sq_ref[...]).astype(BF16)

    hkr = h_cq_kr[:, MLA_Q_LORA:]
    kr = hkr * ck_ref[...] + _swap_halves(hkr, lane < MLA_ROPE // 2, MLA_ROPE // 2) * sk_ref[...]
    kr_ref[...] = kr
    kr_out_ref[...] = kr[:, :MLA_ROPE]
    ckv = hkv * lax.rsqrt(jnp.mean(hkv * hkv, axis=-1, keepdims=True) + EPS) * kvn_ref[...]
    ckv_ref[...] = ckv
    lr = _dot(xb, w_ref[:, _C_LR:_C_END]).astype(BF16)
    gla_q = _dot(xb, w_ref[:, _C_GQK:_C_GQK + ghd])
    if n_kv_out:
        wkv_ref = rest[0]
        k_ref, v_ref = rest[-2:]
        kv = _dot(ckv.astype(BF16), wkv_ref[...])
        kr_at_rope = pltpu.roll(kr, MLA_NOPE, 1)
        ones_cols = jnp.where(lane >= MLA_DV, 1.0, 0.0)
        for h_i in range(MLA_HEADS):
            sl = slice(h_i * MLA_HEAD_PAD, (h_i + 1) * MLA_HEAD_PAD)
            g = kv[:, sl]
            k_ref[:, sl] = jnp.where(lane < MLA_NOPE, g, kr_at_rope).astype(BF16)
            v_ref[:, sl] = jnp.where(lane < MLA_DV, pltpu.roll(g, LANES - MLA_NOPE, 1), ones_cols).astype(BF16)

    gqk_ref[:, :ghd] = gla_q * (GLA_DK ** -0.5)
    logit = _dot(lr, wgate_ref[...]) + bgate_ref[...]
    gqk_ref[:, ghd:] = _dot(xb, w_ref[:, _C_GQK + ghd:_C_GV])
    log_sig = jnp.minimum(logit, 0.0) - jnp.log1p(jnp.exp(-jnp.abs(logit)))
    la_ref[...] = log_sig / GLA_TAU


def _in_proj(x, lw, tabs, layer, carried, emit_kv):
    n = x.shape[0]
    tm = min(INP_TM, n)
    period = tabs["cr"].shape[0] // tm

    def tab_spec(cols):
        return pl.BlockSpec((tm, cols), lambda i: (i % period, 0))

    def layer_spec(cols):
        return pl.BlockSpec((None, tm, cols), lambda i: (layer, i, 0))

    rows = lambda cols, dtype: (_row_spec(tm, cols), jax.ShapeDtypeStruct((n, cols), dtype))
    stacked = lambda cols: (layer_spec(cols), jax.ShapeDtypeStruct((DEPTH, n, cols), F32))
    outs = [rows(2 * RET_HEADS * RET_DK, F32), rows(RET_HEADS * RET_DV, BF16), rows(MLA_HEADS * MLA_HEAD_PAD, BF16),
            stacked(MLA_KV_LORA), rows(LANES, F32), stacked(MLA_ROPE), rows(2 * GLA_HEADS * GLA_DK, F32),
            rows(GLA_HEADS * GLA_DV, BF16), rows(GLA_HEADS * GLA_DK, F32)]
    weights = [lw["w1"], lw["wq"], lw["wgate"], lw["bgate"], lw["qn"], lw["kvn"]]
    tables = [tabs["cr"], tabs["sr"], tabs["cq"], tabs["sq"], tabs["ck"], tabs["sk"]]
    in_specs = ([_row_spec(tm, D_MODEL)] + [_wspec(p) for p in weights]
                + [tab_spec(t.shape[1]) for t in tables])
    args = [x, *[p[0] for p in weights], *tables]
    if emit_kv:
        kv_weights = [lw["wkv"]]
        in_specs += [_wspec(p) for p in kv_weights]
        args += [p[0] for p in kv_weights]
        outs += [rows(MLA_HEADS * MLA_HEAD_PAD, BF16)] * 2
    aliases = {}
    if carried is not None:
        aliases = {len(args): 3, len(args) + 1: 5}
        in_specs += [pl.BlockSpec(memory_space=pl.ANY)] * 2
        args += list(carried)
    return pl.pallas_call(
        functools.partial(_in_proj_kernel, n_kv_out=2 if emit_kv else 0),
        grid=(n // tm,),
        in_specs=in_specs,
        out_specs=[o[0] for o in outs],
        out_shape=[o[1] for o in outs],
        input_output_aliases=aliases,
        compiler_params=_params("parallel"),
        name="in_proj",
    )(*args)


SCAN_CHUNKS_PER_STEP = 4
SCAN_STREAMS_PER_STEP = 4


def _cumsum_rows(a):
    rows = lax.broadcasted_iota(jnp.int32, a.shape, 0)
    s = 1
    while s < a.shape[0]:
        a = a + jnp.where(rows >= s, pltpu.roll(a, s, 0), 0.0)
        s *= 2
    return a


def _scan_kernel(*refs, heads, dk, dv, n_chunks, nb, has_la, has_s0):
    it = iter(refs)
    qk_ref, v_ref = next(it), next(it)
    la_ref = next(it)
    s0_ref = next(it) if has_s0 else None
    o_ref, sT_ref, st_ref = next(it), next(it), next(it)
    hd = heads * dk
    step = pl.program_id(1)

    @pl.when(step == 0)
    def _():
        if has_s0:
            st_ref[...] = s0_ref[...]
        else:
            st_ref[...] = jnp.zeros_like(st_ref)

    row = lax.broadcasted_iota(jnp.int32, (CHUNK, CHUNK), 0)
    col = lax.broadcasted_iota(jnp.int32, (CHUNK, CHUNK), 1)
    causal = row >= col
    ksl = lambda h: slice(h * dk, (h + 1) * dk)
    vsl = lambda h: slice(h * dv, (h + 1) * dv)

    def decayed_operands(bi, rows):
        if has_la:
            bc = _cumsum_rows(la_ref[bi, rows, :])
        else:
            steps = lax.broadcasted_iota(jnp.int32, (CHUNK, hd), 0) + 1
            bc = steps.astype(F32) * la_ref[...]
        bl = bc[CHUNK - 1:CHUNK, :]
        q = qk_ref[bi, rows, :hd]
        k = qk_ref[bi, rows, hd:]
        return ((q * jnp.exp(bc)).astype(BF16), (k * jnp.exp(-bc)).astype(BF16),
                (k * jnp.exp(bl - bc)).astype(BF16), jnp.exp(bl))

    def advance(chains, ops, rows):
        att = [lax.dot_general(ops[bi][0][:, ksl(h)], ops[bi][1][:, ksl(h)], _NT, preferred_element_type=F32)
               for bi, h in chains]
        cross = [lax.dot_general(ops[bi][0][:, ksl(h)], st_ref[bi, h].astype(BF16), _NT, preferred_element_type=F32)
                 for bi, h in chains]
        upd = [lax.dot_general(v_ref[bi, rows, vsl(h)], ops[bi][2][:, ksl(h)], _TN, preferred_element_type=F32)
               for bi, h in chains]
        for n, (bi, h) in enumerate(chains):
            a = jnp.where(causal, att[n], 0.0).astype(BF16)
            o_ref[bi, rows, vsl(h)] = _dot(a, v_ref[bi, rows, vsl(h)]) + cross[n]
        for n, (bi, h) in enumerate(chains):
            st_ref[bi, h] = st_ref[bi, h] * ops[bi][3][:, ksl(h)] + upd[n]

    def chunk(c, carry):
        rows = pl.ds(pl.multiple_of(c * CHUNK, CHUNK), CHUNK)
        if has_la:
            ops = {0: decayed_operands(0, rows)}
            for bi in range(nb):
                if bi + 1 < nb:
                    ops[bi + 1] = decayed_operands(bi + 1, rows)
                advance([(bi, h) for h in range(heads)], ops, rows)
        else:
            ops = {bi: decayed_operands(bi, rows) for bi in range(nb)}
            advance([(bi, h) for h in range(heads) for bi in range(nb)], ops, rows)
        return carry

    lax.fori_loop(0, n_chunks, chunk, 0)

    @pl.when(step == pl.num_programs(1) - 1)
    def _():
        sT_ref[...] = st_ref[...]


def _scan(qk, v, la, s0T, *, heads, dk, dv):
    b, t, _ = qk.shape
    has_la = la.ndim == 3
    has_s0 = s0T is not None
    nb = min(SCAN_STREAMS_PER_STEP, b)
    ncs = min(SCAN_CHUNKS_PER_STEP, t // CHUNK)
    rows = ncs * CHUNK
    hd, hv = heads * dk, heads * dv

    def seq_spec(cols):
        return pl.BlockSpec((nb, rows, cols), lambda bi, si: (bi, si, 0))

    st_spec = pl.BlockSpec((nb, heads, dv, dk), lambda bi, si: (bi, 0, 0, 0))
    in_specs = [seq_spec(2 * hd), seq_spec(hv), seq_spec(hd) if has_la else _full_spec(la.shape)]
    args = [qk, v, la]
    if has_s0:
        in_specs.append(st_spec)
        args.append(s0T)
    kern = functools.partial(_scan_kernel, heads=heads, dk=dk, dv=dv, n_chunks=ncs, nb=nb,
                             has_la=has_la, has_s0=has_s0)
    return pl.pallas_call(
        kern,
        grid=(b // nb, t // rows),
        in_specs=in_specs,
        out_specs=[seq_spec(hv), st_spec],
        out_shape=[jax.ShapeDtypeStruct((b, t, hv), F32), jax.ShapeDtypeStruct((b, heads, dv, dk), F32)],
        scratch_shapes=[pltpu.VMEM((nb, heads, dv, dk), F32)],
        compiler_params=_params("parallel", "arbitrary"),
        name="scan_h%d_dk%d" % (heads, dk),
    )(*args)


ATT_TQ = 1024
ATT_TK = 512
ATT_SUB = 512
ATT_LOOKAHEAD = 2


def _attn_kernel(i_ref, j_ref, q_ref, k_ref, v_ref, o_ref, m_ref, acc_ref, *, tq, tk, nk):
    i, j = i_ref[pl.program_id(1)], j_ref[pl.program_id(1)]
    n_sub = tq // ATT_SUB
    q_lo = [(i * tq + r * ATT_SUB) // CHUNK for r in range(n_sub)]
    q_hi = [(i * tq + (r + 1) * ATT_SUB - 1) // CHUNK for r in range(n_sub)]
    k_lo = (j * tk) // CHUNK
    k_hi = (j * tk + tk - 1) // CHUNK
    j_last = jnp.minimum(nk - 1, ((q_hi[-1] + 1) * CHUNK - 1) // tk)

    @pl.when(j == 0)
    def _():
        m_ref[...] = jnp.full_like(m_ref, -jnp.inf)
        acc_ref[...] = jnp.zeros_like(acc_ref)

    def tile(modes):
        vis = {}
        for r, mode in enumerate(modes):
            if mode == "masked":
                row0 = i * tq + r * ATT_SUB
                qc = (row0 + lax.broadcasted_iota(jnp.int32, (ATT_SUB, tk), 0)) >> CHUNK_SHIFT
                kc = (j * tk + lax.broadcasted_iota(jnp.int32, (ATT_SUB, tk), 1)) >> CHUNK_SHIFT
                vis[r] = kc <= qc
        work = [(h, r) for h in range(MLA_HEADS) for r, mode in enumerate(modes) if mode != "skip"]

        def scores(h, r):
            hs = slice(h * MLA_HEAD_PAD, (h + 1) * MLA_HEAD_PAD)
            return lax.dot_general(q_ref[0, r * ATT_SUB:(r + 1) * ATT_SUB, hs], k_ref[0, :, hs], _NT,
                                   preferred_element_type=F32)

        pending = [scores(*w) for w in work[:ATT_LOOKAHEAD]]
        for n, (h, r) in enumerate(work):
            hs = slice(h * MLA_HEAD_PAD, (h + 1) * MLA_HEAD_PAD)
            rs = slice(r * ATT_SUB, (r + 1) * ATT_SUB)
            s = pending.pop(0)
            if n + ATT_LOOKAHEAD < len(work):
                pending.append(scores(*work[n + ATT_LOOKAHEAD]))
            if modes[r] == "masked":
                s = jnp.where(vis[r], s, -jnp.inf)
            m_prev = m_ref[h, rs, :]
            m_new = jnp.maximum(m_prev, jnp.max(s, axis=-1, keepdims=True))
            alpha = jnp.exp2(m_prev - m_new)
            p = jnp.concatenate([jnp.exp2(s[:, c * LANES:(c + 1) * LANES] - m_new).astype(BF16)
                                 for c in range(tk // LANES)], axis=1)
            acc_ref[h, rs, :] = alpha * acc_ref[h, rs, :] + _dot(p, v_ref[0, :, hs])
            m_ref[h, rs, :] = m_new

    def cond(r, mode):
        if mode == "full":
            return k_hi <= q_lo[r]
        if mode == "masked":
            return jnp.logical_and(k_hi > q_lo[r], k_lo <= q_hi[r])
        return k_lo > q_hi[r]

    combos = [("full",) * n_sub] + [("skip",) * r + ("masked",) + ("full",) * (n_sub - r - 1) for r in range(n_sub)]
    for modes in combos:
        pred = functools.reduce(jnp.logical_and, [cond(r, mode) for r, mode in enumerate(modes)])
        pl.when(pred)(functools.partial(tile, modes))

    @pl.when(j == j_last)
    def _():
        first = lax.broadcasted_iota(jnp.int32, (tq, LANES), 1) < MLA_DV
        for pair in range(MLA_HEADS // 2):
            a0, a1 = acc_ref[2 * pair], acc_ref[2 * pair + 1]
            even = a0 / pltpu.roll(a0, MLA_DV, 1)
            odd = pltpu.roll(a1, MLA_DV, 1) / a1
            o_ref[0, :, pair * LANES:(pair + 1) * LANES] = jnp.where(first, even, odd).astype(o_ref.dtype)


def _attention(q, k, v, *, tq, tk):
    b, t_q, _ = q.shape
    t_k = k.shape[1]
    assert tq % ATT_SUB == 0 and t_q % tq == 0 and t_k % tk == 0
    nq, nk = t_q // tq, t_k // tk
    width = MLA_HEADS * MLA_HEAD_PAD
    pairs = [(i, j) for i in range(nq)
             for j in range(min(nk - 1, (((i * tq + tq - 1) // CHUNK + 1) * CHUNK - 1) // tk) + 1)]
    i_tab = jnp.asarray([p[0] for p in pairs], jnp.int32)
    j_tab = jnp.asarray([p[1] for p in pairs], jnp.int32)

    kern = functools.partial(_attn_kernel, tq=tq, tk=tk, nk=nk)
    grid_spec = pltpu.PrefetchScalarGridSpec(
        num_scalar_prefetch=2,
        grid=(b, len(pairs)),
        in_specs=[pl.BlockSpec((1, tq, width), lambda bi, p, it, jt: (bi, it[p], 0)),
                  pl.BlockSpec((1, tk, width), lambda bi, p, it, jt: (bi, jt[p], 0)),
                  pl.BlockSpec((1, tk, width), lambda bi, p, it, jt: (bi, jt[p], 0))],
        out_specs=pl.BlockSpec((1, tq, MLA_HEADS * MLA_DV), lambda bi, p, it, jt: (bi, it[p], 0)),
        scratch_shapes=[pltpu.VMEM((MLA_HEADS, tq, LANES), F32), pltpu.VMEM((MLA_HEADS, tq, MLA_HEAD_PAD), F32)],
    )
    return pl.pallas_call(
        kern,
        grid_spec=grid_spec,
        out_shape=jax.ShapeDtypeStruct((b, t_q, MLA_HEADS * MLA_DV), BF16),
        compiler_params=_params("parallel", "arbitrary"),
        name="mla_attention",
    )(i_tab, j_tab, q, k, v)


CACHED_GROUPS = 2


def _attn_cached_kernel(q_ref, cp_ref, krp_ref, cn_ref, krn_ref, wk_ref, e_ref, wv_ref, o_ref):
    t_new = q_ref.shape[1]
    hsl = lambda h: slice(h * MLA_HEAD_PAD, (h + 1) * MLA_HEAD_PAD)
    ckv_p = cp_ref[...].astype(BF16)
    ckv_n = cn_ref[...].astype(BF16)
    kr_p = _dot(krp_ref[...].astype(BF16), e_ref[:MLA_ROPE, :]).astype(BF16)
    kr_n = _dot(krn_ref[...].astype(BF16), e_ref[...]).astype(BF16)
    q_all = jnp.concatenate([q_ref[0, :, hsl(h)] for h in range(MLA_HEADS)], axis=0)
    q_lat = jnp.concatenate(
        [lax.dot_general(q_ref[0, :, hsl(h)], wk_ref[:, hsl(h)], _NT, preferred_element_type=F32)
         for h in range(MLA_HEADS)], axis=0).astype(BF16)
    q_chunk = jnp.concatenate([lax.broadcasted_iota(jnp.int32, (t_new, t_new), 0) >> CHUNK_SHIFT] * MLA_HEADS, axis=0)
    k_chunk = lax.broadcasted_iota(jnp.int32, (MLA_HEADS * t_new, t_new), 1) >> CHUNK_SHIFT
    vis_new = k_chunk <= q_chunk
    n_rows = MLA_HEADS * t_new
    grp = n_rows // CACHED_GROUPS

    def scores(g):
        rs = slice(g * grp, (g + 1) * grp)
        s_past = (lax.dot_general(q_lat[rs], ckv_p, _NT, preferred_element_type=F32)
                  + lax.dot_general(q_all[rs], kr_p, _NT, preferred_element_type=F32))
        s_new = (lax.dot_general(q_lat[rs], ckv_n, _NT, preferred_element_type=F32)
                 + lax.dot_general(q_all[rs], kr_n, _NT, preferred_element_type=F32))
        return s_past, jnp.where(vis_new[rs], s_new, -jnp.inf)

    pending = scores(0)
    o_lat = []
    for g in range(CACHED_GROUPS):
        s_past, s_new = pending
        if g + 1 < CACHED_GROUPS:
            pending = scores(g + 1)
        m = jnp.maximum(jnp.max(s_past, axis=-1, keepdims=True), jnp.max(s_new, axis=-1, keepdims=True))
        p_past = jnp.exp2(s_past - m)
        p_new = jnp.exp2(s_new - m)
        denom = jnp.sum(p_past, axis=-1, keepdims=True) + jnp.sum(p_new, axis=-1, keepdims=True)
        o_lat.append(((_dot(p_past.astype(BF16), ckv_p) + _dot(p_new.astype(BF16), ckv_n)) / denom).astype(BF16))
    o_lat = jnp.concatenate(o_lat, axis=0)
    for h in range(MLA_HEADS):
        o = _dot(o_lat[h * t_new:(h + 1) * t_new], wv_ref[:, hsl(h)])
        o_ref[0, :, h * MLA_DV:(h + 1) * MLA_DV] = o[:, :MLA_DV].astype(o_ref.dtype)


def _attention_cached(q, cache_ckv, cache_kr, ckv_all, krp, lw, layer):
    b, t, width = q.shape
    t_past = cache_ckv.shape[2]
    weights = [lw["wk"], lw["e_head"], lw["wv"]]
    return pl.pallas_call(
        _attn_cached_kernel,
        grid=(b,),
        in_specs=[pl.BlockSpec((1, t, width), lambda bi: (bi, 0, 0)),
                  pl.BlockSpec((t_past, MLA_KV_LORA), lambda bi: (layer * b + bi, 0)),
                  pl.BlockSpec((t_past, MLA_ROPE), lambda bi: (layer * b + bi, 0)),
                  pl.BlockSpec((t, MLA_KV_LORA), lambda bi: (layer * b + bi, 0)),
                  pl.BlockSpec((t, LANES), lambda bi: (bi, 0))] + [_wspec(p) for p in weights],
        out_specs=pl.BlockSpec((1, t, MLA_HEADS * MLA_DV), lambda bi: (bi, 0, 0)),
        out_shape=jax.ShapeDtypeStruct((b, t, MLA_HEADS * MLA_DV), BF16),
        compiler_params=_params("parallel"),
        name="mla_attention_cached",
    )(q, cache_ckv.reshape(-1, MLA_KV_LORA), cache_kr.reshape(-1, MLA_ROPE), ckv_all.reshape(-1, MLA_KV_LORA),
      krp, *[p[0] for p in weights])


OUT_TM = 1024
OUT_SUB = 256


def _out_proj_kernel(x_ref, or_ref, om_ref, og_ref, wrg_ref, wg3_ref, wro_ref, wmo_ref, wgo_ref, wout_ref,
                     rgn_ref, ggn_ref, g_ref, b_ref, o_ref, hr_ref, hg_ref):
    tm = x_ref.shape[0]
    sub = min(OUT_SUB, tm)
    for r in range(tm // sub):
        rs = slice(r * sub, (r + 1) * sub)
        x = x_ref[rs, :]
        xb = x.astype(BF16)

        ret_gate = _dot(xb, wrg_ref[...])
        gla_gate = [_dot(xb, wg3_ref[:, _G_GOG + h * GLA_DV:_G_GOG + (h + 1) * GLA_DV]) for h in range(GLA_HEADS)]

        for h in range(RET_HEADS):
            sl = slice(h * RET_DV, (h + 1) * RET_DV)
            gate = ret_gate[:, sl]
            o = or_ref[rs, sl]
            oc = o - jnp.mean(o, axis=-1, keepdims=True)
            normed = oc * lax.rsqrt(jnp.mean(oc * oc, axis=-1, keepdims=True) + EPS) * rgn_ref[:, sl]
            hr_ref[rs, sl] = (normed * (gate * jax.nn.sigmoid(gate))).astype(BF16)
        y_m = _dot(om_ref[rs, :], wmo_ref[...])
        gate_m = _dot(xb, wg3_ref[:, _G_BR + D_MODEL:_G_BR + 2 * D_MODEL])

        for h in range(GLA_HEADS):
            sl = slice(h * GLA_DV, (h + 1) * GLA_DV)
            gate = gla_gate[h]
            o = og_ref[rs, sl]
            normed = o * lax.rsqrt(jnp.mean(o * o, axis=-1, keepdims=True) + EPS) * ggn_ref[:, sl]
            hg_ref[rs, sl] = (normed * (gate * jax.nn.sigmoid(gate))).astype(BF16)
        y_r = _dot(hr_ref[rs, :], wro_ref[...])
        gate_r = _dot(xb, wg3_ref[:, _G_BR:_G_BR + D_MODEL])
        mix = jax.nn.sigmoid(gate_m) * y_m + jax.nn.sigmoid(gate_r) * y_r
        y_g = _dot(hg_ref[rs, :], wgo_ref[...])
        gate_g = _dot(xb, wg3_ref[:, _G_BR + 2 * D_MODEL:_G_BR + 3 * D_MODEL])
        mix = mix + jax.nn.sigmoid(gate_g) * y_g
        y = _dot(mix.astype(BF16), wout_ref[...])
        o_ref[rs, :] = _layer_norm(ALPHA * x + y, g_ref[...], b_ref[...])


def _out_proj_ln(x, o_r, o_m, o_g, lw, g, b):
    n = x.shape[0]
    tm = min(OUT_TM, n)
    weights = [lw["wrg"], lw["wg3"], lw["w_ret_o"], lw["w_mla_o"], lw["w_gla_o"], lw["w_out"],
               lw["ret_gn"], lw["gla_gn"], g, b]
    return pl.pallas_call(
        _out_proj_kernel,
        grid=(n // tm,),
        in_specs=[_row_spec(tm, D_MODEL), _row_spec(tm, RET_HEADS * RET_DV), _row_spec(tm, MLA_HEADS * MLA_DV),
                  _row_spec(tm, GLA_HEADS * GLA_DV)] + [_wspec(p) for p in weights],
        out_specs=_row_spec(tm, D_MODEL),
        out_shape=jax.ShapeDtypeStruct((n, D_MODEL), F32),
        scratch_shapes=[pltpu.VMEM((tm, RET_HEADS * RET_DV), BF16), pltpu.VMEM((tm, GLA_HEADS * GLA_DV), BF16)],
        compiler_params=_params("parallel"),
        name="out_proj_ln",
    )(x, o_r, o_m, o_g, *[p[0] for p in weights])


def _prep_weights(w):
    offs = np.cumsum((0,) + IN_SPLITS)
    w_in = w["w_in"]
    (r_q, r_k, r_v, r_g, m_cq, m_ckv, m_kr, g_q, g_k, g_v, g_lr, g_og, br) = [
        w_in[:, :, offs[i]:offs[i + 1]] for i in range(len(IN_SPLITS))]

    def pad_last(a, n):
        return jnp.pad(a, [(0, 0)] * (a.ndim - 1) + [(0, n - a.shape[-1])])

    w1 = jnp.concatenate([r_q, r_k, r_v, m_cq, pad_last(m_kr, LANES), m_ckv, g_q, g_k, g_v,
                          pad_last(g_lr, LANES)], axis=2).astype(BF16)
    dq = MLA_NOPE + MLA_ROPE
    width = MLA_HEADS * MLA_HEAD_PAD
    wq = pad_last(w["mla_w_q_up"].reshape(DEPTH, MLA_Q_LORA, MLA_HEADS, dq), MLA_HEAD_PAD).reshape(
        DEPTH, MLA_Q_LORA, width).astype(BF16)
    kv = w["mla_w_kv_up"].reshape(DEPTH, MLA_KV_LORA, MLA_HEADS, MLA_NOPE + MLA_DV)
    wk = pad_last(kv[..., :MLA_NOPE], MLA_HEAD_PAD).reshape(DEPTH, MLA_KV_LORA, width).astype(BF16)
    wv = pad_last(kv[..., MLA_NOPE:], MLA_HEAD_PAD).reshape(DEPTH, MLA_KV_LORA, width).astype(BF16)
    place = np.zeros((1, LANES, MLA_HEAD_PAD), np.float32)
    place[0, np.arange(MLA_ROPE), MLA_NOPE + np.arange(MLA_ROPE)] = 1.0
    row = lambda a: a.reshape(a.shape[0], 1, a.shape[-1])
    return {
        "w1": w1, "wq": wq, "wk": wk, "wv": wv, "wkv": w["mla_w_kv_up"].astype(BF16), "e_head": jnp.asarray(place, BF16),
        "wgate": jnp.pad(w["gla_w_gate_up"], ((0, 0), (0, LANES - GLA_GATE_RANK), (0, 0))).astype(BF16),
        "bgate": row(w["gla_b_gate"]), "qn": row(w["mla_q_norm_g"]), "kvn": row(w["mla_kv_norm_g"]),
        "wrg": r_g.astype(BF16), "wg3": w_in[:, :, offs[11]:offs[13]].astype(BF16),
        "w_ret_o": w["w_ret_o"].astype(BF16), "w_mla_o": w["w_mla_o"].astype(BF16),
        "w_gla_o": w["w_gla_o"].astype(BF16), "w_out": w["w_out"].astype(BF16),
        "ret_gn": row(w["ret_gn_g"]), "gla_gn": row(w["gla_gn_g"]),
        "f1u": w["ffn1_up"].astype(BF16), "f1d": w["ffn1_down"].astype(BF16),
        "f2u": w["ffn2_up"].astype(BF16), "f2d": w["ffn2_down"].astype(BF16),
        "ln_g": w["ln_g"].reshape(DEPTH * 3, 1, D_MODEL), "ln_b": w["ln_b"].reshape(DEPTH * 3, 1, D_MODEL),
    }


def _layer_params(sw, layer):
    lw = {k: (v, layer if v.shape[0] == DEPTH else 0) for k, v in sw.items() if not k.startswith("ln_")}
    for k in range(3):
        lw["ln_g%d" % k] = (sw["ln_g"], layer * 3 + k)
        lw["ln_b%d" % k] = (sw["ln_b"], layer * 3 + k)
    return lw


def _rope_tables(pos, tm):
    def cos_sin(half):
        inv = ROPE_THETA ** (-jnp.arange(half, dtype=F32) / half)
        ang = pos.astype(F32)[:, None] * inv[None, :]
        return jnp.cos(ang), jnp.sin(ang)

    t = pos.shape[0]
    c32, s32 = cos_sin(RET_DK // 2)
    c16, s16 = cos_sin(MLA_ROPE // 2)
    cr_h = jnp.tile(jnp.concatenate([c32, c32], axis=1), (1, RET_HEADS))
    sr_h = jnp.tile(jnp.concatenate([-s32, s32], axis=1), (1, RET_HEADS))
    k_scale = RET_DK ** -0.5
    q_scale = (MLA_NOPE + MLA_ROPE) ** -0.5 * float(np.log2(np.e))
    zeros = lambda n: jnp.zeros((t, n), F32)
    tabs = {
        "cr": jnp.concatenate([cr_h, cr_h * k_scale], axis=1),
        "sr": jnp.concatenate([sr_h, sr_h * k_scale], axis=1),
        "cq": jnp.concatenate([jnp.ones((t, MLA_NOPE), F32), c16, c16, zeros(LANES - MLA_NOPE - MLA_ROPE)], axis=1) * q_scale,
        "sq": jnp.concatenate([zeros(MLA_NOPE), -s16, s16, zeros(LANES - MLA_NOPE - MLA_ROPE)], axis=1) * q_scale,
        "ck": jnp.concatenate([c16, c16, zeros(LANES - MLA_ROPE)], axis=1),
        "sk": jnp.concatenate([-s16, s16, zeros(LANES - MLA_ROPE)], axis=1),
    }
    if t < tm:
        tabs = {k: jnp.tile(v, (tm // t, 1)) for k, v in tabs.items()}
    return tabs


def _group_layer(x, b, t, lw, tabs, past, layer, carried):
    n = b * t
    x = _ffn_ln(x, lw["f1u"], lw["f1d"], lw["ln_g0"], lw["ln_b0"])
    rqk, rv, qm, ckv_all, krp, kr_all, gqk, gv, la, *kv = _in_proj(x, lw, tabs, layer, carried, past is None)

    log_gamma = jnp.log(1.0 - 2.0 ** (-5.0 - jnp.arange(RET_HEADS, dtype=F32)))
    ret_la = jnp.repeat(log_gamma, RET_DK)[None, :]
    s_ret0 = None if past is None else jnp.swapaxes(past[2], -1, -2)
    s_gla0 = None if past is None else jnp.swapaxes(past[3], -1, -2)
    o_r, s_retT = _scan(rqk.reshape(b, t, -1), rv.reshape(b, t, -1), ret_la, s_ret0,
                        heads=RET_HEADS, dk=RET_DK, dv=RET_DV)
    o_g, s_glaT = _scan(gqk.reshape(b, t, -1), gv.reshape(b, t, -1), la.reshape(b, t, -1), s_gla0,
                        heads=GLA_HEADS, dk=GLA_DK, dv=GLA_DV)

    qm = qm.reshape(b, t, -1)
    if past is None:
        o_m = _attention(qm, kv[0].reshape(b, t, -1), kv[1].reshape(b, t, -1), tq=min(ATT_TQ, t), tk=min(ATT_TK, t))
    else:
        o_m = _attention_cached(qm, past[0], past[1], ckv_all, krp, lw, layer)

    x = _out_proj_ln(x, o_r.reshape(n, -1), o_m.reshape(n, -1), o_g.reshape(n, -1), lw, lw["ln_g1"], lw["ln_b1"])
    x = _ffn_ln(x, lw["f2u"], lw["f2d"], lw["ln_g2"], lw["ln_b2"])
    return x, (ckv_all, kr_all), (jnp.swapaxes(s_retT, -1, -2), jnp.swapaxes(s_glaT, -1, -2))


def kernel(x_prompt, x_sample, cache_mla_ckv, cache_mla_krope, state_ret, state_gla, w_in, ret_gn_g, mla_q_norm_g, mla_w_q_up, mla_kv_norm_g, mla_w_kv_up, gla_w_gate_up, gla_b_gate, gla_gn_g, w_ret_o, w_mla_o, w_gla_o, w_out, ffn1_up, ffn1_down, ffn2_up, ffn2_down, ln_g, ln_b):
    w = dict(w_in=w_in, ret_gn_g=ret_gn_g, mla_q_norm_g=mla_q_norm_g, mla_w_q_up=mla_w_q_up,
             mla_kv_norm_g=mla_kv_norm_g, mla_w_kv_up=mla_w_kv_up, gla_w_gate_up=gla_w_gate_up,
             gla_b_gate=gla_b_gate, gla_gn_g=gla_gn_g, w_ret_o=w_ret_o, w_mla_o=w_mla_o, w_gla_o=w_gla_o,
             w_out=w_out, ffn1_up=ffn1_up, ffn1_down=ffn1_down, ffn2_up=ffn2_up, ffn2_down=ffn2_down,
             ln_g=ln_g, ln_b=ln_b)
    bp, tp, _ = x_prompt.shape
    bs, ts, _ = x_sample.shape
    t_past = cache_mla_ckv.shape[2]
    assert t_past % CHUNK == 0 and tp % CHUNK == 0 and ts % CHUNK == 0
    tabs_p = _rope_tables(jnp.arange(tp), min(INP_TM, bp * tp))
    tabs_s = _rope_tables(t_past + jnp.arange(ts), min(INP_TM, bs * ts))
    xp = x_prompt.reshape(bp * tp, D_MODEL)
    xs = x_sample.reshape(bs * ts, D_MODEL)
    carried_p = carried_s = None
    st_p, st_s = [], []
    sw = _prep_weights(w)
    for l in range(DEPTH):
        lw = _layer_params(sw, l)
        xp, carried_p, st = _group_layer(xp, bp, tp, lw, tabs_p, None, l, carried_p)
        st_p.append(st)
        past = (cache_mla_ckv, cache_mla_krope, state_ret[l], state_gla[l])
        xs, carried_s, st = _group_layer(xs, bs, ts, lw, tabs_s, past, l, carried_s)
        st_s.append(st)
    stack = lambda sts, i: jnp.stack([s[i] for s in sts])
    return (xp.reshape(bp, tp, D_MODEL), xs.reshape(bs, ts, D_MODEL),
            carried_p[0].reshape(DEPTH, bp, tp, -1), carried_p[1].reshape(DEPTH, bp, tp, -1),
            stack(st_p, 0), stack(st_p, 1),
            carried_s[0].reshape(DEPTH, bs, ts, -1), carried_s[1].reshape(DEPTH, bs, ts, -1),
            stack(st_s, 0), stack(st_s, 1))
```
